```python
import math
import jax, jax.numpy as jnp
from jax import lax
import numpy as np

D_MODEL = 1024
BATCH = 8
SEQ = 2048
DEPTH = 4

HEAD_DIM = 64
ATT_HEADS = D_MODEL // HEAD_DIM
ATT_WIDTH = ATT_HEADS * HEAD_DIM
ROT_DIM = HEAD_DIM // 4
ROPE_THETA = 500000.0
DILATED_PATTERNS = ((128, 1), (512, 4), (2048, 16))
MAX_REACH = 2048
Q_BLOCK = 128
SSM_WIDTH = D_MODEL // 2
SSM_GROUP = 16
SSM_GROUPS = SSM_WIDTH // SSM_GROUP
SSM_STATE = 64
POOL_WIDTH = 2 * D_MODEL
POOL_WINDOWS = (2, 4, 8, 16)
POOL_GROUP = POOL_WIDTH // len(POOL_WINDOWS)
EVEN_IN = 4 * ATT_WIDTH + 2 * SSM_WIDTH
EVEN_OUT = ATT_WIDTH + SSM_WIDTH
ODD_IN = 2 * POOL_WIDTH
N_EVEN = (DEPTH + 1) // 2
N_ODD = DEPTH // 2
RMS_EPS = 1e-6

kernel_name = "hybrid_dilated_attn_s5_pool_block"


def rmsnorm(x, gain):
    xf = x.astype(jnp.float32)
    y = xf * lax.rsqrt(jnp.mean(xf * xf, axis=-1, keepdims=True) + RMS_EPS)
    return (y * gain.astype(jnp.float32)).astype(x.dtype)


def apply_partial_rope(x, pos):
    half = ROT_DIM // 2
    inv_freq = ROPE_THETA ** (-jnp.arange(0, ROT_DIM, 2, dtype=jnp.float32) / ROT_DIM)
    ang = pos.astype(jnp.float32)[:, None] * inv_freq[None, :]
    cos = jnp.cos(ang)[None, :, None, :]
    sin = jnp.sin(ang)[None, :, None, :]
    xr = x[..., :ROT_DIM].astype(jnp.float32)
    x1, x2 = xr[..., :half], xr[..., half:]
    rot = jnp.concatenate([x1 * cos - x2 * sin, x2 * cos + x1 * sin], axis=-1)
    return jnp.concatenate([rot.astype(x.dtype), x[..., ROT_DIM:]], axis=-1)


def dilated_offsets():
    return np.concatenate([np.arange(w // d + 1) * d for w, d in DILATED_PATTERNS]).astype(np.int32)


def dilated_attention(q, k, v):
    b, s, h, dh = q.shape
    offsets = jnp.asarray(dilated_offsets())
    qh = q.transpose(0, 2, 1, 3)
    pad = ((0, 0), (0, 0), (MAX_REACH, 0), (0, 0))
    k_pad = jnp.pad(k.transpose(0, 2, 1, 3), pad)
    v_pad = jnp.pad(v.transpose(0, 2, 1, 3), pad)

    def block(bi):
        start = bi * Q_BLOCK
        q_blk = lax.dynamic_slice_in_dim(qh, start, Q_BLOCK, axis=2)
        rel = start + jnp.arange(Q_BLOCK, dtype=jnp.int32)[:, None] - offsets[None, :]
        k_g = k_pad[:, :, rel + MAX_REACH]
        v_g = v_pad[:, :, rel + MAX_REACH]
        scores = jnp.einsum('bhqd,bhqkd->bhqk', q_blk, k_g).astype(jnp.float32)
        scores = jnp.where(rel >= 0, scores, -jnp.inf)
        p = jax.nn.softmax(scores, axis=-1)
        return jnp.einsum('bhqk,bhqkd->bhqd', p.astype(v.dtype), v_g)

    out = lax.map(block, jnp.arange(s // Q_BLOCK, dtype=jnp.int32))
    return out.transpose(1, 0, 3, 2, 4).reshape(b, s, h * dh)


def s5_ssm(u, a_re, a_im, log_dt, b_re, b_im, c_re, c_im, d_skip, glu_w, glu_b):
    bsz, s, _ = u.shape
    f32 = jnp.float32
    uf = u.astype(f32)
    ug = uf.reshape(bsz, s, SSM_GROUPS, SSM_GROUP)
    lam = lax.complex(a_re.astype(f32), a_im.astype(f32))
    dt = jnp.exp(log_dt.astype(f32))[:, None]
    lam_bar = jnp.exp(lam * dt)
    b_mat = lax.complex(b_re.astype(f32), b_im.astype(f32))
    b_bar = ((lam_bar - 1.0) / lam)[..., None] * b_mat
    bu = jnp.einsum('bsgh,gph->bsgp', ug.astype(jnp.complex64), b_bar)
    a_seq = jnp.broadcast_to(lam_bar, bu.shape)

    def combine(left, right):
        a_l, s_l = left
        a_r, s_r = right
        return a_r * a_l, a_r * s_l + s_r

    _, states = lax.associative_scan(combine, (a_seq, bu), axis=1)
    c_mat = lax.complex(c_re.astype(f32), c_im.astype(f32))
    y = jnp.real(jnp.einsum('bsgp,ghp->bsgh', states, c_mat)).reshape(bsz, s, SSM_WIDTH)
    y = jax.nn.gelu(y + d_skip.astype(f32) * uf)
    y = y * jax.nn.sigmoid(y @ glu_w.astype(f32) + glu_b.astype(f32))
    return y.astype(u.dtype)


def multiscale_pool(u, pool_w, pool_scale):
    bsz, s, _ = u.shape
    ug = u.astype(jnp.float32).reshape(bsz, s, len(POOL_WINDOWS), POOL_GROUP)
    count_base = jnp.arange(1, s + 1, dtype=jnp.float32)[None, :, None]
    outs = []
    for g, w in enumerate(POOL_WINDOWS):
        ch = ug[:, :, g]
        cs = jnp.cumsum(ch, axis=1)
        lagged = jnp.pad(cs, ((0, 0), (w, 0), (0, 0)))[:, :s]
        mean = (cs - lagged) / jnp.minimum(count_base, float(w))
        outs.append(mean - ch)
    mixed = jnp.stack(outs, axis=2)
    y = jnp.einsum('bsgc,gcd->bsgd', mixed, pool_w.astype(jnp.float32)).reshape(bsz, s, POOL_WIDTH)
    return (y * pool_scale.astype(jnp.float32)).astype(u.dtype)


def even_mixer(h, w_in, w_out, a_re, a_im, log_dt, b_re, b_im, c_re, c_im, d_skip, glu_w, glu_b):
    bsz, s, _ = h.shape
    proj = h @ w_in
    cuts = [ATT_WIDTH, 2 * ATT_WIDTH, 3 * ATT_WIDTH, 4 * ATT_WIDTH, 4 * ATT_WIDTH + SSM_WIDTH]
    q, k, v, g_att, u_ssm, g_ssm = jnp.split(proj, cuts, axis=-1)
    pos = jnp.arange(s, dtype=jnp.int32)
    q = apply_partial_rope(q.reshape(bsz, s, ATT_HEADS, HEAD_DIM), pos) * (HEAD_DIM ** -0.5)
    k = apply_partial_rope(k.reshape(bsz, s, ATT_HEADS, HEAD_DIM), pos)
    v = v.reshape(bsz, s, ATT_HEADS, HEAD_DIM)
    att = dilated_attention(q, k, v)
    ssm = s5_ssm(u_ssm, a_re, a_im, log_dt, b_re, b_im, c_re, c_im, d_skip, glu_w, glu_b)
    merged = jnp.concatenate([att * jax.nn.silu(g_att), ssm * jax.nn.silu(g_ssm)], axis=-1)
    return merged @ w_out


def odd_mixer(h, w_in, pool_w, pool_scale, w_out):
    u, gate = jnp.split(h @ w_in, [POOL_WIDTH], axis=-1)
    y = multiscale_pool(u, pool_w, pool_scale)
    return (y * jax.nn.silu(gate)) @ w_out


def setup_inputs(seed: int = 0) -> dict:
    key = jax.random.key(seed)
    ks = jax.random.split(key, 24)
    f32 = jnp.float32

    def nrm(k, shape, scale):
        return jax.random.normal(k, shape, f32) * scale

    n_idx = jnp.arange(SSM_STATE, dtype=f32)
    return {
        "x": jax.random.normal(ks[0], (BATCH, SEQ, D_MODEL), f32),
        "pre_norm": 1.0 + nrm(ks[1], (DEPTH, D_MODEL), 0.1),
        "post_norm": 1.0 + nrm(ks[2], (DEPTH, D_MODEL), 0.1),
        "even_w_in": nrm(ks[3], (N_EVEN, D_MODEL, EVEN_IN), D_MODEL ** -0.5),
        "even_w_out": nrm(ks[4], (N_EVEN, EVEN_OUT, D_MODEL), EVEN_OUT ** -0.5),
        "ssm_a_re": -0.5 + nrm(ks[5], (N_EVEN, SSM_GROUPS, SSM_STATE), 0.01),
        "ssm_a_im": math.pi * n_idx + nrm(ks[6], (N_EVEN, SSM_GROUPS, SSM_STATE), 0.01),
        "ssm_log_dt": jax.random.uniform(ks[7], (N_EVEN, SSM_GROUPS), f32, math.log(1e-3), math.log(1e-1)),
        "ssm_b_re": nrm(ks[8], (N_EVEN, SSM_GROUPS, SSM_STATE, SSM_GROUP), (2 * SSM_GROUP) ** -0.5),
        "ssm_b_im": nrm(ks[9], (N_EVEN, SSM_GROUPS, SSM_STATE, SSM_GROUP), (2 * SSM_GROUP) ** -0.5),
        "ssm_c_re": nrm(ks[10], (N_EVEN, SSM_GROUPS, SSM_GROUP, SSM_STATE), (2 * SSM_STATE) ** -0.5),
        "ssm_c_im": nrm(ks[11], (N_EVEN, SSM_GROUPS, SSM_GROUP, SSM_STATE), (2 * SSM_STATE) ** -0.5),
        "ssm_d": nrm(ks[12], (N_EVEN, SSM_WIDTH), 1.0),
        "ssm_glu_w": nrm(ks[13], (N_EVEN, SSM_WIDTH, SSM_WIDTH), SSM_WIDTH ** -0.5),
        "ssm_glu_b": nrm(ks[14], (N_EVEN, SSM_WIDTH), 0.02),
        "odd_w_in": nrm(ks[15], (N_ODD, D_MODEL, ODD_IN), D_MODEL ** -0.5),
        "pool_w": nrm(ks[16], (N_ODD, len(POOL_WINDOWS), POOL_GROUP, POOL_GROUP), POOL_GROUP ** -0.5),
        "pool_scale": 1.0 + nrm(ks[17], (N_ODD, POOL_WIDTH), 0.1),
        "odd_w_out": nrm(ks[18], (N_ODD, POOL_WIDTH, D_MODEL), POOL_WIDTH ** -0.5),
    }


def reference(x, pre_norm, post_norm, even_w_in, even_w_out, ssm_a_re, ssm_a_im, ssm_log_dt,
              ssm_b_re, ssm_b_im, ssm_c_re, ssm_c_im, ssm_d, ssm_glu_w, ssm_glu_b,
              odd_w_in, pool_w, pool_scale, odd_w_out):
    for layer in range(DEPTH):
        h = rmsnorm(x, pre_norm[layer])
        i = layer // 2
        if layer % 2 == 0:
            y = even_mixer(h, even_w_in[i], even_w_out[i], ssm_a_re[i], ssm_a_im[i], ssm_log_dt[i],
                           ssm_b_re[i], ssm_b_im[i], ssm_c_re[i], ssm_c_im[i], ssm_d[i],
                           ssm_glu_w[i], ssm_glu_b[i])
        else:
            y = odd_mixer(h, odd_w_in[i], pool_w[i], pool_scale[i], odd_w_out[i])
        x = x + rmsnorm(y, post_norm[layer])
    return x
```

```python
import functools
import math

import jax
import jax.numpy as jnp
from jax import lax
from jax.experimental import pallas as pl
from jax.experimental.pallas import tpu as pltpu

F32 = jnp.float32
BF16 = jnp.bfloat16

D_MODEL = 1024
BATCH = 8
SEQ = 2048
DEPTH = 4
HEAD_DIM = 64
ATT_WIDTH = 1024
ROT_DIM = 16
ROPE_THETA = 500000.0
SSM_WIDTH = 512
SSM_GROUP = 16
SSM_GROUPS = 32
SSM_STATE = 64
N_STATE = SSM_GROUPS * SSM_STATE
POOL_WIDTH = 2048
POOL_WINDOWS = (2, 4, 8, 16)
POOL_GROUP = 512
EVEN_IN = 4 * ATT_WIDTH + 2 * SSM_WIDTH
RMS_EPS = 1e-6

LANES = 128
ATT_CHUNK = 128
NEG_BIG = -1e30
VMEM_LIMIT = 56 * 1024 * 1024

ROW_TILE = 512
SSM_TC = 64
SSM_COLS = 512
ODD_TILE = 256
POOL_HIST = 32


def _sigmoid(x):
    return 1.0 / (1.0 + jnp.exp(-x))


def _silu(x):
    return x * _sigmoid(x)


def _rms(x, gain):
    return x * lax.rsqrt(jnp.mean(x * x, axis=-1, keepdims=True) + RMS_EPS) * gain


def _const_spec(shape):
    nd = len(shape)
    return pl.BlockSpec(shape, lambda *_: (0,) * nd, pipeline_mode=pl.Buffered(1))


def _even_in_kernel(x_ref, gain_ref, w_ref, cos_ref, sa_ref, sb_ref,
                    q_ref, k_ref, v_ref, g_ref, u_ref, gs_ref):
    hb = _rms(x_ref[...], gain_ref[...]).astype(BF16)
    cos = cos_ref[...]
    sa = sa_ref[...]
    sb = sb_ref[...]

    def proj(lo, width):
        return jnp.dot(hb, w_ref[:, lo:lo + width], preferred_element_type=F32)

    def rope_store(dst, y, scale):
        for c in range(ATT_WIDTH // LANES):
            yc = y[:, c * LANES:(c + 1) * LANES]
            r = (yc * cos + pltpu.roll(yc, LANES - ROT_DIM // 2, 1) * sa
                 + pltpu.roll(yc, ROT_DIM // 2, 1) * sb)
            dst[:, c * LANES:(c + 1) * LANES] = (r * scale).astype(dst.dtype)

    rope_store(q_ref, proj(0, ATT_WIDTH), HEAD_DIM ** -0.5)
    rope_store(k_ref, proj(ATT_WIDTH, ATT_WIDTH), 1.0)
    v_ref[...] = proj(2 * ATT_WIDTH, ATT_WIDTH).astype(v_ref.dtype)
    g_ref[...] = proj(3 * ATT_WIDTH, ATT_WIDTH).astype(g_ref.dtype)
    us = proj(4 * ATT_WIDTH, 2 * SSM_WIDTH)
    u_ref[...] = us[:, :SSM_WIDTH]
    gs_ref[...] = us[:, SSM_WIDTH:]


def _even_in(x2, gain, w_in, cos_t, sa_t, sb_t):
    tm = ROW_TILE
    nj = SEQ // tm
    row_spec = pl.BlockSpec((tm, D_MODEL), lambda b, j: (b * nj + j, 0))
    tab_spec = pl.BlockSpec((tm, LANES), lambda b, j: (j, 0))
    att_spec = pl.BlockSpec((tm, ATT_WIDTH), lambda b, j: (b * nj + j, 0))
    ssm_spec = pl.BlockSpec((tm, SSM_WIDTH), lambda b, j: (j, b))
    att_shape = jax.ShapeDtypeStruct((BATCH * SEQ, ATT_WIDTH), BF16)
    ssm_shape = jax.ShapeDtypeStruct((SEQ, BATCH * SSM_WIDTH), F32)
    return pl.pallas_call(
        _even_in_kernel,
        grid=(BATCH, nj),
        in_specs=[row_spec, _const_spec((1, D_MODEL)), _const_spec((D_MODEL, EVEN_IN)),
                  tab_spec, tab_spec, tab_spec],
        out_specs=[att_spec, att_spec, att_spec, att_spec, ssm_spec, ssm_spec],
        out_shape=[att_shape, att_shape, att_shape, att_shape, ssm_shape, ssm_shape],
        compiler_params=pltpu.CompilerParams(
            dimension_semantics=("arbitrary", "arbitrary"), vmem_limit_bytes=VMEM_LIMIT),
        name="even_in_proj",
    )(x2, gain, w_in, cos_t, sa_t, sb_t)


def _attn_kernel(q_ref, k_ref, v_ref, g_ref, o_ref,
                 q32, k32, v32, qp, kp, vp, m_a, m_b, l_a, l_b, acc):
    ch = ATT_CHUNK
    lane = lax.broadcasted_iota(jnp.int32, (1, LANES), 1)
    is_a = lane < HEAD_DIM
    row1 = lax.broadcasted_iota(jnp.int32, (ch, ch), 0)
    col1 = lax.broadcasted_iota(jnp.int32, (ch, ch), 1)
    bias_cur = jnp.where(col1 <= row1, 0.0, NEG_BIG).astype(F32)
    bias_prev = jnp.where(col1 >= row1, 0.0, NEG_BIG).astype(F32)
    bias_full = jnp.concatenate([bias_prev, bias_cur], axis=1)

    q32[...] = q_ref[0].astype(F32)
    k32[...] = k_ref[0].astype(F32)
    v32[...] = v_ref[0].astype(F32)

    def step(qc, kc, vc, state_idx, has_state, finalize_rows):
        nk = kc.shape[0]
        bias = bias_full if nk == 2 * ch else bias_cur
        zero = jnp.zeros_like(qc)
        zero_v = jnp.zeros_like(vc)
        new = []
        pv = None
        for is_h, m_ref, l_ref in ((is_a, m_a, l_a), (~is_a, m_b, l_b)):
            qh = jnp.where(is_h, qc, zero)
            s = lax.dot_general(qh, kc, (((1,), (1,)), ((), ())),
                                preferred_element_type=F32) + bias
            m_c = jnp.max(s, axis=1, keepdims=True)
            if has_state:
                m_prev = m_ref[state_idx, :]
                m_new = jnp.maximum(m_prev, m_c)
            else:
                m_new = jnp.broadcast_to(m_c, (ch, LANES))
            m_k = jnp.concatenate([m_new] * (nk // LANES), axis=1)
            p = jnp.exp(s - m_k)
            l_c = jnp.sum(p, axis=1, keepdims=True)
            if has_state:
                alpha = jnp.exp(m_prev - m_new)
                l_new = alpha * l_ref[state_idx, :] + l_c
            else:
                alpha = None
                l_new = jnp.broadcast_to(l_c, (ch, LANES))
            vh = jnp.where(is_h, vc, zero_v)
            pv_h = jnp.dot(p.astype(BF16), vh, preferred_element_type=F32)
            pv = pv_h if pv is None else pv + pv_h
            new.append((m_new, l_new, alpha))
        if has_state:
            alpha_ab = jnp.where(is_a, new[0][2], new[1][2])
            acc_new = alpha_ab * acc[state_idx, :] + pv
        else:
            acc_new = pv
        if finalize_rows is None:
            m_a[state_idx, :] = new[0][0]
            m_b[state_idx, :] = new[1][0]
            l_a[state_idx, :] = new[0][1]
            l_b[state_idx, :] = new[1][1]
            acc[state_idx, :] = acc_new
        else:
            l_ab = jnp.where(is_a, new[0][1], new[1][1])
            gate = _silu(g_ref[0, finalize_rows, :].astype(F32))
            o_ref[0, finalize_rows, :] = (acc_new / l_ab * gate).astype(o_ref.dtype)

    def run_classes(stride, first):
        lc = SEQ // stride
        for r in range(stride):
            qp[r * lc:(r + 1) * lc, :] = q32[pl.ds(r, lc, stride=stride), :].astype(BF16)
            kp[r * lc:(r + 1) * lc, :] = k32[pl.ds(r, lc, stride=stride), :].astype(BF16)
            vp[r * lc:(r + 1) * lc, :] = v32[pl.ds(r, lc, stride=stride), :].astype(BF16)
        for r in range(stride):
            for ic in range(lc // ch):
                base = r * lc + ic * ch
                lo = base - ch if ic > 0 else base
                step(qp[base:base + ch, :], kp[lo:base + ch, :], vp[lo:base + ch, :],
                     pl.ds(stride * ch * ic + r, ch, stride=stride), not first, None)

    run_classes(16, True)
    run_classes(4, False)
    for i in range(SEQ // ch):
        base = i * ch
        lo = base - ch if i > 0 else base
        rows = pl.ds(base, ch)
        step(q_ref[0, rows, :], k_ref[0, lo:base + ch, :], v_ref[0, lo:base + ch, :],
             rows, True, rows)


def _attention(q, k, v, g):
    blk = pl.BlockSpec((1, SEQ, LANES), lambda b, h: (b, 0, h))
    f32_scr = pltpu.VMEM((SEQ, LANES), F32)
    bf_scr = pltpu.VMEM((SEQ, LANES), BF16)
    return pl.pallas_call(
        _attn_kernel,
        grid=(BATCH, ATT_WIDTH // LANES),
        in_specs=[blk, blk, blk, blk],
        out_specs=blk,
        out_shape=jax.ShapeDtypeStruct((BATCH, SEQ, ATT_WIDTH), BF16),
        scratch_shapes=[f32_scr, f32_scr, f32_scr, bf_scr, bf_scr, bf_scr,
                        f32_scr, f32_scr, f32_scr, f32_scr, f32_scr],
        compiler_params=pltpu.CompilerParams(
            dimension_semantics=("arbitrary", "arbitrary"), vmem_limit_bytes=VMEM_LIMIT),
        name="dilated_attention",
    )(q, k, v, g)


def _ssm_kernel(u_ref, gs_ref, lam_re_ref, lam_im_ref, wb_ref, wc_re_ref, wc_im_ref,
                d_ref, gw_ref, gb_ref, o_ref, xr, xi, st_re, st_im, y_s):
    tc = SSM_TC
    rows = tc * BATCH
    n_chunk = SSM_WIDTH // LANES
    cpc = N_STATE // n_chunk

    @pl.when(pl.program_id(0) == 0)
    def _():
        st_re[...] = jnp.zeros_like(st_re)
        st_im[...] = jnp.zeros_like(st_im)

    u = u_ref[...].reshape(rows, SSM_WIDTH)
    ub = u.astype(BF16)
    for c in range(n_chunk):
        bu = jnp.dot(ub[:, c * LANES:(c + 1) * LANES], wb_ref[c], preferred_element_type=F32)
        xr[:, c * cpc:(c + 1) * cpc] = bu[:, :cpc]
        xi[:, c * cpc:(c + 1) * cpc] = bu[:, cpc:]

    for cc in range(N_STATE // SSM_COLS):
        cols = slice(cc * SSM_COLS, (cc + 1) * SSM_COLS)
        lr = lam_re_ref[:, cols]
        li = lam_im_ref[:, cols]

        def body(t, carry):
            sr, si = carry
            r0 = pl.multiple_of(t * BATCH, BATCH)
            nr = lr * sr - li * si + xr[pl.ds(r0, BATCH), cols]
            ni = lr * si + li * sr + xi[pl.ds(r0, BATCH), cols]
            xr[pl.ds(r0, BATCH), cols] = nr
            xi[pl.ds(r0, BATCH), cols] = ni
            return nr, ni

        sr, si = lax.fori_loop(0, tc, body, (st_re[:, cols], st_im[:, cols]), unroll=8)
        st_re[:, cols] = sr
        st_im[:, cols] = si

    for c in range(n_chunk):
        y_s[:, c * LANES:(c + 1) * LANES] = (
            jnp.dot(xr[:, c * cpc:(c + 1) * cpc].astype(BF16), wc_re_ref[c],
                    preferred_element_type=F32)
            + jnp.dot(xi[:, c * cpc:(c + 1) * cpc].astype(BF16), wc_im_ref[c],
                      preferred_element_type=F32))
    y = jax.nn.gelu(y_s[...] + d_ref[...] * u)
    z = jnp.dot(y.astype(BF16), gw_ref[...], preferred_element_type=F32) + gb_ref[...]
    gs = gs_ref[...].reshape(rows, SSM_WIDTH)
    o_ref[...] = (y * _sigmoid(z) * _silu(gs)).reshape(tc, BATCH, SSM_WIDTH)


def _ssm(u3, gs3, lam_re, lam_im, wb, wc_re, wc_im, d_skip, glu_w, glu_b):
    tc = SSM_TC
    rows = tc * BATCH
    blk = pl.BlockSpec((tc, BATCH, SSM_WIDTH), lambda t: (t, 0, 0))
    return pl.pallas_call(
        _ssm_kernel,
        grid=(SEQ // tc,),
        in_specs=[blk, blk,
                  _const_spec((BATCH, N_STATE)), _const_spec((BATCH, N_STATE)),
                  _const_spec(wb.shape), _const_spec(wc_re.shape), _const_spec(wc_im.shape),
                  _const_spec((1, SSM_WIDTH)), _const_spec((SSM_WIDTH, SSM_WIDTH)),
                  _const_spec((1, SSM_WIDTH))],
        out_specs=blk,
        out_shape=jax.ShapeDtypeStruct((SEQ, BATCH, SSM_WIDTH), F32),
        scratch_shapes=[pltpu.VMEM((rows, N_STATE), F32), pltpu.VMEM((rows, N_STATE), F32),
                        pltpu.VMEM((BATCH, N_STATE), F32), pltpu.VMEM((BATCH, N_STATE), F32),
                        pltpu.VMEM((rows, SSM_WIDTH), F32)],
        compiler_params=pltpu.CompilerParams(
            dimension_semantics=("arbitrary",), vmem_limit_bytes=VMEM_LIMIT),
        name="s5_branch",
    )(u3, gs3, lam_re, lam_im, wb, wc_re, wc_im, d_skip, glu_w, glu_b)


def _even_out_kernel(x_ref, att_ref, ssm_ref, w_ref, gain_ref, o_ref):
    y = jnp.dot(att_ref[...], w_ref[:ATT_WIDTH, :], preferred_element_type=F32)
    y = y + jnp.dot(ssm_ref[...].astype(BF16), w_ref[ATT_WIDTH:, :],
                    preferred_element_type=F32)
    o_ref[...] = x_ref[...] + _rms(y, gain_ref[...])


def _even_out(x2, att2, ssm2, w_out, gain):
    tm = ROW_TILE
    nj = SEQ // tm
    row_spec = pl.BlockSpec((tm, D_MODEL), lambda b, j: (b * nj + j, 0))
    return pl.pallas_call(
        _even_out_kernel,
        grid=(BATCH, nj),
        in_specs=[row_spec, pl.BlockSpec((tm, ATT_WIDTH), lambda b, j: (b * nj + j, 0)),
                  pl.BlockSpec((tm, SSM_WIDTH), lambda b, j: (j, b)),
                  _const_spec(w_out.shape), _const_spec((1, D_MODEL))],
        out_specs=row_spec,
        out_shape=jax.ShapeDtypeStruct((BATCH * SEQ, D_MODEL), F32),
        compiler_params=pltpu.CompilerParams(
            dimension_semantics=("arbitrary", "arbitrary"), vmem_limit_bytes=VMEM_LIMIT),
        name="even_out_proj",
    )(x2, att2, ssm2, w_out, gain)


def _odd_kernel(x_ref, pre_ref, w_in_ref, pw_ref, ps_ref, w_out_ref, post_ref, o_ref,
                ext, tmp0, tmp1):
    tm = ODD_TILE
    hist = POOL_HIST
    total = hist + tm
    j = pl.program_id(1)

    @pl.when(j == 0)
    def _():
        ext[0:hist, :] = jnp.zeros((hist, POOL_WIDTH), F32)

    x = x_ref[...]
    hb = _rms(x, pre_ref[...]).astype(BF16)
    ext[hist:total, :] = jnp.dot(hb, w_in_ref[:, :POOL_WIDTH], preferred_element_type=F32)

    t_idx = j * tm + lax.broadcasted_iota(jnp.int32, (tm, 1), 0)
    count_base = (t_idx + 1).astype(F32)
    tmps = (tmp0, tmp1)
    y = None
    for g, w in enumerate(POOL_WINDOWS):
        cols = slice(g * POOL_GROUP, (g + 1) * POOL_GROUP)
        levels = int(math.log2(w))
        src_ref, src_cols = ext, cols
        for lev in range(levels):
            shift = 2 ** lev
            start = hist - 8 * (levels - 1 - lev)
            val = (src_ref[start:total, src_cols]
                   + src_ref[start - shift:total - shift, src_cols])
            if lev < levels - 1:
                dst = tmps[lev % 2]
                dst[start:total, :] = val
                src_ref, src_cols = dst, slice(None)
        u_g = ext[hist:total, cols]
        mixed = val / jnp.minimum(count_base, float(w)) - u_g
        yg = jnp.dot(mixed.astype(BF16), pw_ref[g], preferred_element_type=F32)
        gate = jnp.dot(hb, w_in_ref[:, POOL_WIDTH + g * POOL_GROUP:
                                    POOL_WIDTH + (g + 1) * POOL_GROUP],
                       preferred_element_type=F32)
        yg = yg * ps_ref[:, cols] * _silu(gate)
        part = jnp.dot(yg.astype(BF16), w_out_ref[cols, :], preferred_element_type=F32)
        y = part if y is None else y + part
    ext[0:hist, :] = ext[tm:total, :]
    o_ref[...] = x + _rms(y, post_ref[...])


def _odd_layer(x2, pre, w_in, pool_w, pool_scale, w_out, post):
    tm = ODD_TILE
    nj = SEQ // tm
    row_spec = pl.BlockSpec((tm, D_MODEL), lambda b, j: (b * nj + j, 0))
    return pl.pallas_call(
        _odd_kernel,
        grid=(BATCH, nj),
        in_specs=[row_spec, _const_spec((1, D_MODEL)), _const_spec(w_in.shape),
                  _const_spec(pool_w.shape), _const_spec((1, POOL_WIDTH)),
                  _const_spec(w_out.shape), _const_spec((1, D_MODEL))],
        out_specs=row_spec,
        out_shape=jax.ShapeDtypeStruct((BATCH * SEQ, D_MODEL), F32),
        scratch_shapes=[pltpu.VMEM((POOL_HIST + tm, POOL_WIDTH), F32),
                        pltpu.VMEM((POOL_HIST + tm, POOL_GROUP), F32),
                        pltpu.VMEM((POOL_HIST + tm, POOL_GROUP), F32)],
        compiler_params=pltpu.CompilerParams(
            dimension_semantics=("arbitrary", "arbitrary"), vmem_limit_bytes=VMEM_LIMIT),
        name="odd_layer",
    )(x2, pre, w_in, pool_w, pool_scale, w_out, post)


def _rope_tables():
    half = ROT_DIM // 2
    inv_freq = ROPE_THETA ** (-jnp.arange(0, ROT_DIM, 2, dtype=F32) / ROT_DIM)
    ang = jnp.arange(SEQ, dtype=jnp.int32).astype(F32)[:, None] * inv_freq[None, :]
    cos, sin = jnp.cos(ang), jnp.sin(ang)
    zeros = jnp.zeros((SEQ, HEAD_DIM - ROT_DIM), F32)
    zeros_h = jnp.zeros((SEQ, half), F32)
    cos_h = jnp.concatenate([cos, cos, jnp.ones_like(zeros)], axis=1)
    sa_h = jnp.concatenate([-sin, zeros_h, zeros], axis=1)
    sb_h = jnp.concatenate([zeros_h, sin, zeros], axis=1)
    rep = LANES // HEAD_DIM
    return (jnp.tile(cos_h, (1, rep)), jnp.tile(sa_h, (1, rep)), jnp.tile(sb_h, (1, rep)))


def _ssm_params(a_re, a_im, log_dt, b_re, b_im, c_re, c_im):
    lam = lax.complex(a_re.astype(F32), a_im.astype(F32))
    dt = jnp.exp(log_dt.astype(F32))[:, None]
    lam_bar = jnp.exp(lam * dt)
    b_bar = ((lam_bar - 1.0) / lam)[..., None] * lax.complex(b_re.astype(F32), b_im.astype(F32))
    lam_re = jnp.broadcast_to(jnp.real(lam_bar).reshape(1, N_STATE), (BATCH, N_STATE))
    lam_im = jnp.broadcast_to(jnp.imag(lam_bar).reshape(1, N_STATE), (BATCH, N_STATE))
    gpc = LANES // SSM_GROUP
    n_chunk = SSM_GROUPS // gpc
    eye = jnp.eye(gpc, dtype=F32)

    def block_diag_in(m):
        m = m.reshape(n_chunk, gpc, SSM_STATE, SSM_GROUP)
        return jnp.einsum('cgph,gk->cghkp', m, eye).reshape(
            n_chunk, gpc * SSM_GROUP, gpc * SSM_STATE)

    def block_diag_out(m):
        m = m.reshape(n_chunk, gpc, SSM_GROUP, SSM_STATE)
        return jnp.einsum('cghp,gk->cgpkh', m, eye).reshape(
            n_chunk, gpc * SSM_STATE, gpc * SSM_GROUP)

    wb = jnp.concatenate([block_diag_in(jnp.real(b_bar)), block_diag_in(jnp.imag(b_bar))],
                         axis=2).astype(BF16)
    wc_re = block_diag_out(c_re.astype(F32)).astype(BF16)
    wc_im = block_diag_out(-c_im.astype(F32)).astype(BF16)
    return lam_re, lam_im, wb, wc_re, wc_im


def kernel(x, pre_norm, post_norm, even_w_in, even_w_out, ssm_a_re, ssm_a_im, ssm_log_dt,
           ssm_b_re, ssm_b_im, ssm_c_re, ssm_c_im, ssm_d, ssm_glu_w, ssm_glu_b,
           odd_w_in, pool_w, pool_scale, odd_w_out):
    cos_t, sa_t, sb_t = _rope_tables()
    x2 = x.reshape(BATCH * SEQ, D_MODEL)
    for layer in range(DEPTH):
        i = layer // 2
        pre = pre_norm[layer].reshape(1, D_MODEL)
        post = post_norm[layer].reshape(1, D_MODEL)
        if layer % 2 == 0:
            q, k, v, g, u, gs = _even_in(x2, pre, even_w_in[i].astype(BF16), cos_t, sa_t, sb_t)
            shape3 = (BATCH, SEQ, ATT_WIDTH)
            att = _attention(q.reshape(shape3), k.reshape(shape3), v.reshape(shape3),
                             g.reshape(shape3))
            lam_re, lam_im, wb, wc_re, wc_im = _ssm_params(
                ssm_a_re[i], ssm_a_im[i], ssm_log_dt[i], ssm_b_re[i], ssm_b_im[i],
                ssm_c_re[i], ssm_c_im[i])
            ssm = _ssm(u.reshape(SEQ, BATCH, SSM_WIDTH), gs.reshape(SEQ, BATCH, SSM_WIDTH),
                       lam_re, lam_im, wb, wc_re, wc_im,
                       ssm_d[i].reshape(1, SSM_WIDTH), ssm_glu_w[i].astype(BF16),
                       ssm_glu_b[i].reshape(1, SSM_WIDTH))
            x2 = _even_out(x2, att.reshape(BATCH * SEQ, ATT_WIDTH),
                           ssm.reshape(SEQ, BATCH * SSM_WIDTH), even_w_out[i].astype(BF16), post)
        else:
            x2 = _odd_layer(x2, pre, odd_w_in[i].astype(BF16), pool_w[i].astype(BF16),
                            pool_scale[i].reshape(1, POOL_WIDTH), odd_w_out[i].astype(BF16), post)
    return x2.reshape(BATCH, SEQ, D_MODEL)
```

```python
import math

import jax
import jax.numpy as jnp
from jax import lax
from jax.experimental import pallas as pl
from jax.experimental.pallas import tpu as pltpu

F32 = jnp.float32
BF16 = jnp.bfloat16

D_MODEL = 1024
BATCH = 8
SEQ = 2048
DEPTH = 4
HEAD_DIM = 64
ATT_WIDTH = 1024
ROT_DIM = 16
ROPE_THETA = 500000.0
SSM_WIDTH = 512
SSM_GROUP = 16
SSM_GROUPS = 32
SSM_STATE = 64
N_STATE = SSM_GROUPS * SSM_STATE
POOL_WIDTH = 2048
POOL_WINDOWS = (2, 4, 8, 16)
POOL_GROUP = 512
EVEN_IN = 4 * ATT_WIDTH + 2 * SSM_WIDTH
RMS_EPS = 1e-6
LOG2_E = math.log2(math.e)

LANES = 128
ATT_CHUNK = 128
ATT_GROUP = 4
NEG_BIG = -1e30
VMEM_LIMIT = 56 * 1024 * 1024

ROW_TILE = 512
SSM_TC = 64
SSM_COLS = 512
ODD_TILE = 256
POOL_HIST = 32


def _sigmoid(x):
    return 1.0 / (1.0 + jnp.exp(-x))


def _silu(x):
    return x * _sigmoid(x)


def _rms(x, gain):
    return x * lax.rsqrt(jnp.mean(x * x, axis=-1, keepdims=True) + RMS_EPS) * gain


def _const_spec(shape):
    nd = len(shape)
    return pl.BlockSpec(shape, lambda *_: (0,) * nd, pipeline_mode=pl.Buffered(1))


def _even_in_kernel(x_ref, gain_ref, w_ref, cos_ref, sa_ref, sb_ref,
                    q_ref, k_ref, v_ref, g_ref, us_ref):
    hb = _rms(x_ref[...], gain_ref[...]).astype(BF16)
    cos = cos_ref[...]
    sa = sa_ref[...]
    sb = sb_ref[...]

    def proj(lo, width):
        return jnp.dot(hb, w_ref[:, lo:lo + width], preferred_element_type=F32)

    def rope_store(dst, y, scale):
        for c in range(ATT_WIDTH // LANES):
            yc = y[:, c * LANES:(c + 1) * LANES]
            r = (yc * cos + pltpu.roll(yc, LANES - ROT_DIM // 2, 1) * sa
                 + pltpu.roll(yc, ROT_DIM // 2, 1) * sb)
            dst[:, c * LANES:(c + 1) * LANES] = (r * scale).astype(dst.dtype)

    rope_store(q_ref, proj(0, ATT_WIDTH), HEAD_DIM ** -0.5 * LOG2_E)
    rope_store(k_ref, proj(ATT_WIDTH, ATT_WIDTH), 1.0)
    v_ref[...] = proj(2 * ATT_WIDTH, ATT_WIDTH).astype(v_ref.dtype)
    g_ref[...] = proj(3 * ATT_WIDTH, ATT_WIDTH).astype(g_ref.dtype)
    us_ref[...] = proj(4 * ATT_WIDTH, 2 * SSM_WIDTH).astype(us_ref.dtype)


def _even_in(x2, gain, w_in, cos_t, sa_t, sb_t):
    tm = ROW_TILE
    nj = SEQ // tm
    row_spec = pl.BlockSpec((tm, D_MODEL), lambda b, j: (b * nj + j, 0))
    tab_spec = pl.BlockSpec((tm, LANES), lambda b, j: (j, 0))
    att_spec = pl.BlockSpec((tm, ATT_WIDTH), lambda b, j: (b * nj + j, 0))
    att_shape = jax.ShapeDtypeStruct((BATCH * SEQ, ATT_WIDTH), BF16)
    return pl.pallas_call(
        _even_in_kernel,
        grid=(BATCH, nj),
        in_specs=[row_spec, _const_spec((1, D_MODEL)), _const_spec((D_MODEL, EVEN_IN)),
                  tab_spec, tab_spec, tab_spec],
        out_specs=[att_spec] * 5,
        out_shape=[att_shape] * 5,
        compiler_params=pltpu.CompilerParams(
            dimension_semantics=("arbitrary", "arbitrary"), vmem_limit_bytes=VMEM_LIMIT),
        name="even_in_proj",
    )(x2, gain, w_in, cos_t, sa_t, sb_t)


def _attn_kernel(q_ref, k_ref, v_ref, g_ref, o_ref,
                 q32, k32, v32, qp, kp, vp, m_a, m_b, acc_a, acc_b):
    ch = ATT_CHUNK
    lane = lax.broadcasted_iota(jnp.int32, (1, LANES), 1)
    is_a = lane < HEAD_DIM
    heads = ((is_a, m_a, acc_a), (jnp.logical_not(is_a), m_b, acc_b))
    row1 = lax.broadcasted_iota(jnp.int32, (ch, ch), 0)
    col1 = lax.broadcasted_iota(jnp.int32, (ch, ch), 1)
    bias_cur = jnp.where(col1 <= row1, 0.0, NEG_BIG).astype(F32)
    bias_prev = jnp.where(col1 >= row1, 0.0, NEG_BIG).astype(F32)
    bias_full = jnp.concatenate([bias_prev, bias_cur], axis=1)
    ones_row = jnp.ones((1, LANES), BF16)
    zeros_row = jnp.zeros((1, LANES), BF16)

    q32[...] = q_ref[0].astype(F32)
    k32[...] = k_ref[0].astype(F32)
    v32[...] = v_ref[0].astype(F32)

    def process(items, has_state, finalize):
        scores = []
        for qc, kc, vc, sidx in items:
            bias = bias_full if kc.shape[0] == 2 * ch else bias_cur
            for is_h, m_ref, acc_ref in heads:
                qh = jnp.where(is_h, qc, zeros_row)
                s = lax.dot_general(qh, kc, (((1,), (1,)), ((), ())),
                                    preferred_element_type=F32) + bias
                scores.append((s, jnp.max(s, axis=1, keepdims=True)))
        partial = []
        n = 0
        for qc, kc, vc, sidx in items:
            for is_h, m_ref, acc_ref in heads:
                s, m_c = scores[n]
                n += 1
                if has_state:
                    m_prev = m_ref[sidx, :]
                    m_new = jnp.maximum(m_prev, m_c)
                else:
                    m_prev = None
                    m_new = jnp.broadcast_to(m_c, (ch, LANES))
                m_k = jnp.concatenate([m_new] * (kc.shape[0] // LANES), axis=1)
                p = jnp.exp2(s - m_k)
                vh = jnp.where(is_h, vc, ones_row)
                pv = jnp.dot(p.astype(BF16), vh, preferred_element_type=F32)
                partial.append((m_prev, m_new, pv))
        n = 0
        for qc, kc, vc, sidx in items:
            res = []
            for is_h, m_ref, acc_ref in heads:
                m_prev, m_new, pv = partial[n]
                n += 1
                if has_state:
                    a_new = jnp.exp2(m_prev - m_new) * acc_ref[sidx, :] + pv
                else:
                    a_new = pv
                res.append(a_new)
                if not finalize:
                    m_ref[sidx, :] = m_new
                    acc_ref[sidx, :] = a_new
            if finalize:
                num = jnp.where(is_a, res[0], res[1])
                den = jnp.where(is_a, pltpu.roll(res[0], HEAD_DIM, 1),
                                pltpu.roll(res[1], HEAD_DIM, 1))
                gate = _silu(g_ref[0, sidx, :].astype(F32))
                o_ref[0, sidx, :] = (num / den * gate).astype(o_ref.dtype)

    def run_classes(stride, first):
        lc = SEQ // stride
        for r in range(stride):
            qp[r * lc:(r + 1) * lc, :] = q32[pl.ds(r, lc, stride=stride), :].astype(BF16)
            kp[r * lc:(r + 1) * lc, :] = k32[pl.ds(r, lc, stride=stride), :].astype(BF16)
            vp[r * lc:(r + 1) * lc, :] = v32[pl.ds(r, lc, stride=stride), :].astype(BF16)
        items = []
        for r in range(stride):
            for ic in range(lc // ch):
                base = r * lc + ic * ch
                lo = base - ch if ic > 0 else base
                items.append((qp[base:base + ch, :], kp[lo:base + ch, :], vp[lo:base + ch, :],
                              pl.ds(stride * ch * ic + r, ch, stride=stride)))
        for i in range(0, len(items), ATT_GROUP):
            process(items[i:i + ATT_GROUP], not first, False)

    run_classes(16, True)
    run_classes(4, False)
    items = []
    for i in range(SEQ // ch):
        base = i * ch
        lo = base - ch if i > 0 else base
        items.append((q_ref[0, base:base + ch, :], k_ref[0, lo:base + ch, :],
                      v_ref[0, lo:base + ch, :], pl.ds(base, ch)))
    for i in range(0, len(items), ATT_GROUP):
        process(items[i:i + ATT_GROUP], True, True)


def _attention(q, k, v, g):
    blk = pl.BlockSpec((1, SEQ, LANES), lambda b, h: (b, 0, h))
    f32_scr = pltpu.VMEM((SEQ, LANES), F32)
    bf_scr = pltpu.VMEM((SEQ, LANES), BF16)
    return pl.pallas_call(
        _attn_kernel,
        grid=(BATCH, ATT_WIDTH // LANES),
        in_specs=[blk, blk, blk, blk],
        out_specs=blk,
        out_shape=jax.ShapeDtypeStruct((BATCH, SEQ, ATT_WIDTH), BF16),
        scratch_shapes=[f32_scr, f32_scr, f32_scr, bf_scr, bf_scr, bf_scr,
                        f32_scr, f32_scr, f32_scr, f32_scr],
        compiler_params=pltpu.CompilerParams(
            dimension_semantics=("arbitrary", "arbitrary"), vmem_limit_bytes=VMEM_LIMIT),
        name="dilated_attention",
    )(q, k, v, g)


def _ssm_kernel(us_ref, perm_ref, lam_re_ref, lam_im_ref, wb_ref, wc_re_ref, wc_im_ref,
                d_ref, gw_ref, gb_ref, o_ref, xr, xi, st_re, st_im, y_tb, y_bt):
    tc = SSM_TC
    rows = tc * BATCH
    n_chunk = SSM_WIDTH // LANES
    cpc = N_STATE // n_chunk

    @pl.when(pl.program_id(0) == 0)
    def _():
        st_re[...] = jnp.zeros_like(st_re)
        st_im[...] = jnp.zeros_like(st_im)

    ub = us_ref[:, :, :SSM_WIDTH].reshape(rows, SSM_WIDTH)
    u_tb = jnp.dot(perm_ref[...], ub, preferred_element_type=F32).astype(BF16)
    for c in range(n_chunk):
        bu = jnp.dot(u_tb[:, c * LANES:(c + 1) * LANES], wb_ref[c], preferred_element_type=F32)
        xr[:, c * cpc:(c + 1) * cpc] = bu[:, :cpc]
        xi[:, c * cpc:(c + 1) * cpc] = bu[:, cpc:]

    for cc in range(N_STATE // SSM_COLS):
        cols = slice(cc * SSM_COLS, (cc + 1) * SSM_COLS)
        lr = lam_re_ref[:, cols]
        li = lam_im_ref[:, cols]

        def body(t, carry):
            sr, si = carry
            r0 = pl.multiple_of(t * BATCH, BATCH)
            nr = lr * sr - li * si + xr[pl.ds(r0, BATCH), cols]
            ni = lr * si + li * sr + xi[pl.ds(r0, BATCH), cols]
            xr[pl.ds(r0, BATCH), cols] = nr
            xi[pl.ds(r0, BATCH), cols] = ni
            return nr, ni

        sr, si = lax.fori_loop(0, tc, body, (st_re[:, cols], st_im[:, cols]), unroll=8)
        st_re[:, cols] = sr
        st_im[:, cols] = si

    for c in range(n_chunk):
        y_tb[c] = (
            jnp.dot(xr[:, c * cpc:(c + 1) * cpc].astype(BF16), wc_re_ref[c],
                    preferred_element_type=F32)
            + jnp.dot(xi[:, c * cpc:(c + 1) * cpc].astype(BF16), wc_im_ref[c],
                      preferred_element_type=F32))
        for b in range(BATCH):
            y_bt[b * tc:(b + 1) * tc, c * LANES:(c + 1) * LANES] = (
                y_tb[c, pl.ds(b, tc, stride=BATCH), :])
    u = ub.astype(F32)
    y = jax.nn.gelu(y_bt[...] + d_ref[...] * u)
    z = jnp.dot(y.astype(BF16), gw_ref[...], preferred_element_type=F32) + gb_ref[...]
    gs = us_ref[:, :, SSM_WIDTH:].reshape(rows, SSM_WIDTH).astype(F32)
    o_ref[...] = (y * _sigmoid(z) * _silu(gs)).reshape(BATCH, tc, SSM_WIDTH).astype(o_ref.dtype)


def _ssm(us3, lam_re, lam_im, wb, wc_re, wc_im, d_skip, glu_w, glu_b):
    tc = SSM_TC
    rows = tc * BATCH
    src = (jnp.arange(rows, dtype=jnp.int32) % BATCH) * tc + jnp.arange(rows, dtype=jnp.int32) // BATCH
    perm = (src[:, None] == jnp.arange(rows, dtype=jnp.int32)[None, :]).astype(BF16)
    return pl.pallas_call(
        _ssm_kernel,
        grid=(SEQ // tc,),
        in_specs=[pl.BlockSpec((BATCH, tc, 2 * SSM_WIDTH), lambda t: (0, t, 0)),
                  _const_spec((rows, rows)),
                  _const_spec((BATCH, N_STATE)), _const_spec((BATCH, N_STATE)),
                  _const_spec(wb.shape), _const_spec(wc_re.shape), _const_spec(wc_im.shape),
                  _const_spec((1, SSM_WIDTH)), _const_spec((SSM_WIDTH, SSM_WIDTH)),
                  _const_spec((1, SSM_WIDTH))],
        out_specs=pl.BlockSpec((BATCH, tc, SSM_WIDTH), lambda t: (0, t, 0)),
        out_shape=jax.ShapeDtypeStruct((BATCH, SEQ, SSM_WIDTH), BF16),
        scratch_shapes=[pltpu.VMEM((rows, N_STATE), F32), pltpu.VMEM((rows, N_STATE), F32),
                        pltpu.VMEM((BATCH, N_STATE), F32), pltpu.VMEM((BATCH, N_STATE), F32),
                        pltpu.VMEM((SSM_WIDTH // LANES, rows, LANES), F32),
                        pltpu.VMEM((rows, SSM_WIDTH), F32)],
        compiler_params=pltpu.CompilerParams(
            dimension_semantics=("arbitrary",), vmem_limit_bytes=VMEM_LIMIT),
        name="s5_branch",
    )(us3, perm, lam_re, lam_im, wb, wc_re, wc_im, d_skip, glu_w, glu_b)


def _even_out_kernel(x_ref, att_ref, ssm_ref, w_ref, gain_ref, o_ref):
    y = jnp.dot(att_ref[...], w_ref[:ATT_WIDTH, :], preferred_element_type=F32)
    y = y + jnp.dot(ssm_ref[...], w_ref[ATT_WIDTH:, :], preferred_element_type=F32)
    o_ref[...] = x_ref[...] + _rms(y, gain_ref[...])


def _even_out(x2, att2, ssm2, w_out, gain):
    tm = ROW_TILE
    nj = SEQ // tm
    row_spec = pl.BlockSpec((tm, D_MODEL), lambda b, j: (b * nj + j, 0))
    return pl.pallas_call(
        _even_out_kernel,
        grid=(BATCH, nj),
        in_specs=[row_spec, pl.BlockSpec((tm, ATT_WIDTH), lambda b, j: (b * nj + j, 0)),
                  pl.BlockSpec((tm, SSM_WIDTH), lambda b, j: (b * nj + j, 0)),
                  _const_spec(w_out.shape), _const_spec((1, D_MODEL))],
        out_specs=row_spec,
        out_shape=jax.ShapeDtypeStruct((BATCH * SEQ, D_MODEL), F32),
        compiler_params=pltpu.CompilerParams(
            dimension_semantics=("arbitrary", "arbitrary"), vmem_limit_bytes=VMEM_LIMIT),
        name="even_out_proj",
    )(x2, att2, ssm2, w_out, gain)


def _odd_kernel(x_ref, pre_ref, w_in_ref, pw_ref, ps_ref, w_out_ref, post_ref, o_ref,
                ext, tmp0, tmp1):
    tm = ODD_TILE
    hist = POOL_HIST
    total = hist + tm
    j = pl.program_id(1)

    @pl.when(j == 0)
    def _():
        ext[0:hist, :] = jnp.zeros((hist, POOL_WIDTH), F32)

    x = x_ref[...]
    hb = _rms(x, pre_ref[...]).astype(BF16)
    ext[hist:total, :] = jnp.dot(hb, w_in_ref[:, :POOL_WIDTH], preferred_element_type=F32)

    t_idx = j * tm + lax.broadcasted_iota(jnp.int32, (tm, 1), 0)
    count_base = (t_idx + 1).astype(F32)
    tmps = (tmp0, tmp1)
    y = None
    for g, w in enumerate(POOL_WINDOWS):
        cols = slice(g * POOL_GROUP, (g + 1) * POOL_GROUP)
        levels = int(math.log2(w))
        src_ref, src_cols = ext, cols
        for lev in range(levels):
            shift = 2 ** lev
            start = hist - 8 * (levels - 1 - lev)
            val = (src_ref[start:total, src_cols]
                   + src_ref[start - shift:total - shift, src_cols])
            if lev < levels - 1:
                dst = tmps[lev % 2]
                dst[start:total, :] = val
                src_ref, src_cols = dst, slice(None)
        u_g = ext[hist:total, cols]
        mixed = val / jnp.minimum(count_base, float(w)) - u_g
        yg = jnp.dot(mixed.astype(BF16), pw_ref[g], preferred_element_type=F32)
        gate = jnp.dot(hb, w_in_ref[:, POOL_WIDTH + g * POOL_GROUP:
                                    POOL_WIDTH + (g + 1) * POOL_GROUP],
                       preferred_element_type=F32)
        yg = yg * ps_ref[:, cols] * _silu(gate)
        part = jnp.dot(yg.astype(BF16), w_out_ref[cols, :], preferred_element_type=F32)
        y = part if y is None else y + part
    ext[0:hist, :] = ext[tm:total, :]
    o_ref[...] = x + _rms(y, post_ref[...])


def _odd_layer(x2, pre, w_in, pool_w, pool_scale, w_out, post):
    tm = ODD_TILE
    nj = SEQ // tm
    row_spec = pl.BlockSpec((tm, D_MODEL), lambda b, j: (b * nj + j, 0))
    return pl.pallas_call(
        _odd_kernel,
        grid=(BATCH, nj),
        in_specs=[row_spec, _const_spec((1, D_MODEL)), _const_spec(w_in.shape),
                  _const_spec(pool_w.shape), _const_spec((1, POOL_WIDTH)),
                  _const_spec(w_out.shape), _const_spec((1, D_MODEL))],
        out_specs=row_spec,
        out_shape=jax.ShapeDtypeStruct((BATCH * SEQ, D_MODEL), F32),
        scratch_shapes=[pltpu.VMEM((POOL_HIST + tm, POOL_WIDTH), F32),
                        pltpu.VMEM((POOL_HIST + tm, POOL_GROUP), F32),
                        pltpu.VMEM((POOL_HIST + tm, POOL_GROUP), F32)],
        compiler_params=pltpu.CompilerParams(
            dimension_semantics=("arbitrary", "arbitrary"), vmem_limit_bytes=VMEM_LIMIT),
        name="odd_layer",
    )(x2, pre, w_in, pool_w, pool_scale, w_out, post)


def _rope_tables():
    half = ROT_DIM // 2
    inv_freq = ROPE_THETA ** (-jnp.arange(0, ROT_DIM, 2, dtype=F32) / ROT_DIM)
    ang = jnp.arange(SEQ, dtype=jnp.int32).astype(F32)[:, None] * inv_freq[None, :]
    cos, sin = jnp.cos(ang), jnp.sin(ang)
    zeros = jnp.zeros((SEQ, HEAD_DIM - ROT_DIM), F32)
    zeros_h = jnp.zeros((SEQ, half), F32)
    cos_h = jnp.concatenate([cos, cos, jnp.ones_like(zeros)], axis=1)
    sa_h = jnp.concatenate([-sin, zeros_h, zeros], axis=1)
    sb_h = jnp.concatenate([zeros_h, sin, zeros], axis=1)
    rep = LANES // HEAD_DIM
    return (jnp.tile(cos_h, (1, rep)), jnp.tile(sa_h, (1, rep)), jnp.tile(sb_h, (1, rep)))


def _ssm_params(a_re, a_im, log_dt, b_re, b_im, c_re, c_im):
    lam = lax.complex(a_re.astype(F32), a_im.astype(F32))
    dt = jnp.exp(log_dt.astype(F32))[:, None]
    lam_bar = jnp.exp(lam * dt)
    b_bar = ((lam_bar - 1.0) / lam)[..., None] * lax.complex(b_re.astype(F32), b_im.astype(F32))
    lam_re = jnp.broadcast_to(jnp.real(lam_bar).reshape(1, N_STATE), (BATCH, N_STATE))
    lam_im = jnp.broadcast_to(jnp.imag(lam_bar).reshape(1, N_STATE), (BATCH, N_STATE))
    gpc = LANES // SSM_GROUP
    n_chunk = SSM_GROUPS // gpc
    eye = jnp.eye(gpc, dtype=F32)

    def block_diag_in(m):
        m = m.reshape(n_chunk, gpc, SSM_STATE, SSM_GROUP)
        return jnp.einsum('cgph,gk->cghkp', m, eye).reshape(
            n_chunk, gpc * SSM_GROUP, gpc * SSM_STATE)

    def block_diag_out(m):
        m = m.reshape(n_chunk, gpc, SSM_GROUP, SSM_STATE)
        return jnp.einsum('cghp,gk->cgpkh', m, eye).reshape(
            n_chunk, gpc * SSM_STATE, gpc * SSM_GROUP)

    wb = jnp.concatenate([block_diag_in(jnp.real(b_bar)), block_diag_in(jnp.imag(b_bar))],
                         axis=2).astype(BF16)
    wc_re = block_diag_out(c_re.astype(F32)).astype(BF16)
    wc_im = block_diag_out(-c_im.astype(F32)).astype(BF16)
    return lam_re, lam_im, wb, wc_re, wc_im


def kernel(x, pre_norm, post_norm, even_w_in, even_w_out, ssm_a_re, ssm_a_im, ssm_log_dt,
           ssm_b_re, ssm_b_im, ssm_c_re, ssm_c_im, ssm_d, ssm_glu_w, ssm_glu_b,
           odd_w_in, pool_w, pool_scale, odd_w_out):
    cos_t, sa_t, sb_t = _rope_tables()
    x2 = x.reshape(BATCH * SEQ, D_MODEL)
    for layer in range(DEPTH):
        i = layer // 2
        pre = pre_norm[layer].reshape(1, D_MODEL)
        post = post_norm[layer].reshape(1, D_MODEL)
        if layer % 2 == 0:
            q, k, v, g, us = _even_in(x2, pre, even_w_in[i].astype(BF16), cos_t, sa_t, sb_t)
            shape3 = (BATCH, SEQ, ATT_WIDTH)
            att = _attention(q.reshape(shape3), k.reshape(shape3), v.reshape(shape3),
                             g.reshape(shape3))
            lam_re, lam_im, wb, wc_re, wc_im = _ssm_params(
                ssm_a_re[i], ssm_a_im[i], ssm_log_dt[i], ssm_b_re[i], ssm_b_im[i],
                ssm_c_re[i], ssm_c_im[i])
            ssm = _ssm(us.reshape(BATCH, SEQ, 2 * SSM_WIDTH), lam_re, lam_im, wb, wc_re, wc_im,
                       ssm_d[i].reshape(1, SSM_WIDTH), ssm_glu_w[i].astype(BF16),
                       ssm_glu_b[i].reshape(1, SSM_WIDTH))
            x2 = _even_out(x2, att.reshape(BATCH * SEQ, ATT_WIDTH),
                           ssm.reshape(BATCH * SEQ, SSM_WIDTH), even_w_out[i].astype(BF16), post)
        else:
            x2 = _odd_layer(x2, pre, odd_w_in[i].astype(BF16), pool_w[i].astype(BF16),
                            pool_scale[i].reshape(1, POOL_WIDTH), odd_w_out[i].astype(BF16), post)
    return x2.reshape(BATCH, SEQ, D_MODEL)
```

```python
import math

import jax
import jax.numpy as jnp
import numpy as np
from jax import lax
from jax.experimental import pallas as pl
from jax.experimental.pallas import tpu as pltpu

F32 = jnp.float32
BF16 = jnp.bfloat16

D_MODEL = 1024
BATCH = 8
SEQ = 2048
DEPTH = 4
HEAD_DIM = 64
ATT_WIDTH = 1024
ROT_DIM = 16
ROPE_THETA = 500000.0
SSM_WIDTH = 512
SSM_GROUP = 16
SSM_GROUPS = 32
SSM_STATE = 64
N_STATE = SSM_GROUPS * SSM_STATE
POOL_WIDTH = 2048
POOL_WINDOWS = (2, 4, 8, 16)
POOL_GROUP = 512
EVEN_IN = 4 * ATT_WIDTH + 2 * SSM_WIDTH
RMS_EPS = 1e-6
LOG2_E = math.log2(math.e)

LANES = 128
ATT_CHUNK = 128
ATT_GROUP = 4
NEG_BIG = -1e30
VMEM_LIMIT = 56 * 1024 * 1024

ROW_TILE = 512
SSM_TC = 64
ODD_TILE = 512
POOL_HIST = 32


def _sigmoid(x):
    return 1.0 / (1.0 + jnp.exp(-x))


def _silu(x):
    return x * _sigmoid(x)


def _rms(x, gain):
    return x * lax.rsqrt(jnp.mean(x * x, axis=-1, keepdims=True) + RMS_EPS) * gain


def _const_spec(shape):
    nd = len(shape)
    return pl.BlockSpec(shape, lambda *_: (0,) * nd, pipeline_mode=pl.Buffered(1))


def _even_in_kernel(x_ref, gain_ref, w_ref, cos_ref, sa_ref, sb_ref,
                    q_ref, k_ref, v_ref, g_ref, us_ref):
    hb = _rms(x_ref[...], gain_ref[...]).astype(BF16)
    cos = cos_ref[...]
    sa = sa_ref[...]
    sb = sb_ref[...]

    def proj(lo, width):
        return jnp.dot(hb, w_ref[:, lo:lo + width], preferred_element_type=F32)

    def rope_store(dst, y, scale):
        for c in range(ATT_WIDTH // LANES):
            yc = y[:, c * LANES:(c + 1) * LANES]
            r = (yc * cos + pltpu.roll(yc, LANES - ROT_DIM // 2, 1) * sa
                 + pltpu.roll(yc, ROT_DIM // 2, 1) * sb)
            dst[:, c * LANES:(c + 1) * LANES] = (r * scale).astype(dst.dtype)

    rope_store(q_ref, proj(0, ATT_WIDTH), HEAD_DIM ** -0.5 * LOG2_E)
    rope_store(k_ref, proj(ATT_WIDTH, ATT_WIDTH), 1.0)
    v_ref[...] = proj(2 * ATT_WIDTH, ATT_WIDTH).astype(v_ref.dtype)
    g_ref[...] = proj(3 * ATT_WIDTH, ATT_WIDTH).astype(g_ref.dtype)
    us_ref[...] = proj(4 * ATT_WIDTH, 2 * SSM_WIDTH).astype(us_ref.dtype)


def _even_in(x2, gain, w_in, cos_t, sa_t, sb_t):
    tm = ROW_TILE
    nj = SEQ // tm
    row_spec = pl.BlockSpec((tm, D_MODEL), lambda b, j: (b * nj + j, 0))
    tab_spec = pl.BlockSpec((tm, LANES), lambda b, j: (j, 0))
    att_spec = pl.BlockSpec((tm, ATT_WIDTH), lambda b, j: (b * nj + j, 0))
    att_shape = jax.ShapeDtypeStruct((BATCH * SEQ, ATT_WIDTH), BF16)
    return pl.pallas_call(
        _even_in_kernel,
        grid=(BATCH, nj),
        in_specs=[row_spec, _const_spec((1, D_MODEL)), _const_spec((D_MODEL, EVEN_IN)),
                  tab_spec, tab_spec, tab_spec],
        out_specs=[att_spec] * 5,
        out_shape=[att_shape] * 5,
        compiler_params=pltpu.CompilerParams(
            dimension_semantics=("arbitrary", "arbitrary"), vmem_limit_bytes=VMEM_LIMIT),
        name="even_in_proj",
    )(x2, gain, w_in, cos_t, sa_t, sb_t)


N_CLS = 16
CLS_LEN = SEQ // N_CLS


def _attn_masks():
    ch = ATT_CHUNK
    rho = np.arange(ch)
    tau = 16 * (rho % 8) + rho // 8
    kk = np.arange(2 * ch)
    p1_full = np.where(kk[None, :] < ch, kk[None, :] >= tau[:, None],
                       (kk[None, :] - ch) <= tau[:, None])
    p1_first = np.arange(ch)[None, :] <= tau[:, None]
    a = 4 * (rho % 32) + rho // 32
    kap = np.arange(2 * ch)
    kk2 = 4 * (kap % 64) + kap // 64
    p2_full = (kk2[None, :] >= a[:, None]) & (kk2[None, :] <= a[:, None] + ch)
    kap1 = np.arange(ch)
    kk1 = 4 * (kap1 % 32) + kap1 // 32
    p2_first = kk1[None, :] <= a[:, None]
    p3 = np.arange(ch)[None, :] <= rho[:, None]
    valid = np.concatenate([p1_full, p1_first, p2_full, p2_first, p3], axis=1)
    return jnp.asarray(np.where(valid, 0.0, NEG_BIG), dtype=F32)


def _attn_kernel(q_ref, k_ref, v_ref, mask_ref, o_ref,
                 q32, k32, v32, q16f, q16, k16, v16, m_a, m_b, acc_a, acc_b, fin):
    ch = ATT_CHUNK
    lane = lax.broadcasted_iota(jnp.int32, (1, LANES), 1)
    is_a = lane < HEAD_DIM
    is_b = jnp.logical_not(is_a)
    state = ((m_a, acc_a), (m_b, acc_b))
    ones_row = jnp.ones((1, LANES), BF16)
    zeros_row = jnp.zeros((1, LANES), BF16)
    mask_p1_full = mask_ref[:, 0:2 * ch]
    mask_p1_first = mask_ref[:, 2 * ch:3 * ch]
    mask_p2_full = mask_ref[:, 3 * ch:5 * ch]
    mask_p2_first = mask_ref[:, 5 * ch:6 * ch]
    mask_p3 = mask_ref[:, 6 * ch:7 * ch]

    q32[...] = q_ref[0].astype(F32)
    k32[...] = k_ref[0].astype(F32)
    v32[...] = v_ref[0].astype(F32)
    for r in range(N_CLS):
        rows = slice(r * CLS_LEN, (r + 1) * CLS_LEN)
        qv = q32[pl.ds(r, CLS_LEN, stride=N_CLS), :]
        q16f[rows, :] = qv
        q16[rows, :] = qv.astype(BF16)
        k16[rows, :] = k32[pl.ds(r, CLS_LEN, stride=N_CLS), :].astype(BF16)
        v16[rows, :] = v32[pl.ds(r, CLS_LEN, stride=N_CLS), :].astype(BF16)

    def gather(ref, slabs):
        parts = [ref[s:s + n, :] for s, n in slabs]
        return parts[0] if len(parts) == 1 else jnp.concatenate(parts, axis=0)

    def scatter(ref, slabs, val):
        off = 0
        for s, n in slabs:
            ref[s:s + n, :] = val[off:off + n, :]
            off += n

    def process(items, has_state, finalize):
        scores = []
        for qc, kc, vc, mask, slabs in items:
            for is_h in (is_a, is_b):
                qh = jnp.where(is_h, qc, zeros_row)
                s = lax.dot_general(qh, kc, (((1,), (1,)), ((), ())),
                                    preferred_element_type=F32) + mask
                scores.append((s, jnp.max(s, axis=1, keepdims=True)))
        partial = []
        n = 0
        for qc, kc, vc, mask, slabs in items:
            for h, is_h in enumerate((is_a, is_b)):
                s, m_c = scores[n]
                n += 1
                if has_state:
                    m_prev = gather(state[h][0], slabs)
                    m_new = jnp.maximum(m_prev, m_c)
                else:
                    m_prev = None
                    m_new = jnp.broadcast_to(m_c, (ch, LANES))
                m_k = jnp.concatenate([m_new] * (kc.shape[0] // LANES), axis=1)
                p = jnp.exp2(s - m_k)
                vh = jnp.where(is_h, vc, ones_row)
                pv = jnp.dot(p.astype(BF16), vh, preferred_element_type=F32)
                partial.append((m_prev, m_new, pv))
        n = 0
        for qc, kc, vc, mask, slabs in items:
            res = []
            for h in range(2):
                m_prev, m_new, pv = partial[n]
                n += 1
                if has_state:
                    a_new = jnp.exp2(m_prev - m_new) * gather(state[h][1], slabs) + pv
                else:
                    a_new = pv
                res.append(a_new)
                if not finalize:
                    scatter(state[h][0], slabs, m_new)
                    scatter(state[h][1], slabs, a_new)
            if finalize:
                num = jnp.where(is_a, res[0], res[1])
                den = jnp.where(is_a, pltpu.roll(res[0], HEAD_DIM, 1),
                                pltpu.roll(res[1], HEAD_DIM, 1))
                scatter(fin, slabs, num / den)

    def run(items, has_state, finalize):
        for i in range(0, len(items), ATT_GROUP):
            process(items[i:i + ATT_GROUP], has_state, finalize)

    per = ch // N_CLS
    items = []
    for i in range(SEQ // ch):
        slabs = [(r * CLS_LEN + per * i, per) for r in range(N_CLS)]
        base = i * ch
        lo = base - ch if i > 0 else base
        items.append((gather(q16f, slabs).astype(BF16), k_ref[0, lo:base + ch, :],
                      v_ref[0, lo:base + ch, :], mask_p1_full if i > 0 else mask_p1_first, slabs))
    run(items, False, False)

    quarter = ch // 4
    items = []
    for r4 in range(4):
        for ic in range(SEQ // (4 * ch)):
            cls = [(4 * j + r4) * CLS_LEN for j in range(4)]
            slabs = [(c + quarter * ic, quarter) for c in cls]
            if ic > 0:
                kslabs = [(c + quarter * (ic - 1), 2 * quarter) for c in cls]
                mask = mask_p2_full
            else:
                kslabs = slabs
                mask = mask_p2_first
            items.append((gather(q16, slabs), gather(k16, kslabs), gather(v16, kslabs),
                          mask, slabs))
    run(items, True, False)

    items = []
    for r in range(N_CLS):
        slabs = [(r * CLS_LEN, CLS_LEN)]
        items.append((gather(q16, slabs), gather(k16, slabs), gather(v16, slabs), mask_p3, slabs))
    run(items, True, True)

    row = lax.broadcasted_iota(jnp.int32, (ch, ch), 0)
    col = lax.broadcasted_iota(jnp.int32, (ch, ch), 1)
    shuffle = jnp.where(col == per * (row % N_CLS) + row // N_CLS, 1.0, 0.0).astype(BF16)
    for i in range(SEQ // ch):
        slab = gather(fin, [(r * CLS_LEN + per * i, per) for r in range(N_CLS)])
        o_ref[0, i * ch:(i + 1) * ch, :] = jnp.dot(
            shuffle, slab.astype(BF16), preferred_element_type=F32).astype(o_ref.dtype)


def _attention(q, k, v):
    blk = pl.BlockSpec((1, SEQ, LANES), lambda b, h: (b, 0, h))
    f32_scr = pltpu.VMEM((SEQ, LANES), F32)
    bf_scr = pltpu.VMEM((SEQ, LANES), BF16)
    masks = _attn_masks()
    return pl.pallas_call(
        _attn_kernel,
        grid=(BATCH, ATT_WIDTH // LANES),
        in_specs=[blk, blk, blk, _const_spec(masks.shape)],
        out_specs=blk,
        out_shape=jax.ShapeDtypeStruct((BATCH, SEQ, ATT_WIDTH), BF16),
        scratch_shapes=[f32_scr] * 4 + [bf_scr] * 3 + [f32_scr] * 5,
        compiler_params=pltpu.CompilerParams(
            dimension_semantics=("arbitrary", "arbitrary"), vmem_limit_bytes=VMEM_LIMIT),
        name="dilated_attention",
    )(q, k, v, masks)


def _ssm_kernel(us_ref, perm_ref, lam_re_ref, lam_im_ref, wb_ref, wc_re_ref, wc_im_ref,
                d_ref, gw_ref, gb_ref, o_ref, xr, xi, st_re, st_im, y_tb, y_bt):
    tc = SSM_TC
    rows = tc * BATCH
    n_chunk = SSM_WIDTH // LANES
    cpc = N_STATE // n_chunk

    @pl.when(pl.program_id(0) == 0)
    def _():
        st_re[...] = jnp.zeros_like(st_re)
        st_im[...] = jnp.zeros_like(st_im)

    ub = us_ref[:, :, :SSM_WIDTH].reshape(rows, SSM_WIDTH)
    u_tb = jnp.dot(perm_ref[...], ub, preferred_element_type=F32).astype(BF16)
    for c in range(n_chunk):
        cols = slice(c * cpc, (c + 1) * cpc)
        bu = jnp.dot(u_tb[:, c * LANES:(c + 1) * LANES], wb_ref[c], preferred_element_type=F32)
        xr[:, cols] = bu[:, :cpc]
        xi[:, cols] = bu[:, cpc:]
        lr = lam_re_ref[:, cols]
        li = lam_im_ref[:, cols]
        sr = st_re[:, cols]
        si = st_im[:, cols]
        for t in range(tc):
            trow = slice(t * BATCH, (t + 1) * BATCH)
            sr, si = (lr * sr - li * si + xr[trow, cols], lr * si + li * sr + xi[trow, cols])
            xr[trow, cols] = sr
            xi[trow, cols] = si
        st_re[:, cols] = sr
        st_im[:, cols] = si
        y_tb[c] = (
            jnp.dot(xr[:, cols].astype(BF16), wc_re_ref[c], preferred_element_type=F32)
            + jnp.dot(xi[:, cols].astype(BF16), wc_im_ref[c], preferred_element_type=F32))
        for b in range(BATCH):
            y_bt[b * tc:(b + 1) * tc, c * LANES:(c + 1) * LANES] = (
                y_tb[c, pl.ds(b, tc, stride=BATCH), :])
    u = ub.astype(F32)
    y = jax.nn.gelu(y_bt[...] + d_ref[...] * u)
    z = jnp.dot(y.astype(BF16), gw_ref[...], preferred_element_type=F32) + gb_ref[...]
    gs = us_ref[:, :, SSM_WIDTH:].reshape(rows, SSM_WIDTH).astype(F32)
    o_ref[...] = (y * _sigmoid(z) * _silu(gs)).reshape(BATCH, tc, SSM_WIDTH).astype(o_ref.dtype)


def _ssm(us3, lam_re, lam_im, wb, wc_re, wc_im, d_skip, glu_w, glu_b):
    tc = SSM_TC
    rows = tc * BATCH
    dst_row = np.arange(rows)
    perm = jnp.asarray((dst_row[:, None] % BATCH) * tc + dst_row[:, None] // BATCH
                       == dst_row[None, :], dtype=BF16)
    return pl.pallas_call(
        _ssm_kernel,
        grid=(SEQ // tc,),
        in_specs=[pl.BlockSpec((BATCH, tc, 2 * SSM_WIDTH), lambda t: (0, t, 0)),
                  _const_spec((rows, rows)),
                  _const_spec((BATCH, N_STATE)), _const_spec((BATCH, N_STATE)),
                  _const_spec(wb.shape), _const_spec(wc_re.shape), _const_spec(wc_im.shape),
                  _const_spec((1, SSM_WIDTH)), _const_spec((SSM_WIDTH, SSM_WIDTH)),
                  _const_spec((1, SSM_WIDTH))],
        out_specs=pl.BlockSpec((BATCH, tc, SSM_WIDTH), lambda t: (0, t, 0)),
        out_shape=jax.ShapeDtypeStruct((BATCH, SEQ, SSM_WIDTH), BF16),
        scratch_shapes=[pltpu.VMEM((rows, N_STATE), F32), pltpu.VMEM((rows, N_STATE), F32),
                        pltpu.VMEM((BATCH, N_STATE), F32), pltpu.VMEM((BATCH, N_STATE), F32),
                        pltpu.VMEM((SSM_WIDTH // LANES, rows, LANES), F32),
                        pltpu.VMEM((rows, SSM_WIDTH), F32)],
        compiler_params=pltpu.CompilerParams(
            dimension_semantics=("arbitrary",), vmem_limit_bytes=VMEM_LIMIT),
        name="s5_branch",
    )(us3, perm, lam_re, lam_im, wb, wc_re, wc_im, d_skip, glu_w, glu_b)


def _even_out_kernel(x_ref, att_ref, g_ref, ssm_ref, w_ref, gain_ref, o_ref):
    gated = (att_ref[...].astype(F32) * _silu(g_ref[...].astype(F32))).astype(BF16)
    y = jnp.dot(gated, w_ref[:ATT_WIDTH, :], preferred_element_type=F32)
    y = y + jnp.dot(ssm_ref[...], w_ref[ATT_WIDTH:, :], preferred_element_type=F32)
    o_ref[...] = x_ref[...] + _rms(y, gain_ref[...])


def _even_out(x2, att2, g2, ssm2, w_out, gain):
    tm = ROW_TILE
    nj = SEQ // tm
    row_spec = pl.BlockSpec((tm, D_MODEL), lambda b, j: (b * nj + j, 0))
    att_spec = pl.BlockSpec((tm, ATT_WIDTH), lambda b, j: (b * nj + j, 0))
    return pl.pallas_call(
        _even_out_kernel,
        grid=(BATCH, nj),
        in_specs=[row_spec, att_spec, att_spec,
                  pl.BlockSpec((tm, SSM_WIDTH), lambda b, j: (b * nj + j, 0)),
                  _const_spec(w_out.shape), _const_spec((1, D_MODEL))],
        out_specs=row_spec,
        out_shape=jax.ShapeDtypeStruct((BATCH * SEQ, D_MODEL), F32),
        compiler_params=pltpu.CompilerParams(
            dimension_semantics=("arbitrary", "arbitrary"), vmem_limit_bytes=VMEM_LIMIT),
        name="even_out_proj",
    )(x2, att2, g2, ssm2, w_out, gain)


def _odd_kernel(x_ref, pre_ref, w_in_ref, pw_ref, ps_ref, w_out_ref, post_ref, o_ref,
                ext, tmp0, tmp1):
    tm = ODD_TILE
    hist = POOL_HIST
    total = hist + tm
    j = pl.program_id(1)

    @pl.when(j == 0)
    def _():
        ext[0:hist, :] = jnp.zeros((hist, POOL_WIDTH), F32)

    x = x_ref[...]
    hb = _rms(x, pre_ref[...]).astype(BF16)
    ext[hist:total, :] = jnp.dot(hb, w_in_ref[:, :POOL_WIDTH], preferred_element_type=F32)

    t_idx = j * tm + lax.broadcasted_iota(jnp.int32, (tm, 1), 0)
    count_base = (t_idx + 1).astype(F32)
    tmps = (tmp0, tmp1)
    y = None
    for g, w in enumerate(POOL_WINDOWS):
        cols = slice(g * POOL_GROUP, (g + 1) * POOL_GROUP)
        levels = int(math.log2(w))
        src_ref, src_cols = ext, cols
        for lev in range(levels):
            shift = 2 ** lev
            start = hist - 8 * (levels - 1 - lev)
            val = (src_ref[start:total, src_cols]
                   + src_ref[start - shift:total - shift, src_cols])
            if lev < levels - 1:
                dst = tmps[lev % 2]
                dst[start:total, :] = val
                src_ref, src_cols = dst, slice(None)
        u_g = ext[hist:total, cols]
        mixed = val / jnp.minimum(count_base, float(w)) - u_g
        yg = jnp.dot(mixed.astype(BF16), pw_ref[g], preferred_element_type=F32)
        gate = jnp.dot(hb, w_in_ref[:, POOL_WIDTH + g * POOL_GROUP:
                                    POOL_WIDTH + (g + 1) * POOL_GROUP],
                       preferred_element_type=F32)
        yg = yg * ps_ref[:, cols] * _silu(gate)
        part = jnp.dot(yg.astype(BF16), w_out_ref[cols, :], preferred_element_type=F32)
        y = part if y is None else y + part
    ext[0:hist, :] = ext[tm:total, :]
    o_ref[...] = x + _rms(y, post_ref[...])


def _odd_layer(x2, pre, w_in, pool_w, pool_scale, w_out, post):
    tm = ODD_TILE
    nj = SEQ // tm
    row_spec = pl.BlockSpec((tm, D_MODEL), lambda b, j: (b * nj + j, 0))
    return pl.pallas_call(
        _odd_kernel,
        grid=(BATCH, nj),
        in_specs=[row_spec, _const_spec((1, D_MODEL)), _const_spec(w_in.shape),
                  _const_spec(pool_w.shape), _const_spec((1, POOL_WIDTH)),
                  _const_spec(w_out.shape), _const_spec((1, D_MODEL))],
        out_specs=row_spec,
        out_shape=jax.ShapeDtypeStruct((BATCH * SEQ, D_MODEL), F32),
        scratch_shapes=[pltpu.VMEM((POOL_HIST + tm, POOL_WIDTH), F32),
                        pltpu.VMEM((POOL_HIST + tm, POOL_GROUP), F32),
                        pltpu.VMEM((POOL_HIST + tm, POOL_GROUP), F32)],
        compiler_params=pltpu.CompilerParams(
            dimension_semantics=("arbitrary", "arbitrary"), vmem_limit_bytes=VMEM_LIMIT),
        name="odd_layer",
    )(x2, pre, w_in, pool_w, pool_scale, w_out, post)


def _rope_tables():
    half = ROT_DIM // 2
    inv_freq = ROPE_THETA ** (-jnp.arange(0, ROT_DIM, 2, dtype=F32) / ROT_DIM)
    ang = jnp.arange(SEQ, dtype=jnp.int32).astype(F32)[:, None] * inv_freq[None, :]
    cos, sin = jnp.cos(ang), jnp.sin(ang)
    zeros = jnp.zeros((SEQ, HEAD_DIM - ROT_DIM), F32)
    zeros_h = jnp.zeros((SEQ, half), F32)
    cos_h = jnp.concatenate([cos, cos, jnp.ones_like(zeros)], axis=1)
    sa_h = jnp.concatenate([-sin, zeros_h, zeros], axis=1)
    sb_h = jnp.concatenate([zeros_h, sin, zeros], axis=1)
    rep = LANES // HEAD_DIM
    return (jnp.tile(cos_h, (1, rep)), jnp.tile(sa_h, (1, rep)), jnp.tile(sb_h, (1, rep)))


def _ssm_params(a_re, a_im, log_dt, b_re, b_im, c_re, c_im):
    lam = lax.complex(a_re.astype(F32), a_im.astype(F32))
    dt = jnp.exp(log_dt.astype(F32))[:, None]
    lam_bar = jnp.exp(lam * dt)
    b_bar = ((lam_bar - 1.0) / lam)[..., None] * lax.complex(b_re.astype(F32), b_im.astype(F32))
    lam_re = jnp.broadcast_to(jnp.real(lam_bar).reshape(1, N_STATE), (BATCH, N_STATE))
    lam_im = jnp.broadcast_to(jnp.imag(lam_bar).reshape(1, N_STATE), (BATCH, N_STATE))
    gpc = LANES // SSM_GROUP
    n_chunk = SSM_GROUPS // gpc
    eye = jnp.eye(gpc, dtype=F32)

    def block_diag_in(m):
        m = m.reshape(n_chunk, gpc, SSM_STATE, SSM_GROUP)
        return jnp.einsum('cgph,gk->cghkp', m, eye).reshape(
            n_chunk, gpc * SSM_GROUP, gpc * SSM_STATE)

    def block_diag_out(m):
        m = m.reshape(n_chunk, gpc, SSM_GROUP, SSM_STATE)
        return jnp.einsum('cghp,gk->cgpkh', m, eye).reshape(
            n_chunk, gpc * SSM_STATE, gpc * SSM_GROUP)

    wb = jnp.concatenate([block_diag_in(jnp.real(b_bar)), block_diag_in(jnp.imag(b_bar))],
                         axis=2).astype(BF16)
    wc_re = block_diag_out(c_re.astype(F32)).astype(BF16)
    wc_im = block_diag_out(-c_im.astype(F32)).astype(BF16)
    return lam_re, lam_im, wb, wc_re, wc_im


def kernel(x, pre_norm, post_norm, even_w_in, even_w_out, ssm_a_re, ssm_a_im, ssm_log_dt,
           ssm_b_re, ssm_b_im, ssm_c_re, ssm_c_im, ssm_d, ssm_glu_w, ssm_glu_b,
           odd_w_in, pool_w, pool_scale, odd_w_out):
    cos_t, sa_t, sb_t = _rope_tables()
    x2 = x.reshape(BATCH * SEQ, D_MODEL)
    for layer in range(DEPTH):
        i = layer // 2
        pre = pre_norm[layer].reshape(1, D_MODEL)
        post = post_norm[layer].reshape(1, D_MODEL)
        if layer % 2 == 0:
            q, k, v, g, us = _even_in(x2, pre, even_w_in[i].astype(BF16), cos_t, sa_t, sb_t)
            shape3 = (BATCH, SEQ, ATT_WIDTH)
            att = _attention(q.reshape(shape3), k.reshape(shape3), v.reshape(shape3))
            lam_re, lam_im, wb, wc_re, wc_im = _ssm_params(
                ssm_a_re[i], ssm_a_im[i], ssm_log_dt[i], ssm_b_re[i], ssm_b_im[i],
                ssm_c_re[i], ssm_c_im[i])
            ssm = _ssm(us.reshape(BATCH, SEQ, 2 * SSM_WIDTH), lam_re, lam_im, wb, wc_re, wc_im,
                       ssm_d[i].reshape(1, SSM_WIDTH), ssm_glu_w[i].astype(BF16),
                       ssm_glu_b[i].reshape(1, SSM_WIDTH))
            x2 = _even_out(x2, att.reshape(BATCH * SEQ, ATT_WIDTH), g,
                           ssm.reshape(BATCH * SEQ, SSM_WIDTH), even_w_out[i].astype(BF16), post)
        else:
            x2 = _odd_layer(x2, pre, odd_w_in[i].astype(BF16), pool_w[i].astype(BF16),
                            pool_scale[i].reshape(1, POOL_WIDTH), odd_w_out[i].astype(BF16), post)
    return x2.reshape(BATCH, SEQ, D_MODEL)
```

```python
import math

import jax
import jax.numpy as jnp
import numpy as np
from jax import lax
from jax.experimental import pallas as pl
from jax.experimental.pallas import tpu as pltpu

F32 = jnp.float32
BF16 = jnp.bfloat16

D_MODEL = 1024
BATCH = 8
SEQ = 2048
DEPTH = 4
HEAD_DIM = 64
ATT_WIDTH = 1024
ROT_DIM = 16
ROPE_THETA = 500000.0
SSM_WIDTH = 512
SSM_GROUP = 16
SSM_GROUPS = 32
SSM_STATE = 64
N_STATE = SSM_GROUPS * SSM_STATE
POOL_WIDTH = 2048
POOL_WINDOWS = (2, 4, 8, 16)
POOL_GROUP = 512
EVEN_IN = 4 * ATT_WIDTH + 2 * SSM_WIDTH
RMS_EPS = 1e-6
LOG2_E = math.log2(math.e)

LANES = 128
ATT_CHUNK = 128
ATT_GROUP = 8
NEG_BIG = -1e30
VMEM_LIMIT = 56 * 1024 * 1024

ROW_TILE = 512
OUT_TILE = 1024
SUB_TILE = 512
SSM_TC = 64
ODD_TILE = 512
POOL_HIST = 32


def _sigmoid(x):
    return 0.5 + 0.5 * jnp.tanh(0.5 * x)


def _silu(x):
    h = 0.5 * x
    return h + h * jnp.tanh(h)


def _rms(x, gain):
    return x * lax.rsqrt(jnp.mean(x * x, axis=-1, keepdims=True) + RMS_EPS) * gain


def _const_spec(shape):
    nd = len(shape)
    return pl.BlockSpec(shape, lambda *_: (0,) * nd, pipeline_mode=pl.Buffered(1))


def _even_in_kernel(x_ref, gain_ref, w_ref, cos_ref, sa_ref, sb_ref,
                    q_ref, k_ref, v_ref, g_ref, us_ref):
    hb = _rms(x_ref[...], gain_ref[...]).astype(BF16)
    cos = cos_ref[...]
    sa = sa_ref[...]
    sb = sb_ref[...]

    def proj(lo, width):
        return jnp.dot(hb, w_ref[:, lo:lo + width], preferred_element_type=F32)

    def rope_store(dst, y, scale):
        for c in range(ATT_WIDTH // LANES):
            yc = y[:, c * LANES:(c + 1) * LANES]
            r = (yc * cos + pltpu.roll(yc, LANES - ROT_DIM // 2, 1) * sa
                 + pltpu.roll(yc, ROT_DIM // 2, 1) * sb)
            dst[:, c * LANES:(c + 1) * LANES] = (r * scale).astype(dst.dtype)

    rope_store(q_ref, proj(0, ATT_WIDTH), HEAD_DIM ** -0.5 * LOG2_E)
    rope_store(k_ref, proj(ATT_WIDTH, ATT_WIDTH), 1.0)
    v_ref[...] = proj(2 * ATT_WIDTH, ATT_WIDTH).astype(v_ref.dtype)
    g_ref[...] = proj(3 * ATT_WIDTH, ATT_WIDTH).astype(g_ref.dtype)
    us_ref[...] = proj(4 * ATT_WIDTH, 2 * SSM_WIDTH).astype(us_ref.dtype)


def _even_in(x2, gain, w_in, cos_t, sa_t, sb_t):
    tm = ROW_TILE
    nj = SEQ // tm
    row_spec = pl.BlockSpec((tm, D_MODEL), lambda b, j: (b * nj + j, 0))
    tab_spec = pl.BlockSpec((tm, LANES), lambda b, j: (j, 0))
    att_spec = pl.BlockSpec((tm, ATT_WIDTH), lambda b, j: (b * nj + j, 0))
    att_shape = jax.ShapeDtypeStruct((BATCH * SEQ, ATT_WIDTH), BF16)
    return pl.pallas_call(
        _even_in_kernel,
        grid=(BATCH, nj),
        in_specs=[row_spec, _const_spec((1, D_MODEL)), _const_spec((D_MODEL, EVEN_IN)),
                  tab_spec, tab_spec, tab_spec],
        out_specs=[att_spec] * 5,
        out_shape=[att_shape] * 5,
        compiler_params=pltpu.CompilerParams(
            dimension_semantics=("arbitrary", "arbitrary"), vmem_limit_bytes=VMEM_LIMIT),
        name="even_in_proj",
    )(x2, gain, w_in, cos_t, sa_t, sb_t)


N_CLS = 16
CLS_LEN = SEQ // N_CLS


def _attn_masks():
    ch = ATT_CHUNK
    rho = np.arange(ch)
    tau = 16 * (rho % 8) + rho // 8
    kk = np.arange(2 * ch)
    p1_full = np.where(kk[None, :] < ch, kk[None, :] >= tau[:, None],
                       (kk[None, :] - ch) <= tau[:, None])
    p1_first = np.arange(ch)[None, :] <= tau[:, None]
    a = 4 * (rho % 32) + rho // 32
    kap = np.arange(2 * ch)
    kk2 = 4 * (kap % 64) + kap // 64
    p2_full = (kk2[None, :] >= a[:, None]) & (kk2[None, :] <= a[:, None] + ch)
    kap1 = np.arange(ch)
    kk1 = 4 * (kap1 % 32) + kap1 // 32
    p2_first = kk1[None, :] <= a[:, None]
    p3 = np.arange(ch)[None, :] <= rho[:, None]
    valid = np.concatenate([p1_full, p1_first, p2_full, p2_first, p3], axis=1)
    return jnp.asarray(np.where(valid, 0.0, NEG_BIG), dtype=F32)


def _attn_kernel(q_ref, k_ref, v_ref, mask_ref, o_ref,
                 q32, k32, v32, q16f, q16, k16, v16, m_a, m_b, acc_a, acc_b, fin):
    ch = ATT_CHUNK
    lane = lax.broadcasted_iota(jnp.int32, (1, LANES), 1)
    is_a = lane < HEAD_DIM
    is_b = jnp.logical_not(is_a)
    state = ((m_a, acc_a), (m_b, acc_b))
    ones_row = jnp.ones((1, LANES), BF16)
    zeros_row = jnp.zeros((1, LANES), BF16)
    mask_p1_full = mask_ref[:, 0:2 * ch]
    mask_p1_first = mask_ref[:, 2 * ch:3 * ch]
    mask_p2_full = mask_ref[:, 3 * ch:5 * ch]
    mask_p2_first = mask_ref[:, 5 * ch:6 * ch]
    mask_p3 = mask_ref[:, 6 * ch:7 * ch]

    q32[...] = q_ref[0].astype(F32)
    k32[...] = k_ref[0].astype(F32)
    v32[...] = v_ref[0].astype(F32)
    for r in range(N_CLS):
        rows = slice(r * CLS_LEN, (r + 1) * CLS_LEN)
        qv = q32[pl.ds(r, CLS_LEN, stride=N_CLS), :]
        q16f[rows, :] = qv
        q16[rows, :] = qv.astype(BF16)
        k16[rows, :] = k32[pl.ds(r, CLS_LEN, stride=N_CLS), :].astype(BF16)
        v16[rows, :] = v32[pl.ds(r, CLS_LEN, stride=N_CLS), :].astype(BF16)

    def gather(ref, slabs):
        parts = [ref[s:s + n, :] for s, n in slabs]
        return parts[0] if len(parts) == 1 else jnp.concatenate(parts, axis=0)

    def scatter(ref, slabs, val):
        off = 0
        for s, n in slabs:
            ref[s:s + n, :] = val[off:off + n, :]
            off += n

    def process(items, has_state, finalize):
        scores = []
        for qc, kc, vc, mask, slabs in items:
            for is_h in (is_a, is_b):
                qh = jnp.where(is_h, qc, zeros_row)
                s = lax.dot_general(qh, kc, (((1,), (1,)), ((), ())),
                                    preferred_element_type=F32) + mask
                scores.append((s, jnp.max(s, axis=1, keepdims=True)))
        partial = []
        n = 0
        for qc, kc, vc, mask, slabs in items:
            for h, is_h in enumerate((is_a, is_b)):
                s, m_c = scores[n]
                n += 1
                if has_state:
                    m_prev = gather(state[h][0], slabs)
                    m_new = jnp.maximum(m_prev, m_c)
                else:
                    m_prev = None
                    m_new = jnp.broadcast_to(m_c, (ch, LANES))
                m_k = jnp.concatenate([m_new] * (kc.shape[0] // LANES), axis=1)
                p = jnp.exp2(s - m_k)
                vh = jnp.where(is_h, vc, ones_row)
                pv = jnp.dot(p.astype(BF16), vh, preferred_element_type=F32)
                partial.append((m_prev, m_new, pv))
        n = 0
        for qc, kc, vc, mask, slabs in items:
            res = []
            for h in range(2):
                m_prev, m_new, pv = partial[n]
                n += 1
                if has_state:
                    a_new = jnp.exp2(m_prev - m_new) * gather(state[h][1], slabs) + pv
                else:
                    a_new = pv
                res.append(a_new)
                if not finalize:
                    scatter(state[h][0], slabs, m_new)
                    scatter(state[h][1], slabs, a_new)
            if finalize:
                num = jnp.where(is_a, res[0], res[1])
                den = jnp.where(is_a, pltpu.roll(res[0], HEAD_DIM, 1),
                                pltpu.roll(res[1], HEAD_DIM, 1))
                scatter(fin, slabs, num / den)

    def run(items, has_state, finalize):
        for i in range(0, len(items), ATT_GROUP):
            process(items[i:i + ATT_GROUP], has_state, finalize)

    per = ch // N_CLS
    items = []
    for i in range(SEQ // ch):
        slabs = [(r * CLS_LEN + per * i, per) for r in range(N_CLS)]
        base = i * ch
        lo = base - ch if i > 0 else base
        items.append((gather(q16f, slabs).astype(BF16), k_ref[0, lo:base + ch, :],
                      v_ref[0, lo:base + ch, :], mask_p1_full if i > 0 else mask_p1_first, slabs))
    run(items, False, False)

    quarter = ch // 4
    items = []
    for r4 in range(4):
        for ic in range(SEQ // (4 * ch)):
            cls = [(4 * j + r4) * CLS_LEN for j in range(4)]
            slabs = [(c + quarter * ic, quarter) for c in cls]
            if ic > 0:
                kslabs = [(c + quarter * (ic - 1), 2 * quarter) for c in cls]
                mask = mask_p2_full
            else:
                kslabs = slabs
                mask = mask_p2_first
            items.append((gather(q16, slabs), gather(k16, kslabs), gather(v16, kslabs),
                          mask, slabs))
    run(items, True, False)

    items = []
    for r in range(N_CLS):
        slabs = [(r * CLS_LEN, CLS_LEN)]
        items.append((gather(q16, slabs), gather(k16, slabs), gather(v16, slabs), mask_p3, slabs))
    run(items, True, True)

    row = lax.broadcasted_iota(jnp.int32, (ch, ch), 0)
    col = lax.broadcasted_iota(jnp.int32, (ch, ch), 1)
    shuffle = jnp.where(col == per * (row % N_CLS) + row // N_CLS, 1.0, 0.0).astype(BF16)
    for i in range(SEQ // ch):
        slab = gather(fin, [(r * CLS_LEN + per * i, per) for r in range(N_CLS)])
        o_ref[0, i * ch:(i + 1) * ch, :] = jnp.dot(
            shuffle, slab.astype(BF16), preferred_element_type=F32).astype(o_ref.dtype)


def _attention(q, k, v):
    blk = pl.BlockSpec((1, SEQ, LANES), lambda b, h: (b, 0, h))
    f32_scr = pltpu.VMEM((SEQ, LANES), F32)
    bf_scr = pltpu.VMEM((SEQ, LANES), BF16)
    masks = _attn_masks()
    return pl.pallas_call(
        _attn_kernel,
        grid=(BATCH, ATT_WIDTH // LANES),
        in_specs=[blk, blk, blk, _const_spec(masks.shape)],
        out_specs=blk,
        out_shape=jax.ShapeDtypeStruct((BATCH, SEQ, ATT_WIDTH), BF16),
        scratch_shapes=[f32_scr] * 4 + [bf_scr] * 3 + [f32_scr] * 5,
        compiler_params=pltpu.CompilerParams(
            dimension_semantics=("arbitrary", "arbitrary"), vmem_limit_bytes=VMEM_LIMIT),
        name="dilated_attention",
    )(q, k, v, masks)


def _ssm_kernel(us_ref, perm_ref, lam_re_ref, lam_im_ref, wb_ref, wc_re_ref, wc_im_ref,
                d_ref, gw_ref, gb_ref, o_ref, xr, xi, st_re, st_im, y_tb, y_bt):
    tc = SSM_TC
    rows = tc * BATCH
    n_chunk = SSM_WIDTH // LANES
    cpc = N_STATE // n_chunk

    @pl.when(pl.program_id(0) == 0)
    def _():
        st_re[...] = jnp.zeros_like(st_re)
        st_im[...] = jnp.zeros_like(st_im)

    ub = us_ref[:, :, :SSM_WIDTH].reshape(rows, SSM_WIDTH)
    u_tb = jnp.dot(perm_ref[...], ub, preferred_element_type=F32).astype(BF16)
    for c in range(n_chunk):
        cols = slice(c * cpc, (c + 1) * cpc)
        bu = jnp.dot(u_tb[:, c * LANES:(c + 1) * LANES], wb_ref[c], preferred_element_type=F32)
        xr[:, cols] = bu[:, :cpc]
        xi[:, cols] = bu[:, cpc:]
        lr = lam_re_ref[:, cols]
        li = lam_im_ref[:, cols]
        sr = st_re[:, cols]
        si = st_im[:, cols]
        for t in range(tc):
            trow = slice(t * BATCH, (t + 1) * BATCH)
            sr, si = (lr * sr - li * si + xr[trow, cols], lr * si + li * sr + xi[trow, cols])
            xr[trow, cols] = sr
            xi[trow, cols] = si
        st_re[:, cols] = sr
        st_im[:, cols] = si
        y_tb[c] = (
            jnp.dot(xr[:, cols].astype(BF16), wc_re_ref[c], preferred_element_type=F32)
            + jnp.dot(xi[:, cols].astype(BF16), wc_im_ref[c], preferred_element_type=F32))
        lanes = slice(c * LANES, (c + 1) * LANES)
        y_c = jnp.concatenate(
            [y_tb[c, pl.ds(b, tc, stride=BATCH), :] for b in range(BATCH)], axis=0)
        y_c = jax.nn.gelu(y_c + d_ref[:, lanes] * ub[:, lanes].astype(F32))
        y_bt[:, lanes] = y_c
    y = y_bt[...]
    z = jnp.dot(y.astype(BF16), gw_ref[...], preferred_element_type=F32) + gb_ref[...]
    gs = us_ref[:, :, SSM_WIDTH:].reshape(rows, SSM_WIDTH).astype(F32)
    out = y * _sigmoid(z) * _silu(gs)
    o_ref[...] = out.reshape(BATCH, tc, SSM_WIDTH).astype(o_ref.dtype)


def _ssm(us3, lam_re, lam_im, wb, wc_re, wc_im, d_skip, glu_w, glu_b):
    tc = SSM_TC
    rows = tc * BATCH
    dst_row = np.arange(rows)
    perm = jnp.asarray((dst_row[:, None] % BATCH) * tc + dst_row[:, None] // BATCH
                       == dst_row[None, :], dtype=BF16)
    return pl.pallas_call(
        _ssm_kernel,
        grid=(SEQ // tc,),
        in_specs=[pl.BlockSpec((BATCH, tc, 2 * SSM_WIDTH), lambda t: (0, t, 0)),
                  _const_spec((rows, rows)),
                  _const_spec((BATCH, N_STATE)), _const_spec((BATCH, N_STATE)),
                  _const_spec(wb.shape), _const_spec(wc_re.shape), _const_spec(wc_im.shape),
                  _const_spec((1, SSM_WIDTH)), _const_spec((SSM_WIDTH, SSM_WIDTH)),
                  _const_spec((1, SSM_WIDTH))],
        out_specs=pl.BlockSpec((BATCH, tc, SSM_WIDTH), lambda t: (0, t, 0)),
        out_shape=jax.ShapeDtypeStruct((BATCH, SEQ, SSM_WIDTH), BF16),
        scratch_shapes=[pltpu.VMEM((rows, N_STATE), F32), pltpu.VMEM((rows, N_STATE), F32),
                        pltpu.VMEM((BATCH, N_STATE), F32), pltpu.VMEM((BATCH, N_STATE), F32),
                        pltpu.VMEM((SSM_WIDTH // LANES, rows, LANES), F32),
                        pltpu.VMEM((rows, SSM_WIDTH), F32)],
        compiler_params=pltpu.CompilerParams(
            dimension_semantics=("arbitrary",), vmem_limit_bytes=VMEM_LIMIT),
        name="s5_branch",
    )(us3, perm, lam_re, lam_im, wb, wc_re, wc_im, d_skip, glu_w, glu_b)


def _even_out_kernel(x_ref, att_ref, g_ref, ssm_ref, w_ref, gain_ref, o_ref):
    for s in range(OUT_TILE // SUB_TILE):
        rows = slice(s * SUB_TILE, (s + 1) * SUB_TILE)
        gated = (att_ref[rows, :].astype(F32) * _silu(g_ref[rows, :].astype(F32))).astype(BF16)
        y = jnp.dot(gated, w_ref[:ATT_WIDTH, :], preferred_element_type=F32)
        y = y + jnp.dot(ssm_ref[rows, :], w_ref[ATT_WIDTH:, :], preferred_element_type=F32)
        o_ref[rows, :] = x_ref[rows, :] + _rms(y, gain_ref[...])


def _even_out(x2, att2, g2, ssm2, w_out, gain):
    tm = OUT_TILE
    nj = SEQ // tm
    row_spec = pl.BlockSpec((tm, D_MODEL), lambda b, j: (b * nj + j, 0))
    att_spec = pl.BlockSpec((tm, ATT_WIDTH), lambda b, j: (b * nj + j, 0))
    return pl.pallas_call(
        _even_out_kernel,
        grid=(BATCH, nj),
        in_specs=[row_spec, att_spec, att_spec,
                  pl.BlockSpec((tm, SSM_WIDTH), lambda b, j: (b * nj + j, 0)),
                  _const_spec(w_out.shape), _const_spec((1, D_MODEL))],
        out_specs=row_spec,
        out_shape=jax.ShapeDtypeStruct((BATCH * SEQ, D_MODEL), F32),
        compiler_params=pltpu.CompilerParams(
            dimension_semantics=("arbitrary", "arbitrary"), vmem_limit_bytes=VMEM_LIMIT),
        name="even_out_proj",
    )(x2, att2, g2, ssm2, w_out, gain)


def _odd_kernel(x_ref, pre_ref, w_in_ref, pw_ref, ps_ref, w_out_ref, post_ref, o_ref,
                ext, tmp0, tmp1):
    tm = ODD_TILE
    hist = POOL_HIST
    total = hist + tm
    j = pl.program_id(1)

    @pl.when(j == 0)
    def _():
        ext[0:hist, :] = jnp.zeros((hist, POOL_WIDTH), F32)

    x = x_ref[...]
    hb = _rms(x, pre_ref[...]).astype(BF16)
    ext[hist:total, :] = jnp.dot(hb, w_in_ref[:, :POOL_WIDTH], preferred_element_type=F32)

    t_idx = j * tm + lax.broadcasted_iota(jnp.int32, (tm, 1), 0)
    inv_count = 1.0 / (t_idx + 1).astype(F32)
    tmps = (tmp0, tmp1)
    y = None
    for g, w in enumerate(POOL_WINDOWS):
        cols = slice(g * POOL_GROUP, (g + 1) * POOL_GROUP)
        levels = int(math.log2(w))
        src_ref, src_cols = ext, cols
        for lev in range(levels):
            shift = 2 ** lev
            start = hist - 8 * (levels - 1 - lev)
            val = (src_ref[start:total, src_cols]
                   + src_ref[start - shift:total - shift, src_cols])
            if lev < levels - 1:
                dst = tmps[lev % 2]
                dst[start:total, :] = val
                src_ref, src_cols = dst, slice(None)
        u_g = ext[hist:total, cols]
        mixed = val * jnp.maximum(inv_count, 1.0 / w) - u_g
        yg = jnp.dot(mixed.astype(BF16), pw_ref[g], preferred_element_type=F32)
        gate = jnp.dot(hb, w_in_ref[:, POOL_WIDTH + g * POOL_GROUP:
                                    POOL_WIDTH + (g + 1) * POOL_GROUP],
                       preferred_element_type=F32)
        yg = yg * ps_ref[:, cols] * _silu(gate)
        part = jnp.dot(yg.astype(BF16), w_out_ref[cols, :], preferred_element_type=F32)
        y = part if y is None else y + part
    ext[0:hist, :] = ext[tm:total, :]
    o_ref[...] = x + _rms(y, post_ref[...])


def _odd_layer(x2, pre, w_in, pool_w, pool_scale, w_out, post):
    tm = ODD_TILE
    nj = SEQ // tm
    row_spec = pl.BlockSpec((tm, D_MODEL), lambda b, j: (b * nj + j, 0))
    return pl.pallas_call(
        _odd_kernel,
        grid=(BATCH, nj),
        in_specs=[row_spec, _const_spec((1, D_MODEL)), _const_spec(w_in.shape),
                  _const_spec(pool_w.shape), _const_spec((1, POOL_WIDTH)),
                  _const_spec(w_out.shape), _const_spec((1, D_MODEL))],
        out_specs=row_spec,
        out_shape=jax.ShapeDtypeStruct((BATCH * SEQ, D_MODEL), F32),
        scratch_shapes=[pltpu.VMEM((POOL_HIST + tm, POOL_WIDTH), F32),
                        pltpu.VMEM((POOL_HIST + tm, POOL_GROUP), F32),
                        pltpu.VMEM((POOL_HIST + tm, POOL_GROUP), F32)],
        compiler_params=pltpu.CompilerParams(
            dimension_semantics=("arbitrary", "arbitrary"), vmem_limit_bytes=VMEM_LIMIT),
        name="odd_layer",
    )(x2, pre, w_in, pool_w, pool_scale, w_out, post)


def _rope_tables():
    half = ROT_DIM // 2
    inv_freq = ROPE_THETA ** (-jnp.arange(0, ROT_DIM, 2, dtype=F32) / ROT_DIM)
    ang = jnp.arange(SEQ, dtype=jnp.int32).astype(F32)[:, None] * inv_freq[None, :]
    cos, sin = jnp.cos(ang), jnp.sin(ang)
    zeros = jnp.zeros((SEQ, HEAD_DIM - ROT_DIM), F32)
    zeros_h = jnp.zeros((SEQ, half), F32)
    cos_h = jnp.concatenate([cos, cos, jnp.ones_like(zeros)], axis=1)
    sa_h = jnp.concatenate([-sin, zeros_h, zeros], axis=1)
    sb_h = jnp.concatenate([zeros_h, sin, zeros], axis=1)
    rep = LANES // HEAD_DIM
    return (jnp.tile(cos_h, (1, rep)), jnp.tile(sa_h, (1, rep)), jnp.tile(sb_h, (1, rep)))


def _ssm_params(a_re, a_im, log_dt, b_re, b_im, c_re, c_im):
    lam = lax.complex(a_re.astype(F32), a_im.astype(F32))
    dt = jnp.exp(log_dt.astype(F32))[:, None]
    lam_bar = jnp.exp(lam * dt)
    b_bar = ((lam_bar - 1.0) / lam)[..., None] * lax.complex(b_re.astype(F32), b_im.astype(F32))
    lam_re = jnp.broadcast_to(jnp.real(lam_bar).reshape(1, N_STATE), (BATCH, N_STATE))
    lam_im = jnp.broadcast_to(jnp.imag(lam_bar).reshape(1, N_STATE), (BATCH, N_STATE))
    gpc = LANES // SSM_GROUP
    n_chunk = SSM_GROUPS // gpc
    eye = jnp.eye(gpc, dtype=F32)

    def block_diag_in(m):
        m = m.reshape(n_chunk, gpc, SSM_STATE, SSM_GROUP)
        return jnp.einsum('cgph,gk->cghkp', m, eye).reshape(
            n_chunk, gpc * SSM_GROUP, gpc * SSM_STATE)

    def block_diag_out(m):
        m = m.reshape(n_chunk, gpc, SSM_GROUP, SSM_STATE)
        return jnp.einsum('cghp,gk->cgpkh', m, eye).reshape(
            n_chunk, gpc * SSM_STATE, gpc * SSM_GROUP)

    wb = jnp.concatenate([block_diag_in(jnp.real(b_bar)), block_diag_in(jnp.imag(b_bar))],
                         axis=2).astype(BF16)
    wc_re = block_diag_out(c_re.astype(F32)).astype(BF16)
    wc_im = block_diag_out(-c_im.astype(F32)).astype(BF16)
    return lam_re, lam_im, wb, wc_re, wc_im


def kernel(x, pre_norm, post_norm, even_w_in, even_w_out, ssm_a_re, ssm_a_im, ssm_log_dt,
           ssm_b_re, ssm_b_im, ssm_c_re, ssm_c_im, ssm_d, ssm_glu_w, ssm_glu_b,
           odd_w_in, pool_w, pool_scale, odd_w_out):
    cos_t, sa_t, sb_t = _rope_tables()
    x2 = x.reshape(BATCH * SEQ, D_MODEL)
    for layer in range(DEPTH):
        i = layer // 2
        pre = pre_norm[layer].reshape(1, D_MODEL)
        post = post_norm[layer].reshape(1, D_MODEL)
        if layer % 2 == 0:
            q, k, v, g, us = _even_in(x2, pre, even_w_in[i].astype(BF16), cos_t, sa_t, sb_t)
            shape3 = (BATCH, SEQ, ATT_WIDTH)
            att = _attention(q.reshape(shape3), k.reshape(shape3), v.reshape(shape3))
            lam_re, lam_im, wb, wc_re, wc_im = _ssm_params(
                ssm_a_re[i], ssm_a_im[i], ssm_log_dt[i], ssm_b_re[i], ssm_b_im[i],
                ssm_c_re[i], ssm_c_im[i])
            ssm = _ssm(us.reshape(BATCH, SEQ, 2 * SSM_WIDTH), lam_re, lam_im, wb, wc_re, wc_im,
                       ssm_d[i].reshape(1, SSM_WIDTH), ssm_glu_w[i].astype(BF16),
                       ssm_glu_b[i].reshape(1, SSM_WIDTH))
            x2 = _even_out(x2, att.reshape(BATCH * SEQ, ATT_WIDTH), g,
                           ssm.reshape(BATCH * SEQ, SSM_WIDTH), even_w_out[i].astype(BF16), post)
        else:
            x2 = _odd_layer(x2, pre, odd_w_in[i].astype(BF16), pool_w[i].astype(BF16),
                            pool_scale[i].reshape(1, POOL_WIDTH), odd_w_out[i].astype(BF16), post)
    return x2.reshape(BATCH, SEQ, D_MODEL)
```

```python
import math

import jax
import jax.numpy as jnp
import numpy as np
from jax import lax
from jax.experimental import pallas as pl
from jax.experimental.pallas import tpu as pltpu

F32 = jnp.float32
BF16 = jnp.bfloat16

D_MODEL = 1024
BATCH = 8
SEQ = 2048
DEPTH = 4
HEAD_DIM = 64
ATT_WIDTH = 1024
ROT_DIM = 16
ROPE_THETA = 500000.0
SSM_WIDTH = 512
SSM_GROUP = 16
SSM_GROUPS = 32
SSM_STATE = 64
N_STATE = SSM_GROUPS * SSM_STATE
POOL_WIDTH = 2048
POOL_WINDOWS = (2, 4, 8, 16)
POOL_GROUP = 512
EVEN_IN = 4 * ATT_WIDTH + 2 * SSM_WIDTH
RMS_EPS = 1e-6
LOG2_E = math.log2(math.e)

LANES = 128
ATT_CHUNK = 128
ATT_GROUP = 4
NEG_BIG = -1e30
VMEM_LIMIT = 56 * 1024 * 1024

ROW_TILE = 512
OUT_TILE = 1024
SUB_TILE = 512
SSM_TC = 64
ODD_TILE = 1024
ODD_SUB = 512
POOL_HIST = 32


def _sigmoid(x):
    return 0.5 + 0.5 * jnp.tanh(0.5 * x)


def _silu(x):
    h = 0.5 * x
    return h + h * jnp.tanh(h)


def _rms(x, gain):
    return x * lax.rsqrt(jnp.mean(x * x, axis=-1, keepdims=True) + RMS_EPS) * gain


def _const_spec(shape):
    nd = len(shape)
    return pl.BlockSpec(shape, lambda *_: (0,) * nd, pipeline_mode=pl.Buffered(1))


def _layer_spec(stacked, layer):
    shape = stacked.shape[1:]
    nd = len(shape)
    return pl.BlockSpec((None,) + tuple(shape), lambda *_: (layer,) + (0,) * nd,
                        pipeline_mode=pl.Buffered(1))


def _even_in_kernel(x_ref, gain_ref, w_ref, cos_ref, sa_ref, sb_ref,
                    q_ref, k_ref, v_ref, g_ref, us_ref):
    hb = _rms(x_ref[...], gain_ref[...]).astype(BF16)
    cos = cos_ref[...]
    sa = sa_ref[...]
    sb = sb_ref[...]

    def proj(lo, width):
        return jnp.dot(hb, w_ref[:, lo:lo + width], preferred_element_type=F32)

    def rope_store(dst, y, scale):
        for c in range(ATT_WIDTH // LANES):
            yc = y[:, c * LANES:(c + 1) * LANES]
            r = (yc * cos + pltpu.roll(yc, LANES - ROT_DIM // 2, 1) * sa
                 + pltpu.roll(yc, ROT_DIM // 2, 1) * sb)
            dst[:, c * LANES:(c + 1) * LANES] = (r * scale).astype(dst.dtype)

    rope_store(q_ref, proj(0, ATT_WIDTH), HEAD_DIM ** -0.5 * LOG2_E)
    rope_store(k_ref, proj(ATT_WIDTH, ATT_WIDTH), 1.0)
    v_ref[...] = proj(2 * ATT_WIDTH, ATT_WIDTH).astype(v_ref.dtype)
    g_ref[...] = proj(3 * ATT_WIDTH, ATT_WIDTH).astype(g_ref.dtype)
    us_ref[...] = proj(4 * ATT_WIDTH, 2 * SSM_WIDTH).astype(us_ref.dtype)


def _even_in(x2, gains, layer, w_in, i, cos_t, sa_t, sb_t):
    tm = ROW_TILE
    nj = SEQ // tm
    row_spec = pl.BlockSpec((tm, D_MODEL), lambda b, j: (b * nj + j, 0))
    tab_spec = pl.BlockSpec((tm, LANES), lambda b, j: (j, 0))
    att_spec = pl.BlockSpec((tm, ATT_WIDTH), lambda b, j: (b * nj + j, 0))
    att_shape = jax.ShapeDtypeStruct((BATCH * SEQ, ATT_WIDTH), BF16)
    return pl.pallas_call(
        _even_in_kernel,
        grid=(BATCH, nj),
        in_specs=[row_spec, _layer_spec(gains, layer), _layer_spec(w_in, i),
                  tab_spec, tab_spec, tab_spec],
        out_specs=[att_spec] * 5,
        out_shape=[att_shape] * 5,
        compiler_params=pltpu.CompilerParams(
            dimension_semantics=("arbitrary", "arbitrary"), vmem_limit_bytes=VMEM_LIMIT),
        name="even_in_proj",
    )(x2, gains, w_in, cos_t, sa_t, sb_t)


N_CLS = 16
CLS_LEN = SEQ // N_CLS


def _attn_masks():
    ch = ATT_CHUNK
    rho = np.arange(ch)
    tau = 16 * (rho % 8) + rho // 8
    kk = np.arange(2 * ch)
    p1_full = np.where(kk[None, :] < ch, kk[None, :] >= tau[:, None],
                       (kk[None, :] - ch) <= tau[:, None])
    p1_first = np.arange(ch)[None, :] <= tau[:, None]
    a = 4 * (rho % 32) + rho // 32
    kap = np.arange(2 * ch)
    kk2 = 4 * (kap % 64) + kap // 64
    p2_full = (kk2[None, :] >= a[:, None]) & (kk2[None, :] <= a[:, None] + ch)
    kap1 = np.arange(ch)
    kk1 = 4 * (kap1 % 32) + kap1 // 32
    p2_first = kk1[None, :] <= a[:, None]
    p3 = np.arange(ch)[None, :] <= rho[:, None]
    valid = np.concatenate([p1_full, p1_first, p2_full, p2_first, p3], axis=1)
    return jnp.asarray(np.where(valid, 0.0, NEG_BIG), dtype=F32)


def _attn_kernel(q_ref, k_ref, v_ref, mask_ref, o_ref,
                 q32, k32, v32, q16f, q16, k16, v16, m_a, m_b, acc_a, acc_b, fin):
    ch = ATT_CHUNK
    lane = lax.broadcasted_iota(jnp.int32, (1, LANES), 1)
    is_a = lane < HEAD_DIM
    is_b = jnp.logical_not(is_a)
    state = ((m_a, acc_a), (m_b, acc_b))
    ones_row = jnp.ones((1, LANES), BF16)
    zeros_row = jnp.zeros((1, LANES), BF16)
    mask_p1_full = slice(0, 2 * ch)
    mask_p1_first = slice(2 * ch, 3 * ch)
    mask_p2_full = slice(3 * ch, 5 * ch)
    mask_p2_first = slice(5 * ch, 6 * ch)
    mask_p3 = slice(6 * ch, 7 * ch)

    q32[...] = q_ref[0].astype(F32)
    k32[...] = k_ref[0].astype(F32)
    v32[...] = v_ref[0].astype(F32)
    for r in range(N_CLS):
        rows = slice(r * CLS_LEN, (r + 1) * CLS_LEN)
        qv = q32[pl.ds(r, CLS_LEN, stride=N_CLS), :]
        q16f[rows, :] = qv
        q16[rows, :] = qv.astype(BF16)
        k16[rows, :] = k32[pl.ds(r, CLS_LEN, stride=N_CLS), :].astype(BF16)
        v16[rows, :] = v32[pl.ds(r, CLS_LEN, stride=N_CLS), :].astype(BF16)

    def gather(ref, slabs):
        parts = [ref[s:s + n, :] for s, n in slabs]
        return parts[0] if len(parts) == 1 else jnp.concatenate(parts, axis=0)

    def scatter(ref, slabs, val):
        off = 0
        for s, n in slabs:
            ref[s:s + n, :] = val[off:off + n, :]
            off += n

    def phase_scores(items):
        scores = []
        for load_q, load_k, load_v, mask, slabs in items:
            qc = load_q()
            kc = load_k()
            for is_h in (is_a, is_b):
                qh = jnp.where(is_h, qc, zeros_row)
                s = lax.dot_general(qh, kc, (((1,), (1,)), ((), ())),
                                    preferred_element_type=F32) + mask_ref[:, mask]
                scores.append((s, jnp.max(s, axis=1, keepdims=True)))
        return scores

    def phase_update(items, scores, has_state, finalize):
        partial = []
        n = 0
        for load_q, load_k, load_v, mask, slabs in items:
            vc = load_v()
            for h, is_h in enumerate((is_a, is_b)):
                s, m_c = scores[n]
                n += 1
                if has_state:
                    m_prev = gather(state[h][0], slabs)
                    m_new = jnp.maximum(m_prev, m_c)
                else:
                    m_prev = None
                    m_new = jnp.broadcast_to(m_c, (ch, LANES))
                m_k = jnp.concatenate([m_new] * (s.shape[1] // LANES), axis=1)
                p = jnp.exp2(s - m_k)
                vh = jnp.where(is_h, vc, ones_row)
                pv = jnp.dot(p.astype(BF16), vh, preferred_element_type=F32)
                partial.append((m_prev, m_new, pv))
        n = 0
        for load_q, load_k, load_v, mask, slabs in items:
            res = []
            for h in range(2):
                m_prev, m_new, pv = partial[n]
                n += 1
                if has_state:
                    a_new = jnp.exp2(m_prev - m_new) * gather(state[h][1], slabs) + pv
                else:
                    a_new = pv
                res.append(a_new)
                if not finalize:
                    scatter(state[h][0], slabs, m_new)
                    scatter(state[h][1], slabs, a_new)
            if finalize:
                num = jnp.where(is_a, res[0], res[1])
                den = jnp.where(is_a, pltpu.roll(res[0], HEAD_DIM, 1),
                                pltpu.roll(res[1], HEAD_DIM, 1))
                scatter(fin, slabs, num / den)

    groups = []

    def add_groups(items, has_state, finalize):
        for i in range(0, len(items), ATT_GROUP):
            groups.append((items[i:i + ATT_GROUP], has_state, finalize))

    per = ch // N_CLS
    items = []
    for i in range(SEQ // ch):
        slabs = [(r * CLS_LEN + per * i, per) for r in range(N_CLS)]
        base = i * ch
        lo = base - ch if i > 0 else base
        items.append((lambda slabs=slabs: gather(q16f, slabs).astype(BF16),
                      lambda lo=lo, base=base: k_ref[0, lo:base + ch, :],
                      lambda lo=lo, base=base: v_ref[0, lo:base + ch, :],
                      mask_p1_full if i > 0 else mask_p1_first, slabs))
    add_groups(items, False, False)

    quarter = ch // 4
    items = []
    for r4 in range(4):
        for ic in range(SEQ // (4 * ch)):
            cls = [(4 * j + r4) * CLS_LEN for j in range(4)]
            slabs = [(c + quarter * ic, quarter) for c in cls]
            if ic > 0:
                kslabs = [(c + quarter * (ic - 1), 2 * quarter) for c in cls]
                mask = mask_p2_full
            else:
                kslabs = slabs
                mask = mask_p2_first
            items.append((lambda slabs=slabs: gather(q16, slabs),
                          lambda kslabs=kslabs: gather(k16, kslabs),
                          lambda kslabs=kslabs: gather(v16, kslabs), mask, slabs))
    add_groups(items, True, False)

    items = []
    for r in range(N_CLS):
        slabs = [(r * CLS_LEN, CLS_LEN)]
        items.append((lambda slabs=slabs: gather(q16, slabs),
                      lambda slabs=slabs: gather(k16, slabs),
                      lambda slabs=slabs: gather(v16, slabs), mask_p3, slabs))
    add_groups(items, True, True)

    pending = None
    for items, has_state, finalize in groups:
        scores = phase_scores(items)
        if pending is not None:
            phase_update(*pending)
        pending = (items, scores, has_state, finalize)
    phase_update(*pending)

    row = lax.broadcasted_iota(jnp.int32, (ch, ch), 0)
    col = lax.broadcasted_iota(jnp.int32, (ch, ch), 1)
    shuffle = jnp.where(col == per * (row % N_CLS) + row // N_CLS, 1.0, 0.0).astype(BF16)
    for i in range(SEQ // ch):
        slab = gather(fin, [(r * CLS_LEN + per * i, per) for r in range(N_CLS)])
        o_ref[0, i * ch:(i + 1) * ch, :] = jnp.dot(
            shuffle, slab.astype(BF16), preferred_element_type=F32).astype(o_ref.dtype)


def _attention(q, k, v):
    blk = pl.BlockSpec((1, SEQ, LANES), lambda b, h: (b, 0, h))
    f32_scr = pltpu.VMEM((SEQ, LANES), F32)
    bf_scr = pltpu.VMEM((SEQ, LANES), BF16)
    masks = _attn_masks()
    return pl.pallas_call(
        _attn_kernel,
        grid=(BATCH, ATT_WIDTH // LANES),
        in_specs=[blk, blk, blk, _const_spec(masks.shape)],
        out_specs=blk,
        out_shape=jax.ShapeDtypeStruct((BATCH, SEQ, ATT_WIDTH), BF16),
        scratch_shapes=[f32_scr] * 4 + [bf_scr] * 3 + [f32_scr] * 5,
        compiler_params=pltpu.CompilerParams(
            dimension_semantics=("arbitrary", "arbitrary"), vmem_limit_bytes=VMEM_LIMIT),
        name="dilated_attention",
    )(q, k, v, masks)


def _ssm_kernel(us_ref, perm_ref, lam_re_ref, lam_im_ref, wb_ref, wc_re_ref, wc_im_ref,
                d_ref, gw_ref, gb_ref, o_ref, xr, xi, st_re, st_im, y_tb, y_bt):
    tc = SSM_TC
    rows = tc * BATCH
    n_chunk = SSM_WIDTH // LANES
    cpc = N_STATE // n_chunk

    @pl.when(pl.program_id(0) == 0)
    def _():
        st_re[...] = jnp.zeros_like(st_re)
        st_im[...] = jnp.zeros_like(st_im)

    ub = us_ref[:, :, :SSM_WIDTH].reshape(rows, SSM_WIDTH)
    u_tb = jnp.dot(perm_ref[...], ub, preferred_element_type=F32).astype(BF16)
    for c in range(n_chunk):
        cols = slice(c * cpc, (c + 1) * cpc)
        bu = jnp.dot(u_tb[:, c * LANES:(c + 1) * LANES], wb_ref[c], preferred_element_type=F32)
        xr[:, cols] = bu[:, :cpc]
        xi[:, cols] = bu[:, cpc:]
        lr = lam_re_ref[:, cols]
        li = lam_im_ref[:, cols]
        sr = st_re[:, cols]
        si = st_im[:, cols]
        for t in range(tc):
            trow = slice(t * BATCH, (t + 1) * BATCH)
            sr, si = (lr * sr - li * si + xr[trow, cols], lr * si + li * sr + xi[trow, cols])
            xr[trow, cols] = sr
            xi[trow, cols] = si
        st_re[:, cols] = sr
        st_im[:, cols] = si
        y_tb[c] = (
            jnp.dot(xr[:, cols].astype(BF16), wc_re_ref[c], preferred_element_type=F32)
            + jnp.dot(xi[:, cols].astype(BF16), wc_im_ref[c], preferred_element_type=F32))
        lanes = slice(c * LANES, (c + 1) * LANES)
        y_c = jnp.concatenate(
            [y_tb[c, pl.ds(b, tc, stride=BATCH), :] for b in range(BATCH)], axis=0)
        y_c = jax.nn.gelu(y_c + d_ref[:, lanes] * ub[:, lanes].astype(F32))
        y_bt[:, lanes] = y_c
    y = y_bt[...]
    z = jnp.dot(y.astype(BF16), gw_ref[...], preferred_element_type=F32) + gb_ref[...]
    gs = us_ref[:, :, SSM_WIDTH:].reshape(rows, SSM_WIDTH).astype(F32)
    out = y * _sigmoid(z) * _silu(gs)
    o_ref[...] = out.reshape(BATCH, tc, SSM_WIDTH).astype(o_ref.dtype)


def _ssm(us3, params, i):
    tc = SSM_TC
    rows = tc * BATCH
    dst_row = np.arange(rows)
    perm = jnp.asarray((dst_row[:, None] % BATCH) * tc + dst_row[:, None] // BATCH
                       == dst_row[None, :], dtype=BF16)
    return pl.pallas_call(
        _ssm_kernel,
        grid=(SEQ // tc,),
        in_specs=[pl.BlockSpec((BATCH, tc, 2 * SSM_WIDTH), lambda t: (0, t, 0)),
                  _const_spec((rows, rows))] + [_layer_spec(p, i) for p in params],
        out_specs=pl.BlockSpec((BATCH, tc, SSM_WIDTH), lambda t: (0, t, 0)),
        out_shape=jax.ShapeDtypeStruct((BATCH, SEQ, SSM_WIDTH), BF16),
        scratch_shapes=[pltpu.VMEM((rows, N_STATE), F32), pltpu.VMEM((rows, N_STATE), F32),
                        pltpu.VMEM((BATCH, N_STATE), F32), pltpu.VMEM((BATCH, N_STATE), F32),
                        pltpu.VMEM((SSM_WIDTH // LANES, rows, LANES), F32),
                        pltpu.VMEM((rows, SSM_WIDTH), F32)],
        compiler_params=pltpu.CompilerParams(
            dimension_semantics=("arbitrary",), vmem_limit_bytes=VMEM_LIMIT),
        name="s5_branch",
    )(us3, perm, *params)


def _even_out_kernel(x_ref, att_ref, g_ref, ssm_ref, w_ref, gain_ref, o_ref):
    for s in range(OUT_TILE // SUB_TILE):
        rows = slice(s * SUB_TILE, (s + 1) * SUB_TILE)
        gated = (att_ref[rows, :].astype(F32) * _silu(g_ref[rows, :].astype(F32))).astype(BF16)
        y = jnp.dot(gated, w_ref[:ATT_WIDTH, :], preferred_element_type=F32)
        y = y + jnp.dot(ssm_ref[rows, :], w_ref[ATT_WIDTH:, :], preferred_element_type=F32)
        o_ref[rows, :] = x_ref[rows, :] + _rms(y, gain_ref[...])


def _even_out(x2, att2, g2, ssm2, w_out, i, gains, layer):
    tm = OUT_TILE
    nj = SEQ // tm
    row_spec = pl.BlockSpec((tm, D_MODEL), lambda b, j: (b * nj + j, 0))
    att_spec = pl.BlockSpec((tm, ATT_WIDTH), lambda b, j: (b * nj + j, 0))
    return pl.pallas_call(
        _even_out_kernel,
        grid=(BATCH, nj),
        in_specs=[row_spec, att_spec, att_spec,
                  pl.BlockSpec((tm, SSM_WIDTH), lambda b, j: (b * nj + j, 0)),
                  _layer_spec(w_out, i), _layer_spec(gains, layer)],
        out_specs=row_spec,
        out_shape=jax.ShapeDtypeStruct((BATCH * SEQ, D_MODEL), F32),
        compiler_params=pltpu.CompilerParams(
            dimension_semantics=("arbitrary", "arbitrary"), vmem_limit_bytes=VMEM_LIMIT),
        name="even_out_proj",
    )(x2, att2, g2, ssm2, w_out, gains)


def _odd_kernel(x_ref, pre_ref, w_in_ref, pw_ref, ps_ref, w_out_ref, post_ref, o_ref,
                ext, tmp0, tmp1):
    tm = ODD_TILE
    hist = POOL_HIST
    total = hist + tm
    j = pl.program_id(1)

    @pl.when(j == 0)
    def _():
        ext[0:hist, :] = jnp.zeros((hist, POOL_WIDTH), F32)

    tmps = (tmp0, tmp1)
    sub = ODD_SUB
    for s in range(tm // sub):
        lo = hist + s * sub
        hi = lo + sub
        x = x_ref[s * sub:(s + 1) * sub, :]
        hb = _rms(x, pre_ref[...]).astype(BF16)
        ext[lo:hi, :] = jnp.dot(hb, w_in_ref[:, :POOL_WIDTH], preferred_element_type=F32)
        t_idx = j * tm + s * sub + lax.broadcasted_iota(jnp.int32, (sub, 1), 0)
        inv_count = 1.0 / (t_idx + 1).astype(F32)
        y = None
        for g, w in enumerate(POOL_WINDOWS):
            cols = slice(g * POOL_GROUP, (g + 1) * POOL_GROUP)
            levels = int(math.log2(w))
            src_ref, src_cols = ext, cols
            for lev in range(levels):
                shift = 2 ** lev
                start = lo - 8 * (levels - 1 - lev)
                val = (src_ref[start:hi, src_cols] + src_ref[start - shift:hi - shift, src_cols])
                if lev < levels - 1:
                    dst = tmps[lev % 2]
                    dst[start:hi, :] = val
                    src_ref, src_cols = dst, slice(None)
            u_g = ext[lo:hi, cols]
            mixed = val * jnp.maximum(inv_count, 1.0 / w) - u_g
            yg = jnp.dot(mixed.astype(BF16), pw_ref[g], preferred_element_type=F32)
            gate = jnp.dot(hb, w_in_ref[:, POOL_WIDTH + g * POOL_GROUP:
                                        POOL_WIDTH + (g + 1) * POOL_GROUP],
                           preferred_element_type=F32)
            yg = yg * ps_ref[:, cols] * _silu(gate)
            part = jnp.dot(yg.astype(BF16), w_out_ref[cols, :], preferred_element_type=F32)
            y = part if y is None else y + part
        o_ref[s * sub:(s + 1) * sub, :] = x + _rms(y, post_ref[...])
    ext[0:hist, :] = ext[tm:total, :]


def _odd_layer(x2, pre, w_in, pool_w, pool_scale, w_out, post, i, layer):
    tm = ODD_TILE
    nj = SEQ // tm
    row_spec = pl.BlockSpec((tm, D_MODEL), lambda b, j: (b * nj + j, 0))
    return pl.pallas_call(
        _odd_kernel,
        grid=(BATCH, nj),
        in_specs=[row_spec, _layer_spec(pre, layer), _layer_spec(w_in, i),
                  _layer_spec(pool_w, i), _layer_spec(pool_scale, i),
                  _layer_spec(w_out, i), _layer_spec(post, layer)],
        out_specs=row_spec,
        out_shape=jax.ShapeDtypeStruct((BATCH * SEQ, D_MODEL), F32),
        scratch_shapes=[pltpu.VMEM((POOL_HIST + tm, POOL_WIDTH), F32),
                        pltpu.VMEM((POOL_HIST + tm, POOL_GROUP), F32),
                        pltpu.VMEM((POOL_HIST + tm, POOL_GROUP), F32)],
        compiler_params=pltpu.CompilerParams(
            dimension_semantics=("arbitrary", "arbitrary"), vmem_limit_bytes=VMEM_LIMIT),
        name="odd_layer",
    )(x2, pre, w_in, pool_w, pool_scale, w_out, post)


def _rope_tables():
    half = ROT_DIM // 2
    inv_freq = (np.float32(ROPE_THETA)
                ** (-np.arange(0, ROT_DIM, 2, dtype=np.float32) / np.float32(ROT_DIM)))
    ang = np.arange(SEQ, dtype=np.float32)[:, None] * inv_freq[None, :].astype(np.float32)
    cos, sin = np.cos(ang).astype(np.float32), np.sin(ang).astype(np.float32)
    zeros = np.zeros((SEQ, HEAD_DIM - ROT_DIM), np.float32)
    zeros_h = np.zeros((SEQ, half), np.float32)
    cos_h = np.concatenate([cos, cos, np.ones_like(zeros)], axis=1)
    sa_h = np.concatenate([-sin, zeros_h, zeros], axis=1)
    sb_h = np.concatenate([zeros_h, sin, zeros], axis=1)
    rep = LANES // HEAD_DIM
    return tuple(jnp.asarray(np.tile(t, (1, rep))) for t in (cos_h, sa_h, sb_h))


def _ssm_params(a_re, a_im, log_dt, b_re, b_im, c_re, c_im):
    n_layer = a_re.shape[0]
    lam = lax.complex(a_re.astype(F32), a_im.astype(F32))
    dt = jnp.exp(log_dt.astype(F32))[..., None]
    lam_bar = jnp.exp(lam * dt)
    b_bar = ((lam_bar - 1.0) / lam)[..., None] * lax.complex(b_re.astype(F32), b_im.astype(F32))
    lam_re = jnp.broadcast_to(jnp.real(lam_bar).reshape(n_layer, 1, N_STATE),
                              (n_layer, BATCH, N_STATE))
    lam_im = jnp.broadcast_to(jnp.imag(lam_bar).reshape(n_layer, 1, N_STATE),
                              (n_layer, BATCH, N_STATE))
    gpc = LANES // SSM_GROUP
    n_chunk = SSM_GROUPS // gpc
    eye = jnp.eye(gpc, dtype=F32)

    def block_diag_in(m):
        m = m.reshape(n_layer, n_chunk, gpc, SSM_STATE, SSM_GROUP)
        return jnp.einsum('lcgph,gk->lcghkp', m, eye).reshape(
            n_layer, n_chunk, gpc * SSM_GROUP, gpc * SSM_STATE)

    def block_diag_out(m):
        m = m.reshape(n_layer, n_chunk, gpc, SSM_GROUP, SSM_STATE)
        return jnp.einsum('lcghp,gk->lcgpkh', m, eye).reshape(
            n_layer, n_chunk, gpc * SSM_STATE, gpc * SSM_GROUP)

    wb = jnp.concatenate([block_diag_in(jnp.real(b_bar)), block_diag_in(jnp.imag(b_bar))],
                         axis=3).astype(BF16)
    wc_re = block_diag_out(c_re.astype(F32)).astype(BF16)
    wc_im = block_diag_out(-c_im.astype(F32)).astype(BF16)
    return lam_re, lam_im, wb, wc_re, wc_im


def kernel(x, pre_norm, post_norm, even_w_in, even_w_out, ssm_a_re, ssm_a_im, ssm_log_dt,
           ssm_b_re, ssm_b_im, ssm_c_re, ssm_c_im, ssm_d, ssm_glu_w, ssm_glu_b,
           odd_w_in, pool_w, pool_scale, odd_w_out):
    cos_t, sa_t, sb_t = _rope_tables()
    pre = pre_norm.astype(F32).reshape(DEPTH, 1, D_MODEL)
    post = post_norm.astype(F32).reshape(DEPTH, 1, D_MODEL)
    even_w_in_b = even_w_in.astype(BF16)
    even_w_out_b = even_w_out.astype(BF16)
    odd_w_in_b = odd_w_in.astype(BF16)
    odd_w_out_b = odd_w_out.astype(BF16)
    pool_w_b = pool_w.astype(BF16)
    pool_scale3 = pool_scale.astype(F32).reshape(-1, 1, POOL_WIDTH)
    ssm_params = _ssm_params(ssm_a_re, ssm_a_im, ssm_log_dt, ssm_b_re, ssm_b_im,
                             ssm_c_re, ssm_c_im) + (
        ssm_d.astype(F32).reshape(-1, 1, SSM_WIDTH), ssm_glu_w.astype(BF16),
        ssm_glu_b.astype(F32).reshape(-1, 1, SSM_WIDTH))
    x2 = x.reshape(BATCH * SEQ, D_MODEL)
    for layer in range(DEPTH):
        i = layer // 2
        if layer % 2 == 0:
            q, k, v, g, us = _even_in(x2, pre, layer, even_w_in_b, i, cos_t, sa_t, sb_t)
            shape3 = (BATCH, SEQ, ATT_WIDTH)
            att = _attention(q.reshape(shape3), k.reshape(shape3), v.reshape(shape3))
            ssm = _ssm(us.reshape(BATCH, SEQ, 2 * SSM_WIDTH), ssm_params, i)
            x2 = _even_out(x2, att.reshape(BATCH * SEQ, ATT_WIDTH), g,
                           ssm.reshape(BATCH * SEQ, SSM_WIDTH), even_w_out_b, i, post, layer)
        else:
            x2 = _odd_layer(x2, pre, odd_w_in_b, pool_w_b, pool_scale3, odd_w_out_b, post,
                            i, layer)
    return x2.reshape(BATCH, SEQ, D_MODEL)
```

```python
import math

import jax
import jax.numpy as jnp
import numpy as np
from jax import lax
from jax.experimental import pallas as pl
from jax.experimental.pallas import tpu as pltpu

F32 = jnp.float32
BF16 = jnp.bfloat16

D_MODEL = 1024
BATCH = 8
SEQ = 2048
DEPTH = 4
HEAD_DIM = 64
ATT_WIDTH = 1024
ROT_DIM = 16
ROPE_THETA = 500000.0
SSM_WIDTH = 512
SSM_GROUP = 16
SSM_GROUPS = 32
SSM_STATE = 64
N_STATE = SSM_GROUPS * SSM_STATE
POOL_WIDTH = 2048
POOL_WINDOWS = (2, 4, 8, 16)
POOL_GROUP = 512
EVEN_IN = 4 * ATT_WIDTH + 2 * SSM_WIDTH
RMS_EPS = 1e-6
LOG2_E = math.log2(math.e)

LANES = 128
ATT_CHUNK = 128
ATT_GROUP = 4
NEG_BIG = -1e30
VMEM_LIMIT = 56 * 1024 * 1024

ROW_TILE = 512
OUT_TILE = 1024
SUB_TILE = 512
SSM_TC = 64
ODD_TILE = 1024
ODD_SUB = 512
POOL_HIST = 32


def _sigmoid(x):
    return 0.5 + 0.5 * jnp.tanh(0.5 * x)


def _silu(x):
    h = 0.5 * x
    return h + h * jnp.tanh(h)


def _rms(x, gain):
    return x * lax.rsqrt(jnp.mean(x * x, axis=-1, keepdims=True) + RMS_EPS) * gain


def _const_spec(shape):
    nd = len(shape)
    return pl.BlockSpec(shape, lambda *_: (0,) * nd, pipeline_mode=pl.Buffered(1))


def _layer_spec(stacked, layer):
    shape = stacked.shape[1:]
    nd = len(shape)
    return pl.BlockSpec((None,) + tuple(shape), lambda *_: (layer,) + (0,) * nd,
                        pipeline_mode=pl.Buffered(1))


def _even_in_kernel(x_ref, gain_ref, w_ref, cos_ref, sa_ref, sb_ref,
                    q16_ref, k16_ref, v16_ref, g_ref, us_ref, h_nat, h_cls):
    tm = ROW_TILE
    per_cls = tm // N_CLS
    h = _rms(x_ref[...], gain_ref[...])
    hb = h.astype(BF16)
    for c in range(D_MODEL // LANES):
        h_nat[c] = h[:, c * LANES:(c + 1) * LANES]
    for c in range(D_MODEL // LANES):
        for cls in range(N_CLS):
            h_cls[cls * per_cls:(cls + 1) * per_cls, c * LANES:(c + 1) * LANES] = (
                h_nat[c, pl.ds(cls, per_cls, stride=N_CLS), :].astype(BF16))
    hb_cls = h_cls[...]
    cos = cos_ref[...]
    sa = sa_ref[...]
    sb = sb_ref[...]

    def rope_store(dst, y, scale):
        for c in range(ATT_WIDTH // LANES):
            yc = y[:, c * LANES:(c + 1) * LANES]
            r = (yc * cos + pltpu.roll(yc, LANES - ROT_DIM // 2, 1) * sa
                 + pltpu.roll(yc, ROT_DIM // 2, 1) * sb)
            dst[0, :, :, c * LANES:(c + 1) * LANES] = (r * scale).reshape(N_CLS, per_cls, LANES)

    def proj(lhs, lo, width):
        return jnp.dot(lhs, w_ref[:, lo:lo + width], preferred_element_type=F32)

    rope_store(q16_ref, proj(hb_cls, 0, ATT_WIDTH), HEAD_DIM ** -0.5 * LOG2_E)
    rope_store(k16_ref, proj(hb_cls, ATT_WIDTH, ATT_WIDTH), 1.0)
    v16_ref[0] = proj(hb_cls, 2 * ATT_WIDTH, ATT_WIDTH).reshape(N_CLS, per_cls, ATT_WIDTH)
    g_ref[...] = proj(hb, 3 * ATT_WIDTH, ATT_WIDTH).astype(g_ref.dtype)
    us_ref[...] = proj(hb, 4 * ATT_WIDTH, 2 * SSM_WIDTH).astype(us_ref.dtype)


def _even_in(x2, gains, layer, w_in, i, cos_t, sa_t, sb_t):
    tm = ROW_TILE
    nj = SEQ // tm
    row_spec = pl.BlockSpec((tm, D_MODEL), lambda b, j: (b * nj + j, 0))
    tab_spec = pl.BlockSpec((tm, LANES), lambda b, j: (j, 0))
    att_spec = pl.BlockSpec((tm, ATT_WIDTH), lambda b, j: (b * nj + j, 0))
    cls_spec = pl.BlockSpec((1, N_CLS, tm // N_CLS, ATT_WIDTH), lambda b, j: (b, 0, j, 0))
    att_shape = jax.ShapeDtypeStruct((BATCH * SEQ, ATT_WIDTH), BF16)
    cls_shape = jax.ShapeDtypeStruct((BATCH, N_CLS, CLS_LEN, ATT_WIDTH), F32)
    return pl.pallas_call(
        _even_in_kernel,
        grid=(BATCH, nj),
        in_specs=[row_spec, _layer_spec(gains, layer), _layer_spec(w_in, i),
                  tab_spec, tab_spec, tab_spec],
        out_specs=[cls_spec, cls_spec, cls_spec, att_spec, att_spec],
        out_shape=[cls_shape, cls_shape, cls_shape, att_shape, att_shape],
        scratch_shapes=[pltpu.VMEM((D_MODEL // LANES, tm, LANES), F32),
                        pltpu.VMEM((tm, D_MODEL), BF16)],
        compiler_params=pltpu.CompilerParams(
            dimension_semantics=("arbitrary", "arbitrary"), vmem_limit_bytes=VMEM_LIMIT),
        name="even_in_proj",
    )(x2, gains, w_in, cos_t, sa_t, sb_t)


N_CLS = 16
CLS_LEN = SEQ // N_CLS


def _attn_masks():
    ch = ATT_CHUNK
    rho = np.arange(ch)
    tau = 16 * (rho % 8) + rho // 8
    kap = np.arange(2 * ch)
    kk = 16 * (kap % 16) + kap // 16
    p1_full = np.where(kk[None, :] < ch, kk[None, :] >= tau[:, None],
                       (kk[None, :] - ch) <= tau[:, None])
    kap1 = np.arange(ch)
    p1_first = (16 * (kap1 % 8) + kap1 // 8)[None, :] <= tau[:, None]
    a = 4 * (rho % 32) + rho // 32
    kap = np.arange(2 * ch)
    kk2 = 4 * (kap % 64) + kap // 64
    p2_full = (kk2[None, :] >= a[:, None]) & (kk2[None, :] <= a[:, None] + ch)
    kap1 = np.arange(ch)
    kk1 = 4 * (kap1 % 32) + kap1 // 32
    p2_first = kk1[None, :] <= a[:, None]
    p3 = np.arange(ch)[None, :] <= rho[:, None]
    valid = np.concatenate([p1_full, p1_first, p2_full, p2_first, p3], axis=1)
    return jnp.asarray(np.where(valid, 0.0, NEG_BIG), dtype=F32)


def _attn_kernel(q16_ref, k16_ref, v16_ref, mask_ref, o_ref, m_a, m_b, acc_a, acc_b, fin):
    ch = ATT_CHUNK
    lane = lax.broadcasted_iota(jnp.int32, (1, LANES), 1)
    is_a = lane < HEAD_DIM
    is_b = jnp.logical_not(is_a)
    state = ((m_a, acc_a), (m_b, acc_b))
    ones_row = jnp.ones((1, LANES), BF16)
    zeros_row = jnp.zeros((1, LANES), BF16)
    mask_p1_full = slice(0, 2 * ch)
    mask_p1_first = slice(2 * ch, 3 * ch)
    mask_p2_full = slice(3 * ch, 5 * ch)
    mask_p2_first = slice(5 * ch, 6 * ch)
    mask_p3 = slice(6 * ch, 7 * ch)

    def gather(ref, slabs):
        parts = [ref[s:s + n, :] for s, n in slabs]
        return parts[0] if len(parts) == 1 else jnp.concatenate(parts, axis=0)

    def gather_cls(ref, slabs):
        parts = [ref[0, s // CLS_LEN, s % CLS_LEN:s % CLS_LEN + n, :] for s, n in slabs]
        return parts[0] if len(parts) == 1 else jnp.concatenate(parts, axis=0)

    def scatter(ref, slabs, val):
        off = 0
        for s, n in slabs:
            ref[s:s + n, :] = val[off:off + n, :]
            off += n

    def phase_scores(items):
        scores = []
        for load_q, load_k, load_v, mask, slabs in items:
            qc = load_q()
            kc = load_k()
            for is_h in (is_a, is_b):
                qh = jnp.where(is_h, qc, zeros_row)
                s = lax.dot_general(qh, kc, (((1,), (1,)), ((), ())),
                                    preferred_element_type=F32) + mask_ref[:, mask]
                scores.append((s, jnp.max(s, axis=1, keepdims=True)))
        return scores

    def phase_update(items, scores, has_state, finalize):
        partial = []
        n = 0
        for load_q, load_k, load_v, mask, slabs in items:
            vc = load_v()
            for h, is_h in enumerate((is_a, is_b)):
                s, m_c = scores[n]
                n += 1
                if has_state:
                    m_prev = gather(state[h][0], slabs)
                    m_new = jnp.maximum(m_prev, m_c)
                else:
                    m_prev = None
                    m_new = jnp.broadcast_to(m_c, (ch, LANES))
                m_k = jnp.concatenate([m_new] * (s.shape[1] // LANES), axis=1)
                p = jnp.exp2(s - m_k)
                vh = jnp.where(is_h, vc, ones_row)
                pv = jnp.dot(p.astype(BF16), vh, preferred_element_type=F32)
                partial.append((m_prev, m_new, pv))
        n = 0
        for load_q, load_k, load_v, mask, slabs in items:
            res = []
            for h in range(2):
                m_prev, m_new, pv = partial[n]
                n += 1
                if has_state:
                    a_new = jnp.exp2(m_prev - m_new) * gather(state[h][1], slabs) + pv
                else:
                    a_new = pv
                res.append(a_new)
                if not finalize:
                    scatter(state[h][0], slabs, m_new)
                    scatter(state[h][1], slabs, a_new)
            if finalize:
                num = jnp.where(is_a, res[0], res[1])
                den = jnp.where(is_a, pltpu.roll(res[0], HEAD_DIM, 1),
                                pltpu.roll(res[1], HEAD_DIM, 1))
                scatter(fin, slabs, num / den)

    groups = []

    def add_groups(items, has_state, finalize):
        for i in range(0, len(items), ATT_GROUP):
            groups.append((items[i:i + ATT_GROUP], has_state, finalize))

    def block(slabs, kslabs, mask):
        return (lambda: gather_cls(q16_ref, slabs).astype(BF16),
                lambda: gather_cls(k16_ref, kslabs).astype(BF16),
                lambda: gather_cls(v16_ref, kslabs).astype(BF16), mask, slabs)

    per = ch // N_CLS
    items = []
    for i in range(SEQ // ch):
        slabs = [(r * CLS_LEN + per * i, per) for r in range(N_CLS)]
        if i > 0:
            kslabs = [(r * CLS_LEN + per * (i - 1), 2 * per) for r in range(N_CLS)]
            items.append(block(slabs, kslabs, mask_p1_full))
        else:
            items.append(block(slabs, slabs, mask_p1_first))
    add_groups(items, False, False)

    quarter = ch // 4
    items = []
    for r4 in range(4):
        for ic in range(SEQ // (4 * ch)):
            cls = [(4 * j + r4) * CLS_LEN for j in range(4)]
            slabs = [(c + quarter * ic, quarter) for c in cls]
            if ic > 0:
                kslabs = [(c + quarter * (ic - 1), 2 * quarter) for c in cls]
                items.append(block(slabs, kslabs, mask_p2_full))
            else:
                items.append(block(slabs, slabs, mask_p2_first))
    add_groups(items, True, False)

    items = []
    for r in range(N_CLS):
        slabs = [(r * CLS_LEN, CLS_LEN)]
        items.append(block(slabs, slabs, mask_p3))
    add_groups(items, True, True)

    pending = None
    for items, has_state, finalize in groups:
        scores = phase_scores(items)
        if pending is not None:
            phase_update(*pending)
        pending = (items, scores, has_state, finalize)
    phase_update(*pending)

    row = lax.broadcasted_iota(jnp.int32, (ch, ch), 0)
    col = lax.broadcasted_iota(jnp.int32, (ch, ch), 1)
    shuffle = jnp.where(col == per * (row % N_CLS) + row // N_CLS, 1.0, 0.0).astype(BF16)
    for i in range(SEQ // ch):
        slab = gather(fin, [(r * CLS_LEN + per * i, per) for r in range(N_CLS)])
        o_ref[0, i * ch:(i + 1) * ch, :] = jnp.dot(
            shuffle, slab.astype(BF16), preferred_element_type=F32).astype(o_ref.dtype)


def _attention(q16, k16, v16):
    blk = pl.BlockSpec((1, SEQ, LANES), lambda b, h: (b, 0, h))
    cls_blk = pl.BlockSpec((1, N_CLS, CLS_LEN, LANES), lambda b, h: (b, 0, 0, h))
    f32_scr = pltpu.VMEM((SEQ, LANES), F32)
    masks = _attn_masks()
    return pl.pallas_call(
        _attn_kernel,
        grid=(BATCH, ATT_WIDTH // LANES),
        in_specs=[cls_blk, cls_blk, cls_blk, _const_spec(masks.shape)],
        out_specs=blk,
        out_shape=jax.ShapeDtypeStruct((BATCH, SEQ, ATT_WIDTH), BF16),
        scratch_shapes=[f32_scr] * 5,
        compiler_params=pltpu.CompilerParams(
            dimension_semantics=("arbitrary", "arbitrary"), vmem_limit_bytes=VMEM_LIMIT),
        name="dilated_attention",
    )(q16, k16, v16, masks)


def _ssm_kernel(us_ref, perm_ref, lam_re_ref, lam_im_ref, wb_ref, wc_re_ref, wc_im_ref,
                d_ref, gw_ref, gb_ref, o_ref, xr, xi, st_re, st_im, y_tb, y_bt):
    tc = SSM_TC
    rows = tc * BATCH
    n_chunk = SSM_WIDTH // LANES
    cpc = N_STATE // n_chunk

    @pl.when(pl.program_id(0) == 0)
    def _():
        st_re[...] = jnp.zeros_like(st_re)
        st_im[...] = jnp.zeros_like(st_im)

    ub = us_ref[:, :, :SSM_WIDTH].reshape(rows, SSM_WIDTH)
    u_tb = jnp.dot(perm_ref[...], ub, preferred_element_type=F32).astype(BF16)
    for c in range(n_chunk):
        cols = slice(c * cpc, (c + 1) * cpc)
        bu = jnp.dot(u_tb[:, c * LANES:(c + 1) * LANES], wb_ref[c], preferred_element_type=F32)
        xr[:, cols] = bu[:, :cpc]
        xi[:, cols] = bu[:, cpc:]
        lr = lam_re_ref[:, cols]
        li = lam_im_ref[:, cols]
        sr = st_re[:, cols]
        si = st_im[:, cols]
        for t in range(tc):
            trow = slice(t * BATCH, (t + 1) * BATCH)
            sr, si = (lr * sr - li * si + xr[trow, cols], lr * si + li * sr + xi[trow, cols])
            xr[trow, cols] = sr
            xi[trow, cols] = si
        st_re[:, cols] = sr
        st_im[:, cols] = si
        y_tb[c] = (
            jnp.dot(xr[:, cols].astype(BF16), wc_re_ref[c], preferred_element_type=F32)
            + jnp.dot(xi[:, cols].astype(BF16), wc_im_ref[c], preferred_element_type=F32))
        lanes = slice(c * LANES, (c + 1) * LANES)
        y_c = jnp.concatenate(
            [y_tb[c, pl.ds(b, tc, stride=BATCH), :] for b in range(BATCH)], axis=0)
        y_c = jax.nn.gelu(y_c + d_ref[:, lanes] * ub[:, lanes].astype(F32))
        y_bt[:, lanes] = y_c
    y = y_bt[...]
    z = jnp.dot(y.astype(BF16), gw_ref[...], preferred_element_type=F32) + gb_ref[...]
    gs = us_ref[:, :, SSM_WIDTH:].reshape(rows, SSM_WIDTH).astype(F32)
    out = y * _sigmoid(z) * _silu(gs)
    o_ref[...] = out.reshape(BATCH, tc, SSM_WIDTH).astype(o_ref.dtype)


def _ssm(us3, params, i):
    tc = SSM_TC
    rows = tc * BATCH
    dst_row = np.arange(rows)
    perm = jnp.asarray((dst_row[:, None] % BATCH) * tc + dst_row[:, None] // BATCH
                       == dst_row[None, :], dtype=BF16)
    return pl.pallas_call(
        _ssm_kernel,
        grid=(SEQ // tc,),
        in_specs=[pl.BlockSpec((BATCH, tc, 2 * SSM_WIDTH), lambda t: (0, t, 0)),
                  _const_spec((rows, rows))] + [_layer_spec(p, i) for p in params],
        out_specs=pl.BlockSpec((BATCH, tc, SSM_WIDTH), lambda t: (0, t, 0)),
        out_shape=jax.ShapeDtypeStruct((BATCH, SEQ, SSM_WIDTH), BF16),
        scratch_shapes=[pltpu.VMEM((rows, N_STATE), F32), pltpu.VMEM((rows, N_STATE), F32),
                        pltpu.VMEM((BATCH, N_STATE), F32), pltpu.VMEM((BATCH, N_STATE), F32),
                        pltpu.VMEM((SSM_WIDTH // LANES, rows, LANES), F32),
                        pltpu.VMEM((rows, SSM_WIDTH), F32)],
        compiler_params=pltpu.CompilerParams(
            dimension_semantics=("arbitrary",), vmem_limit_bytes=VMEM_LIMIT),
        name="s5_branch",
    )(us3, perm, *params)


def _even_out_kernel(x_ref, att_ref, g_ref, ssm_ref, w_ref, gain_ref, o_ref):
    for s in range(OUT_TILE // SUB_TILE):
        rows = slice(s * SUB_TILE, (s + 1) * SUB_TILE)
        gated = (att_ref[rows, :].astype(F32) * _silu(g_ref[rows, :].astype(F32))).astype(BF16)
        y = jnp.dot(gated, w_ref[:ATT_WIDTH, :], preferred_element_type=F32)
        y = y + jnp.dot(ssm_ref[rows, :], w_ref[ATT_WIDTH:, :], preferred_element_type=F32)
        o_ref[rows, :] = x_ref[rows, :] + _rms(y, gain_ref[...])


def _even_out(x2, att2, g2, ssm2, w_out, i, gains, layer):
    tm = OUT_TILE
    nj = SEQ // tm
    row_spec = pl.BlockSpec((tm, D_MODEL), lambda b, j: (b * nj + j, 0))
    att_spec = pl.BlockSpec((tm, ATT_WIDTH), lambda b, j: (b * nj + j, 0))
    return pl.pallas_call(
        _even_out_kernel,
        grid=(BATCH, nj),
        in_specs=[row_spec, att_spec, att_spec,
                  pl.BlockSpec((tm, SSM_WIDTH), lambda b, j: (b * nj + j, 0)),
                  _layer_spec(w_out, i), _layer_spec(gains, layer)],
        out_specs=row_spec,
        out_shape=jax.ShapeDtypeStruct((BATCH * SEQ, D_MODEL), F32),
        compiler_params=pltpu.CompilerParams(
            dimension_semantics=("arbitrary", "arbitrary"), vmem_limit_bytes=VMEM_LIMIT),
        name="even_out_proj",
    )(x2, att2, g2, ssm2, w_out, gains)


def _odd_kernel(x_ref, pre_ref, w_in_ref, pw_ref, ps_ref, w_out_ref, post_ref, o_ref,
                ext, tmp0, tmp1):
    tm = ODD_TILE
    hist = POOL_HIST
    total = hist + tm
    j = pl.program_id(1)

    @pl.when(j == 0)
    def _():
        ext[0:hist, :] = jnp.zeros((hist, POOL_WIDTH), F32)

    tmps = (tmp0, tmp1)
    sub = ODD_SUB
    for s in range(tm // sub):
        lo = hist + s * sub
        hi = lo + sub
        x = x_ref[s * sub:(s + 1) * sub, :]
        hb = _rms(x, pre_ref[...]).astype(BF16)
        ext[lo:hi, :] = jnp.dot(hb, w_in_ref[:, :POOL_WIDTH], preferred_element_type=F32)
        t_idx = j * tm + s * sub + lax.broadcasted_iota(jnp.int32, (sub, 1), 0)
        inv_count = 1.0 / (t_idx + 1).astype(F32)
        y = None
        for g, w in enumerate(POOL_WINDOWS):
            cols = slice(g * POOL_GROUP, (g + 1) * POOL_GROUP)
            levels = int(math.log2(w))
            src_ref, src_cols = ext, cols
            for lev in range(levels):
                shift = 2 ** lev
                start = lo - 8 * (levels - 1 - lev)
                val = (src_ref[start:hi, src_cols] + src_ref[start - shift:hi - shift, src_cols])
                if lev < levels - 1:
                    dst = tmps[lev % 2]
                    dst[start:hi, :] = val
                    src_ref, src_cols = dst, slice(None)
            u_g = ext[lo:hi, cols]
            mixed = val * jnp.maximum(inv_count, 1.0 / w) - u_g
            yg = jnp.dot(mixed.astype(BF16), pw_ref[g], preferred_element_type=F32)
            gate = jnp.dot(hb, w_in_ref[:, POOL_WIDTH + g * POOL_GROUP:
                                        POOL_WIDTH + (g + 1) * POOL_GROUP],
                           preferred_element_type=F32)
            yg = yg * ps_ref[:, cols] * _silu(gate)
            part = jnp.dot(yg.astype(BF16), w_out_ref[cols, :], preferred_element_type=F32)
            y = part if y is None else y + part
        o_ref[s * sub:(s + 1) * sub, :] = x + _rms(y, post_ref[...])
    ext[0:hist, :] = ext[tm:total, :]


def _odd_layer(x2, pre, w_in, pool_w, pool_scale, w_out, post, i, layer):
    tm = ODD_TILE
    nj = SEQ // tm
    row_spec = pl.BlockSpec((tm, D_MODEL), lambda b, j: (b * nj + j, 0))
    return pl.pallas_call(
        _odd_kernel,
        grid=(BATCH, nj),
        in_specs=[row_spec, _layer_spec(pre, layer), _layer_spec(w_in, i),
                  _layer_spec(pool_w, i), _layer_spec(pool_scale, i),
                  _layer_spec(w_out, i), _layer_spec(post, layer)],
        out_specs=row_spec,
        out_shape=jax.ShapeDtypeStruct((BATCH * SEQ, D_MODEL), F32),
        scratch_shapes=[pltpu.VMEM((POOL_HIST + tm, POOL_WIDTH), F32),
                        pltpu.VMEM((POOL_HIST + tm, POOL_GROUP), F32),
                        pltpu.VMEM((POOL_HIST + tm, POOL_GROUP), F32)],
        compiler_params=pltpu.CompilerParams(
            dimension_semantics=("arbitrary", "arbitrary"), vmem_limit_bytes=VMEM_LIMIT),
        name="odd_layer",
    )(x2, pre, w_in, pool_w, pool_scale, w_out, post)


def _rope_tables():
    half = ROT_DIM // 2
    inv_freq = (np.float32(ROPE_THETA)
                ** (-np.arange(0, ROT_DIM, 2, dtype=np.float32) / np.float32(ROT_DIM)))
    ang = np.arange(SEQ, dtype=np.float32)[:, None] * inv_freq[None, :].astype(np.float32)
    cos, sin = np.cos(ang).astype(np.float32), np.sin(ang).astype(np.float32)
    zeros = np.zeros((SEQ, HEAD_DIM - ROT_DIM), np.float32)
    zeros_h = np.zeros((SEQ, half), np.float32)
    cos_h = np.concatenate([cos, cos, np.ones_like(zeros)], axis=1)
    sa_h = np.concatenate([-sin, zeros_h, zeros], axis=1)
    sb_h = np.concatenate([zeros_h, sin, zeros], axis=1)
    rep = LANES // HEAD_DIM
    pos = np.arange(SEQ).reshape(SEQ // ROW_TILE, ROW_TILE // N_CLS, N_CLS)
    order = pos.transpose(0, 2, 1).reshape(SEQ)
    return tuple(jnp.asarray(np.tile(t, (1, rep))[order]) for t in (cos_h, sa_h, sb_h))


def _ssm_params(a_re, a_im, log_dt, b_re, b_im, c_re, c_im):
    n_layer = a_re.shape[0]
    lam = lax.complex(a_re.astype(F32), a_im.astype(F32))
    dt = jnp.exp(log_dt.astype(F32))[..., None]
    lam_bar = jnp.exp(lam * dt)
    b_bar = ((lam_bar - 1.0) / lam)[..., None] * lax.complex(b_re.astype(F32), b_im.astype(F32))
    lam_re = jnp.broadcast_to(jnp.real(lam_bar).reshape(n_layer, 1, N_STATE),
                              (n_layer, BATCH, N_STATE))
    lam_im = jnp.broadcast_to(jnp.imag(lam_bar).reshape(n_layer, 1, N_STATE),
                              (n_layer, BATCH, N_STATE))
    gpc = LANES // SSM_GROUP
    n_chunk = SSM_GROUPS // gpc
    eye = jnp.eye(gpc, dtype=F32)

    def block_diag_in(m):
        m = m.reshape(n_layer, n_chunk, gpc, SSM_STATE, SSM_GROUP)
        return jnp.einsum('lcgph,gk->lcghkp', m, eye).reshape(
            n_layer, n_chunk, gpc * SSM_GROUP, gpc * SSM_STATE)

    def block_diag_out(m):
        m = m.reshape(n_layer, n_chunk, gpc, SSM_GROUP, SSM_STATE)
        return jnp.einsum('lcghp,gk->lcgpkh', m, eye).reshape(
            n_layer, n_chunk, gpc * SSM_STATE, gpc * SSM_GROUP)

    wb = jnp.concatenate([block_diag_in(jnp.real(b_bar)), block_diag_in(jnp.imag(b_bar))],
                         axis=3).astype(BF16)
    wc_re = block_diag_out(c_re.astype(F32)).astype(BF16)
    wc_im = block_diag_out(-c_im.astype(F32)).astype(BF16)
    return lam_re, lam_im, wb, wc_re, wc_im


def kernel(x, pre_norm, post_norm, even_w_in, even_w_out, ssm_a_re, ssm_a_im, ssm_log_dt,
           ssm_b_re, ssm_b_im, ssm_c_re, ssm_c_im, ssm_d, ssm_glu_w, ssm_glu_b,
           odd_w_in, pool_w, pool_scale, odd_w_out):
    cos_t, sa_t, sb_t = _rope_tables()
    pre = pre_norm.astype(F32).reshape(DEPTH, 1, D_MODEL)
    post = post_norm.astype(F32).reshape(DEPTH, 1, D_MODEL)
    even_w_in_b = even_w_in.astype(BF16)
    even_w_out_b = even_w_out.astype(BF16)
    odd_w_in_b = odd_w_in.astype(BF16)
    odd_w_out_b = odd_w_out.astype(BF16)
    pool_w_b = pool_w.astype(BF16)
    pool_scale3 = pool_scale.astype(F32).reshape(-1, 1, POOL_WIDTH)
    ssm_params = _ssm_params(ssm_a_re, ssm_a_im, ssm_log_dt, ssm_b_re, ssm_b_im,
                             ssm_c_re, ssm_c_im) + (
        ssm_d.astype(F32).reshape(-1, 1, SSM_WIDTH), ssm_glu_w.astype(BF16),
        ssm_glu_b.astype(F32).reshape(-1, 1, SSM_WIDTH))
    x2 = x.reshape(BATCH * SEQ, D_MODEL)
    for layer in range(DEPTH):
        i = layer // 2
        if layer % 2 == 0:
            q16, k16, v16, g, us = _even_in(x2, pre, layer, even_w_in_b, i, cos_t, sa_t, sb_t)
            att = _attention(q16, k16, v16)
            ssm = _ssm(us.reshape(BATCH, SEQ, 2 * SSM_WIDTH), ssm_params, i)
            x2 = _even_out(x2, att.reshape(BATCH * SEQ, ATT_WIDTH), g,
                           ssm.reshape(BATCH * SEQ, SSM_WIDTH), even_w_out_b, i, post, layer)
        else:
            x2 = _odd_layer(x2, pre, odd_w_in_b, pool_w_b, pool_scale3, odd_w_out_b, post,
                            i, layer)
    return x2.reshape(BATCH, SEQ, D_MODEL)
```

```python
import math

import jax
import jax.numpy as jnp
import numpy as np
from jax import lax
from jax.experimental import pallas as pl
from jax.experimental.pallas import tpu as pltpu

F32 = jnp.float32
BF16 = jnp.bfloat16

D_MODEL = 1024
BATCH = 8
SEQ = 2048
DEPTH = 4
HEAD_DIM = 64
ATT_WIDTH = 1024
ROT_DIM = 16
ROPE_THETA = 500000.0
SSM_WIDTH = 512
SSM_GROUP = 16
SSM_GROUPS = 32
SSM_STATE = 64
N_STATE = SSM_GROUPS * SSM_STATE
POOL_WIDTH = 2048
POOL_WINDOWS = (2, 4, 8, 16)
POOL_GROUP = 512
EVEN_IN = 4 * ATT_WIDTH + 2 * SSM_WIDTH
RMS_EPS = 1e-6
LOG2_E = math.log2(math.e)

LANES = 128
ATT_CHUNK = 128
ATT_GROUP = 4
NEG_BIG = -1e30
VMEM_LIMIT = 56 * 1024 * 1024

ROW_TILE = 512
SSM_TC = 64
ODD_TILE = 512
ODD_SUB = 512
POOL_HIST = 32


def _sigmoid(x):
    return 0.5 + 0.5 * jnp.tanh(0.5 * x)


def _silu(x):
    h = 0.5 * x
    return h + h * jnp.tanh(h)


def _rms(x, gain):
    return x * lax.rsqrt(jnp.mean(x * x, axis=-1, keepdims=True) + RMS_EPS) * gain


def _const_spec(shape):
    nd = len(shape)
    return pl.BlockSpec(shape, lambda *_: (0,) * nd, pipeline_mode=pl.Buffered(1))


def _layer_spec(stacked, layer):
    shape = stacked.shape[1:]
    nd = len(shape)
    return pl.BlockSpec((None,) + tuple(shape), lambda *_: (layer,) + (0,) * nd,
                        pipeline_mode=pl.Buffered(1))


def _even_in_kernel(x_ref, gain_ref, w_ref, cos_ref, sa_ref, sb_ref,
                    q16_ref, k16_ref, v16_ref, g_ref, us_ref, h_nat, h_cls):
    tm = ROW_TILE
    per_cls = tm // N_CLS
    h = _rms(x_ref[...], gain_ref[...])
    hb = h.astype(BF16)

    def proj(lhs, lo, width):
        return jnp.dot(lhs, w_ref[:, lo:lo + width], preferred_element_type=F32)

    g_ref[...] = proj(hb, 3 * ATT_WIDTH, ATT_WIDTH).astype(g_ref.dtype)
    us_ref[...] = proj(hb, 4 * ATT_WIDTH, 2 * SSM_WIDTH).astype(us_ref.dtype)
    for c in range(D_MODEL // LANES):
        h_nat[c] = h[:, c * LANES:(c + 1) * LANES]
    for c in range(D_MODEL // LANES):
        for cls in range(N_CLS):
            h_cls[cls * per_cls:(cls + 1) * per_cls, c * LANES:(c + 1) * LANES] = (
                h_nat[c, pl.ds(cls, per_cls, stride=N_CLS), :].astype(BF16))
    hb_cls = h_cls[...]
    cos = cos_ref[...]
    sa = sa_ref[...]
    sb = sb_ref[...]

    def rope_store(dst, y, scale):
        for c in range(ATT_WIDTH // LANES):
            yc = y[:, c * LANES:(c + 1) * LANES]
            r = (yc * cos + pltpu.roll(yc, LANES - ROT_DIM // 2, 1) * sa
                 + pltpu.roll(yc, ROT_DIM // 2, 1) * sb)
            dst[0, :, :, c * LANES:(c + 1) * LANES] = (r * scale).reshape(N_CLS, per_cls, LANES)

    rope_store(q16_ref, proj(hb_cls, 0, ATT_WIDTH), HEAD_DIM ** -0.5 * LOG2_E)
    rope_store(k16_ref, proj(hb_cls, ATT_WIDTH, ATT_WIDTH), 1.0)
    v16_ref[0] = proj(hb_cls, 2 * ATT_WIDTH, ATT_WIDTH).reshape(N_CLS, per_cls, ATT_WIDTH)


def _even_in(x2, gains, layer, w_in, i, cos_t, sa_t, sb_t):
    tm = ROW_TILE
    nj = SEQ // tm
    row_spec = pl.BlockSpec((tm, D_MODEL), lambda b, j: (b * nj + j, 0))
    tab_spec = pl.BlockSpec((tm, LANES), lambda b, j: (j, 0))
    att_spec = pl.BlockSpec((tm, ATT_WIDTH), lambda b, j: (b * nj + j, 0))
    cls_spec = pl.BlockSpec((1, N_CLS, tm // N_CLS, ATT_WIDTH), lambda b, j: (b, 0, j, 0))
    att_shape = jax.ShapeDtypeStruct((BATCH * SEQ, ATT_WIDTH), BF16)
    cls_shape = jax.ShapeDtypeStruct((BATCH, N_CLS, CLS_LEN, ATT_WIDTH), F32)
    return pl.pallas_call(
        _even_in_kernel,
        grid=(BATCH, nj),
        in_specs=[row_spec, _layer_spec(gains, layer), _layer_spec(w_in, i),
                  tab_spec, tab_spec, tab_spec],
        out_specs=[cls_spec, cls_spec, cls_spec, att_spec, att_spec],
        out_shape=[cls_shape, cls_shape, cls_shape, att_shape, att_shape],
        scratch_shapes=[pltpu.VMEM((D_MODEL // LANES, tm, LANES), F32),
                        pltpu.VMEM((tm, D_MODEL), BF16)],
        compiler_params=pltpu.CompilerParams(
            dimension_semantics=("arbitrary", "arbitrary"), vmem_limit_bytes=VMEM_LIMIT),
        name="even_in_proj",
    )(x2, gains, w_in, cos_t, sa_t, sb_t)


N_CLS = 16
CLS_LEN = SEQ // N_CLS


def _attn_masks():
    ch = ATT_CHUNK
    rho = np.arange(ch)
    tau = 16 * (rho % 8) + rho // 8
    kap = np.arange(2 * ch)
    kk = 16 * (kap % 16) + kap // 16
    p1_full = np.where(kk[None, :] < ch, kk[None, :] >= tau[:, None],
                       (kk[None, :] - ch) <= tau[:, None])
    kap1 = np.arange(ch)
    p1_first = (16 * (kap1 % 8) + kap1 // 8)[None, :] <= tau[:, None]
    a = 4 * (rho % 32) + rho // 32
    kap = np.arange(2 * ch)
    kk2 = 4 * (kap % 64) + kap // 64
    p2_full = (kk2[None, :] >= a[:, None]) & (kk2[None, :] <= a[:, None] + ch)
    kap1 = np.arange(ch)
    kk1 = 4 * (kap1 % 32) + kap1 // 32
    p2_first = kk1[None, :] <= a[:, None]
    p3 = np.arange(ch)[None, :] <= rho[:, None]
    valid = np.concatenate([p1_full, p1_first, p2_full, p2_first, p3], axis=1)
    return jnp.asarray(np.where(valid, 0.0, NEG_BIG), dtype=F32)


def _attn_kernel(q16_ref, k16_ref, v16_ref, mask_ref, o_ref, m_a, m_b, acc_a, acc_b, fin):
    ch = ATT_CHUNK
    lane = lax.broadcasted_iota(jnp.int32, (1, LANES), 1)
    is_a = lane < HEAD_DIM
    is_b = jnp.logical_not(is_a)
    state = ((m_a, acc_a), (m_b, acc_b))
    ones_row = jnp.ones((1, LANES), BF16)
    zeros_row = jnp.zeros((1, LANES), BF16)
    mask_p1_full = slice(0, 2 * ch)
    mask_p1_first = slice(2 * ch, 3 * ch)
    mask_p2_full = slice(3 * ch, 5 * ch)
    mask_p2_first = slice(5 * ch, 6 * ch)
    mask_p3 = slice(6 * ch, 7 * ch)

    def gather(ref, slabs):
        parts = [ref[s:s + n, :] for s, n in slabs]
        return parts[0] if len(parts) == 1 else jnp.concatenate(parts, axis=0)

    def gather_cls(ref, slabs):
        parts = [ref[0, s // CLS_LEN, s % CLS_LEN:s % CLS_LEN + n, :] for s, n in slabs]
        return parts[0] if len(parts) == 1 else jnp.concatenate(parts, axis=0)

    def scatter(ref, slabs, val):
        off = 0
        for s, n in slabs:
            ref[s:s + n, :] = val[off:off + n, :]
            off += n

    def phase_scores(items):
        scores = []
        for load_q, load_k, load_v, mask, slabs in items:
            qc = load_q()
            kc = load_k()
            for is_h in (is_a, is_b):
                qh = jnp.where(is_h, qc, zeros_row)
                s = lax.dot_general(qh, kc, (((1,), (1,)), ((), ())),
                                    preferred_element_type=F32) + mask_ref[:, mask]
                scores.append((s, jnp.max(s, axis=1, keepdims=True)))
        return scores

    def phase_update(items, scores, has_state, finalize):
        partial = []
        n = 0
        for load_q, load_k, load_v, mask, slabs in items:
            vc = load_v()
            for h, is_h in enumerate((is_a, is_b)):
                s, m_c = scores[n]
                n += 1
                if has_state:
                    m_prev = gather(state[h][0], slabs)
                    m_new = jnp.maximum(m_prev, m_c)
                else:
                    m_prev = None
                    m_new = jnp.broadcast_to(m_c, (ch, LANES))
                m_k = jnp.concatenate([m_new] * (s.shape[1] // LANES), axis=1)
                p = jnp.exp2(s - m_k)
                vh = jnp.where(is_h, vc, ones_row)
                pv = jnp.dot(p.astype(BF16), vh, preferred_element_type=F32)
                partial.append((m_prev, m_new, pv))
        n = 0
        for load_q, load_k, load_v, mask, slabs in items:
            res = []
            for h in range(2):
                m_prev, m_new, pv = partial[n]
                n += 1
                if has_state:
                    a_new = jnp.exp2(m_prev - m_new) * gather(state[h][1], slabs) + pv
                else:
                    a_new = pv
                res.append(a_new)
                if not finalize:
                    scatter(state[h][0], slabs, m_new)
                    scatter(state[h][1], slabs, a_new)
            if finalize:
                num = jnp.where(is_a, res[0], res[1])
                den = jnp.where(is_a, pltpu.roll(res[0], HEAD_DIM, 1),
                                pltpu.roll(res[1], HEAD_DIM, 1))
                scatter(fin, slabs, num / den)

    groups = []

    def add_groups(items, has_state, finalize):
        for i in range(0, len(items), ATT_GROUP):
            groups.append((items[i:i + ATT_GROUP], has_state, finalize))

    def block(slabs, kslabs, mask):
        return (lambda: gather_cls(q16_ref, slabs).astype(BF16),
                lambda: gather_cls(k16_ref, kslabs).astype(BF16),
                lambda: gather_cls(v16_ref, kslabs).astype(BF16), mask, slabs)

    per = ch // N_CLS
    items = []
    for i in range(SEQ // ch):
        slabs = [(r * CLS_LEN + per * i, per) for r in range(N_CLS)]
        if i > 0:
            kslabs = [(r * CLS_LEN + per * (i - 1), 2 * per) for r in range(N_CLS)]
            items.append(block(slabs, kslabs, mask_p1_full))
        else:
            items.append(block(slabs, slabs, mask_p1_first))
    add_groups(items, False, False)

    quarter = ch // 4
    items = []
    for r4 in range(4):
        for ic in range(SEQ // (4 * ch)):
            cls = [(4 * j + r4) * CLS_LEN for j in range(4)]
            slabs = [(c + quarter * ic, quarter) for c in cls]
            if ic > 0:
                kslabs = [(c + quarter * (ic - 1), 2 * quarter) for c in cls]
                items.append(block(slabs, kslabs, mask_p2_full))
            else:
                items.append(block(slabs, slabs, mask_p2_first))
    add_groups(items, True, False)

    items = []
    for r in range(N_CLS):
        slabs = [(r * CLS_LEN, CLS_LEN)]
        items.append(block(slabs, slabs, mask_p3))
    add_groups(items, True, True)

    pending = None
    for items, has_state, finalize in groups:
        scores = phase_scores(items)
        if pending is not None:
            phase_update(*pending)
        pending = (items, scores, has_state, finalize)
    phase_update(*pending)

    row = lax.broadcasted_iota(jnp.int32, (ch, ch), 0)
    col = lax.broadcasted_iota(jnp.int32, (ch, ch), 1)
    shuffle = jnp.where(col == per * (row % N_CLS) + row // N_CLS, 1.0, 0.0).astype(BF16)
    for i in range(SEQ // ch):
        slab = gather(fin, [(r * CLS_LEN + per * i, per) for r in range(N_CLS)])
        o_ref[0, i * ch:(i + 1) * ch, :] = jnp.dot(
            shuffle, slab.astype(BF16), preferred_element_type=F32).astype(o_ref.dtype)


def _attention(q16, k16, v16):
    blk = pl.BlockSpec((1, SEQ, LANES), lambda b, h: (b, 0, h))
    cls_blk = pl.BlockSpec((1, N_CLS, CLS_LEN, LANES), lambda b, h: (b, 0, 0, h))
    f32_scr = pltpu.VMEM((SEQ, LANES), F32)
    masks = _attn_masks()
    return pl.pallas_call(
        _attn_kernel,
        grid=(BATCH, ATT_WIDTH // LANES),
        in_specs=[cls_blk, cls_blk, cls_blk, _const_spec(masks.shape)],
        out_specs=blk,
        out_shape=jax.ShapeDtypeStruct((BATCH, SEQ, ATT_WIDTH), BF16),
        scratch_shapes=[f32_scr] * 5,
        compiler_params=pltpu.CompilerParams(
            dimension_semantics=("arbitrary", "arbitrary"), vmem_limit_bytes=VMEM_LIMIT),
        name="dilated_attention",
    )(q16, k16, v16, masks)


def _ssm_kernel(us_ref, perm_ref, lam_re_ref, lam_im_ref, wb_ref, wc_re_ref, wc_im_ref,
                d_ref, gw_ref, gb_ref, o_ref, xr, xi, st_re, st_im, y_tb, y_bt):
    tc = SSM_TC
    rows = tc * BATCH
    n_chunk = SSM_WIDTH // LANES
    cpc = N_STATE // n_chunk

    @pl.when(pl.program_id(0) == 0)
    def _():
        st_re[...] = jnp.zeros_like(st_re)
        st_im[...] = jnp.zeros_like(st_im)

    ub = us_ref[:, :, :SSM_WIDTH].reshape(rows, SSM_WIDTH)
    u_tb = jnp.dot(perm_ref[...], ub, preferred_element_type=F32).astype(BF16)
    for c in range(n_chunk):
        cols = slice(c * cpc, (c + 1) * cpc)
        bu = jnp.dot(u_tb[:, c * LANES:(c + 1) * LANES], wb_ref[c], preferred_element_type=F32)
        xr[:, cols] = bu[:, :cpc]
        xi[:, cols] = bu[:, cpc:]
        lr = lam_re_ref[:, cols]
        li = lam_im_ref[:, cols]
        sr = st_re[:, cols]
        si = st_im[:, cols]
        for t in range(tc):
            trow = slice(t * BATCH, (t + 1) * BATCH)
            sr, si = (lr * sr - li * si + xr[trow, cols], lr * si + li * sr + xi[trow, cols])
            xr[trow, cols] = sr
            xi[trow, cols] = si
        st_re[:, cols] = sr
        st_im[:, cols] = si
        y_tb[c] = (
            jnp.dot(xr[:, cols].astype(BF16), wc_re_ref[c], preferred_element_type=F32)
            + jnp.dot(xi[:, cols].astype(BF16), wc_im_ref[c], preferred_element_type=F32))
        lanes = slice(c * LANES, (c + 1) * LANES)
        y_c = jnp.concatenate(
            [y_tb[c, pl.ds(b, tc, stride=BATCH), :] for b in range(BATCH)], axis=0)
        y_c = jax.nn.gelu(y_c + d_ref[:, lanes] * ub[:, lanes].astype(F32))
        y_bt[:, lanes] = y_c
    y = y_bt[...]
    z = jnp.dot(y.astype(BF16), gw_ref[...], preferred_element_type=F32) + gb_ref[...]
    gs = us_ref[:, :, SSM_WIDTH:].reshape(rows, SSM_WIDTH).astype(F32)
    out = y * _sigmoid(z) * _silu(gs)
    o_ref[...] = out.reshape(BATCH, tc, SSM_WIDTH).astype(o_ref.dtype)


def _ssm(us3, params, i):
    tc = SSM_TC
    rows = tc * BATCH
    dst_row = np.arange(rows)
    perm = jnp.asarray((dst_row[:, None] % BATCH) * tc + dst_row[:, None] // BATCH
                       == dst_row[None, :], dtype=BF16)
    return pl.pallas_call(
        _ssm_kernel,
        grid=(SEQ // tc,),
        in_specs=[pl.BlockSpec((BATCH, tc, 2 * SSM_WIDTH), lambda t: (0, t, 0)),
                  _const_spec((rows, rows))] + [_layer_spec(p, i) for p in params],
        out_specs=pl.BlockSpec((BATCH, tc, SSM_WIDTH), lambda t: (0, t, 0)),
        out_shape=jax.ShapeDtypeStruct((BATCH, SEQ, SSM_WIDTH), BF16),
        scratch_shapes=[pltpu.VMEM((rows, N_STATE), F32), pltpu.VMEM((rows, N_STATE), F32),
                        pltpu.VMEM((BATCH, N_STATE), F32), pltpu.VMEM((BATCH, N_STATE), F32),
                        pltpu.VMEM((SSM_WIDTH // LANES, rows, LANES), F32),
                        pltpu.VMEM((rows, SSM_WIDTH), F32)],
        compiler_params=pltpu.CompilerParams(
            dimension_semantics=("arbitrary",), vmem_limit_bytes=VMEM_LIMIT),
        name="s5_branch",
    )(us3, perm, *params)


def _tail_kernel(x_ref, att_ref, g_ref, ssm_ref, w_eo_ref, post_e_ref,
                 pre_ref, w_in_ref, pw_ref, ps_ref, w_out_ref, post_ref, o_ref,
                 ext, tmp0, tmp1):
    tm = ODD_TILE
    hist = POOL_HIST
    total = hist + tm
    j = pl.program_id(1)

    @pl.when(j == 0)
    def _():
        ext[0:hist, :] = jnp.zeros((hist, POOL_WIDTH), F32)

    tmps = (tmp0, tmp1)
    sub = ODD_SUB
    for s in range(tm // sub):
        lo = hist + s * sub
        hi = lo + sub
        rows = slice(s * sub, (s + 1) * sub)
        gated = (att_ref[rows, :].astype(F32) * _silu(g_ref[rows, :].astype(F32))).astype(BF16)
        y_even = jnp.dot(gated, w_eo_ref[:ATT_WIDTH, :], preferred_element_type=F32)
        y_even = y_even + jnp.dot(ssm_ref[rows, :], w_eo_ref[ATT_WIDTH:, :],
                                  preferred_element_type=F32)
        x = x_ref[rows, :] + _rms(y_even, post_e_ref[...])
        hb = _rms(x, pre_ref[...]).astype(BF16)
        ext[lo:hi, :] = jnp.dot(hb, w_in_ref[:, :POOL_WIDTH], preferred_element_type=F32)
        t_idx = j * tm + s * sub + lax.broadcasted_iota(jnp.int32, (sub, 1), 0)
        inv_count = 1.0 / (t_idx + 1).astype(F32)
        y = None
        for g, w in enumerate(POOL_WINDOWS):
            cols = slice(g * POOL_GROUP, (g + 1) * POOL_GROUP)
            levels = int(math.log2(w))
            src_ref, src_cols = ext, cols
            for lev in range(levels):
                shift = 2 ** lev
                start = lo - 8 * (levels - 1 - lev)
                val = (src_ref[start:hi, src_cols] + src_ref[start - shift:hi - shift, src_cols])
                if lev < levels - 1:
                    dst = tmps[lev % 2]
                    dst[start:hi, :] = val
                    src_ref, src_cols = dst, slice(None)
            u_g = ext[lo:hi, cols]
            mixed = val * jnp.maximum(inv_count, 1.0 / w) - u_g
            yg = jnp.dot(mixed.astype(BF16), pw_ref[g], preferred_element_type=F32)
            gate = jnp.dot(hb, w_in_ref[:, POOL_WIDTH + g * POOL_GROUP:
                                        POOL_WIDTH + (g + 1) * POOL_GROUP],
                           preferred_element_type=F32)
            yg = yg * ps_ref[:, cols] * _silu(gate)
            part = jnp.dot(yg.astype(BF16), w_out_ref[cols, :], preferred_element_type=F32)
            y = part if y is None else y + part
        o_ref[rows, :] = x + _rms(y, post_ref[...])
    ext[0:hist, :] = ext[tm:total, :]


def _even_tail_odd(x2, att2, g2, ssm2, w_even_out, pre, post, w_in, pool_w, pool_scale, w_out, i,
                   layer):
    tm = ODD_TILE
    nj = SEQ // tm
    row_spec = pl.BlockSpec((tm, D_MODEL), lambda b, j: (b * nj + j, 0))
    att_spec = pl.BlockSpec((tm, ATT_WIDTH), lambda b, j: (b * nj + j, 0))
    return pl.pallas_call(
        _tail_kernel,
        grid=(BATCH, nj),
        in_specs=[row_spec, att_spec, att_spec,
                  pl.BlockSpec((tm, SSM_WIDTH), lambda b, j: (b * nj + j, 0)),
                  _layer_spec(w_even_out, i), _layer_spec(post, layer),
                  _layer_spec(pre, layer + 1), _layer_spec(w_in, i),
                  _layer_spec(pool_w, i), _layer_spec(pool_scale, i),
                  _layer_spec(w_out, i), _layer_spec(post, layer + 1)],
        out_specs=row_spec,
        out_shape=jax.ShapeDtypeStruct((BATCH * SEQ, D_MODEL), F32),
        scratch_shapes=[pltpu.VMEM((POOL_HIST + tm, POOL_WIDTH), F32),
                        pltpu.VMEM((POOL_HIST + tm, POOL_GROUP), F32),
                        pltpu.VMEM((POOL_HIST + tm, POOL_GROUP), F32)],
        compiler_params=pltpu.CompilerParams(
            dimension_semantics=("arbitrary", "arbitrary"), vmem_limit_bytes=VMEM_LIMIT),
        name="even_tail_odd_layer",
    )(x2, att2, g2, ssm2, w_even_out, post, pre, w_in, pool_w, pool_scale, w_out, post)


def _rope_tables():
    half = ROT_DIM // 2
    inv_freq = (np.float32(ROPE_THETA)
                ** (-np.arange(0, ROT_DIM, 2, dtype=np.float32) / np.float32(ROT_DIM)))
    ang = np.arange(SEQ, dtype=np.float32)[:, None] * inv_freq[None, :].astype(np.float32)
    cos, sin = np.cos(ang).astype(np.float32), np.sin(ang).astype(np.float32)
    zeros = np.zeros((SEQ, HEAD_DIM - ROT_DIM), np.float32)
    zeros_h = np.zeros((SEQ, half), np.float32)
    cos_h = np.concatenate([cos, cos, np.ones_like(zeros)], axis=1)
    sa_h = np.concatenate([-sin, zeros_h, zeros], axis=1)
    sb_h = np.concatenate([zeros_h, sin, zeros], axis=1)
    rep = LANES // HEAD_DIM
    pos = np.arange(SEQ).reshape(SEQ // ROW_TILE, ROW_TILE // N_CLS, N_CLS)
    order = pos.transpose(0, 2, 1).reshape(SEQ)
    return tuple(jnp.asarray(np.tile(t, (1, rep))[order]) for t in (cos_h, sa_h, sb_h))


def _ssm_params(a_re, a_im, log_dt, b_re, b_im, c_re, c_im):
    n_layer = a_re.shape[0]
    lam = lax.complex(a_re.astype(F32), a_im.astype(F32))
    dt = jnp.exp(log_dt.astype(F32))[..., None]
    lam_bar = jnp.exp(lam * dt)
    b_bar = ((lam_bar - 1.0) / lam)[..., None] * lax.complex(b_re.astype(F32), b_im.astype(F32))
    lam_re = jnp.broadcast_to(jnp.real(lam_bar).reshape(n_layer, 1, N_STATE),
                              (n_layer, BATCH, N_STATE))
    lam_im = jnp.broadcast_to(jnp.imag(lam_bar).reshape(n_layer, 1, N_STATE),
                              (n_layer, BATCH, N_STATE))
    gpc = LANES // SSM_GROUP
    n_chunk = SSM_GROUPS // gpc
    eye = jnp.eye(gpc, dtype=F32)

    def block_diag_in(m):
        m = m.reshape(n_layer, n_chunk, gpc, SSM_STATE, SSM_GROUP)
        return jnp.einsum('lcgph,gk->lcghkp', m, eye).reshape(
            n_layer, n_chunk, gpc * SSM_GROUP, gpc * SSM_STATE)

    def block_diag_out(m):
        m = m.reshape(n_layer, n_chunk, gpc, SSM_GROUP, SSM_STATE)
        return jnp.einsum('lcghp,gk->lcgpkh', m, eye).reshape(
            n_layer, n_chunk, gpc * SSM_STATE, gpc * SSM_GROUP)

    wb = jnp.concatenate([block_diag_in(jnp.real(b_bar)), block_diag_in(jnp.imag(b_bar))],
                         axis=3).astype(BF16)
    wc_re = block_diag_out(c_re.astype(F32)).astype(BF16)
    wc_im = block_diag_out(-c_im.astype(F32)).astype(BF16)
    return lam_re, lam_im, wb, wc_re, wc_im


def kernel(x, pre_norm, post_norm, even_w_in, even_w_out, ssm_a_re, ssm_a_im, ssm_log_dt,
           ssm_b_re, ssm_b_im, ssm_c_re, ssm_c_im, ssm_d, ssm_glu_w, ssm_glu_b,
           odd_w_in, pool_w, pool_scale, odd_w_out):
    cos_t, sa_t, sb_t = _rope_tables()
    pre = pre_norm.astype(F32).reshape(DEPTH, 1, D_MODEL)
    post = post_norm.astype(F32).reshape(DEPTH, 1, D_MODEL)
    even_w_in_b = even_w_in.astype(BF16)
    even_w_out_b = even_w_out.astype(BF16)
    odd_w_in_b = odd_w_in.astype(BF16)
    odd_w_out_b = odd_w_out.astype(BF16)
    pool_w_b = pool_w.astype(BF16)
    pool_scale3 = pool_scale.astype(F32).reshape(-1, 1, POOL_WIDTH)
    ssm_params = _ssm_params(ssm_a_re, ssm_a_im, ssm_log_dt, ssm_b_re, ssm_b_im,
                             ssm_c_re, ssm_c_im) + (
        ssm_d.astype(F32).reshape(-1, 1, SSM_WIDTH), ssm_glu_w.astype(BF16),
        ssm_glu_b.astype(F32).reshape(-1, 1, SSM_WIDTH))
    x2 = x.reshape(BATCH * SEQ, D_MODEL)
    assert DEPTH % 2 == 0
    for layer in range(0, DEPTH, 2):
        i = layer // 2
        q16, k16, v16, g, us = _even_in(x2, pre, layer, even_w_in_b, i, cos_t, sa_t, sb_t)
        att = _attention(q16, k16, v16)
        ssm = _ssm(us.reshape(BATCH, SEQ, 2 * SSM_WIDTH), ssm_params, i)
        x2 = _even_tail_odd(x2, att.reshape(BATCH * SEQ, ATT_WIDTH), g,
                            ssm.reshape(BATCH * SEQ, SSM_WIDTH), even_w_out_b, pre, post,
                            odd_w_in_b, pool_w_b, pool_scale3, odd_w_out_b, i, layer)
    return x2.reshape(BATCH, SEQ, D_MODEL)
```

```python
import math

import jax
import jax.numpy as jnp
import numpy as np
from jax import lax
from jax.experimental import pallas as pl
from jax.experimental.pallas import tpu as pltpu

F32 = jnp.float32
BF16 = jnp.bfloat16

D_MODEL = 1024
BATCH = 8
SEQ = 2048
DEPTH = 4
HEAD_DIM = 64
ATT_WIDTH = 1024
ROT_DIM = 16
ROPE_THETA = 500000.0
SSM_WIDTH = 512
SSM_GROUP = 16
SSM_GROUPS = 32
SSM_STATE = 64
N_STATE = SSM_GROUPS * SSM_STATE
POOL_WIDTH = 2048
POOL_WINDOWS = (2, 4, 8, 16)
POOL_GROUP = 512
EVEN_IN = 4 * ATT_WIDTH + 2 * SSM_WIDTH
RMS_EPS = 1e-6
LOG2_E = math.log2(math.e)

LANES = 128
ATT_CHUNK = 128
ATT_GROUP = 4
NEG_BIG = -1e30
VMEM_LIMIT = 56 * 1024 * 1024

ROW_TILE = 512
SSM_TC = 64
ODD_TILE = 512
ODD_SUB = 512
POOL_HIST = 32


def _sigmoid(x):
    return 0.5 + 0.5 * jnp.tanh(0.5 * x)


def _silu(x):
    h = 0.5 * x
    return h + h * jnp.tanh(h)


def _rms(x, gain):
    return x * lax.rsqrt(jnp.mean(x * x, axis=-1, keepdims=True) + RMS_EPS) * gain


def _const_spec(shape):
    nd = len(shape)
    return pl.BlockSpec(shape, lambda *_: (0,) * nd, pipeline_mode=pl.Buffered(1))


def _layer_spec(stacked, layer):
    shape = stacked.shape[1:]
    nd = len(shape)
    return pl.BlockSpec((None,) + tuple(shape), lambda *_: (layer,) + (0,) * nd,
                        pipeline_mode=pl.Buffered(1))


def _even_in_kernel(x_ref, gain_ref, w_ref, cos_ref, sa_ref, sb_ref,
                    q16_ref, k16_ref, v16_ref, g_ref, us_ref, h_nat, h_cls):
    tm = ROW_TILE
    per_cls = tm // N_CLS
    def proj(lhs, lo, width):
        return jnp.dot(lhs, w_ref[:, lo:lo + width], preferred_element_type=F32)

    half = tm // 2
    for hh in range(2):
        r = slice(hh * half, (hh + 1) * half)
        h = _rms(x_ref[r, :], gain_ref[...])
        for c in range(D_MODEL // LANES):
            h_nat[c, r, :] = h[:, c * LANES:(c + 1) * LANES]
        hb = h.astype(BF16)
        g_ref[r, :] = proj(hb, 3 * ATT_WIDTH, ATT_WIDTH).astype(g_ref.dtype)
        us_ref[r, :] = proj(hb, 4 * ATT_WIDTH, 2 * SSM_WIDTH).astype(us_ref.dtype)
    for c in range(D_MODEL // LANES):
        for cls in range(N_CLS):
            h_cls[cls * per_cls:(cls + 1) * per_cls, c * LANES:(c + 1) * LANES] = (
                h_nat[c, pl.ds(cls, per_cls, stride=N_CLS), :].astype(BF16))
    hb_cls = h_cls[...]
    cos = cos_ref[...]
    sa = sa_ref[...]
    sb = sb_ref[...]

    def rope_store(dst, y, scale):
        for c in range(ATT_WIDTH // LANES):
            yc = y[:, c * LANES:(c + 1) * LANES]
            r = (yc * cos + pltpu.roll(yc, LANES - ROT_DIM // 2, 1) * sa
                 + pltpu.roll(yc, ROT_DIM // 2, 1) * sb)
            dst[0, :, :, c * LANES:(c + 1) * LANES] = (r * scale).reshape(N_CLS, per_cls, LANES)

    rope_store(q16_ref, proj(hb_cls, 0, ATT_WIDTH), HEAD_DIM ** -0.5 * LOG2_E)
    rope_store(k16_ref, proj(hb_cls, ATT_WIDTH, ATT_WIDTH), 1.0)
    v16_ref[0] = proj(hb_cls, 2 * ATT_WIDTH, ATT_WIDTH).reshape(N_CLS, per_cls, ATT_WIDTH)


def _even_in(x2, gains, layer, w_in, i, cos_t, sa_t, sb_t):
    tm = ROW_TILE
    nj = SEQ // tm
    row_spec = pl.BlockSpec((tm, D_MODEL), lambda b, j: (b * nj + j, 0))
    tab_spec = pl.BlockSpec((tm, LANES), lambda b, j: (j, 0))
    att_spec = pl.BlockSpec((tm, ATT_WIDTH), lambda b, j: (b * nj + j, 0))
    cls_spec = pl.BlockSpec((1, N_CLS, tm // N_CLS, ATT_WIDTH), lambda b, j: (b, 0, j, 0))
    att_shape = jax.ShapeDtypeStruct((BATCH * SEQ, ATT_WIDTH), BF16)
    cls_shape = jax.ShapeDtypeStruct((BATCH, N_CLS, CLS_LEN, ATT_WIDTH), F32)
    return pl.pallas_call(
        _even_in_kernel,
        grid=(BATCH, nj),
        in_specs=[row_spec, _layer_spec(gains, layer), _layer_spec(w_in, i),
                  tab_spec, tab_spec, tab_spec],
        out_specs=[cls_spec, cls_spec, cls_spec, att_spec, att_spec],
        out_shape=[cls_shape, cls_shape, cls_shape, att_shape, att_shape],
        scratch_shapes=[pltpu.VMEM((D_MODEL // LANES, tm, LANES), F32),
                        pltpu.VMEM((tm, D_MODEL), BF16)],
        compiler_params=pltpu.CompilerParams(
            dimension_semantics=("arbitrary", "arbitrary"), vmem_limit_bytes=VMEM_LIMIT),
        name="even_in_proj",
    )(x2, gains, w_in, cos_t, sa_t, sb_t)


N_CLS = 16
CLS_LEN = SEQ // N_CLS


def _attn_masks():
    ch = ATT_CHUNK
    rho = np.arange(ch)
    tau = 16 * (rho % 8) + rho // 8
    kap = np.arange(2 * ch)
    kk = 16 * (kap % 16) + kap // 16
    p1_full = np.where(kk[None, :] < ch, kk[None, :] >= tau[:, None],
                       (kk[None, :] - ch) <= tau[:, None])
    kap1 = np.arange(ch)
    p1_first = (16 * (kap1 % 8) + kap1 // 8)[None, :] <= tau[:, None]
    a = 4 * (rho % 32) + rho // 32
    kap = np.arange(2 * ch)
    kk2 = 4 * (kap % 64) + kap // 64
    p2_full = (kk2[None, :] >= a[:, None]) & (kk2[None, :] <= a[:, None] + ch)
    kap1 = np.arange(ch)
    kk1 = 4 * (kap1 % 32) + kap1 // 32
    p2_first = kk1[None, :] <= a[:, None]
    p3 = np.arange(ch)[None, :] <= rho[:, None]
    valid = np.concatenate([p1_full, p1_first, p2_full, p2_first, p3], axis=1)
    return jnp.asarray(np.where(valid, 0.0, NEG_BIG), dtype=F32)


def _attn_kernel(q16_ref, k16_ref, v16_ref, mask_ref, o_ref, m_a, m_b, acc_a, acc_b, fin):
    ch = ATT_CHUNK
    lane = lax.broadcasted_iota(jnp.int32, (1, LANES), 1)
    is_a = lane < HEAD_DIM
    is_b = jnp.logical_not(is_a)
    state = ((m_a, acc_a), (m_b, acc_b))
    ones_row = jnp.ones((1, LANES), BF16)
    zeros_row = jnp.zeros((1, LANES), BF16)
    mask_p1_full = slice(0, 2 * ch)
    mask_p1_first = slice(2 * ch, 3 * ch)
    mask_p2_full = slice(3 * ch, 5 * ch)
    mask_p2_first = slice(5 * ch, 6 * ch)
    mask_p3 = slice(6 * ch, 7 * ch)

    def gather(ref, slabs):
        parts = [ref[s:s + n, :] for s, n in slabs]
        return parts[0] if len(parts) == 1 else jnp.concatenate(parts, axis=0)

    def gather_cls(ref, slabs):
        parts = [ref[0, s // CLS_LEN, s % CLS_LEN:s % CLS_LEN + n, :] for s, n in slabs]
        return parts[0] if len(parts) == 1 else jnp.concatenate(parts, axis=0)

    def scatter(ref, slabs, val):
        off = 0
        for s, n in slabs:
            ref[s:s + n, :] = val[off:off + n, :]
            off += n

    def phase_scores(items):
        scores = []
        for load_q, load_k, load_v, mask, slabs in items:
            qc = load_q()
            kc = load_k()
            for is_h in (is_a, is_b):
                qh = jnp.where(is_h, qc, zeros_row)
                s = lax.dot_general(qh, kc, (((1,), (1,)), ((), ())),
                                    preferred_element_type=F32) + mask_ref[:, mask]
                scores.append((s, jnp.max(s, axis=1, keepdims=True)))
        return scores

    def phase_update(items, scores, has_state, finalize):
        partial = []
        n = 0
        for load_q, load_k, load_v, mask, slabs in items:
            vc = load_v()
            for h, is_h in enumerate((is_a, is_b)):
                s, m_c = scores[n]
                n += 1
                if has_state:
                    m_prev = gather(state[h][0], slabs)
                    m_new = jnp.maximum(m_prev, m_c)
                else:
                    m_prev = None
                    m_new = jnp.broadcast_to(m_c, (ch, LANES))
                m_k = jnp.concatenate([m_new] * (s.shape[1] // LANES), axis=1)
                p = jnp.exp2(s - m_k)
                vh = jnp.where(is_h, vc, ones_row)
                pv = jnp.dot(p.astype(BF16), vh, preferred_element_type=F32)
                partial.append((m_prev, m_new, pv))
        n = 0
        for load_q, load_k, load_v, mask, slabs in items:
            res = []
            for h in range(2):
                m_prev, m_new, pv = partial[n]
                n += 1
                if has_state:
                    a_new = jnp.exp2(m_prev - m_new) * gather(state[h][1], slabs) + pv
                else:
                    a_new = pv
                res.append(a_new)
                if not finalize:
                    scatter(state[h][0], slabs, m_new)
                    scatter(state[h][1], slabs, a_new)
            if finalize:
                num = jnp.where(is_a, res[0], res[1])
                den = jnp.where(is_a, pltpu.roll(res[0], HEAD_DIM, 1),
                                pltpu.roll(res[1], HEAD_DIM, 1))
                scatter(fin, slabs, num / den)

    groups = []

    def add_groups(items, has_state, finalize):
        for i in range(0, len(items), ATT_GROUP):
            groups.append((items[i:i + ATT_GROUP], has_state, finalize))

    def block(slabs, kslabs, mask):
        return (lambda: gather_cls(q16_ref, slabs).astype(BF16),
                lambda: gather_cls(k16_ref, kslabs).astype(BF16),
                lambda: gather_cls(v16_ref, kslabs).astype(BF16), mask, slabs)

    per = ch // N_CLS
    items = []
    for i in range(SEQ // ch):
        slabs = [(r * CLS_LEN + per * i, per) for r in range(N_CLS)]
        if i > 0:
            kslabs = [(r * CLS_LEN + per * (i - 1), 2 * per) for r in range(N_CLS)]
            items.append(block(slabs, kslabs, mask_p1_full))
        else:
            items.append(block(slabs, slabs, mask_p1_first))
    add_groups(items, False, False)

    quarter = ch // 4
    items = []
    for r4 in range(4):
        for ic in range(SEQ // (4 * ch)):
            cls = [(4 * j + r4) * CLS_LEN for j in range(4)]
            slabs = [(c + quarter * ic, quarter) for c in cls]
            if ic > 0:
                kslabs = [(c + quarter * (ic - 1), 2 * quarter) for c in cls]
                items.append(block(slabs, kslabs, mask_p2_full))
            else:
                items.append(block(slabs, slabs, mask_p2_first))
    add_groups(items, True, False)

    items = []
    for r in range(N_CLS):
        slabs = [(r * CLS_LEN, CLS_LEN)]
        items.append(block(slabs, slabs, mask_p3))
    add_groups(items, True, True)

    pending = None
    for items, has_state, finalize in groups:
        scores = phase_scores(items)
        if pending is not None:
            phase_update(*pending)
        pending = (items, scores, has_state, finalize)
    phase_update(*pending)

    row = lax.broadcasted_iota(jnp.int32, (ch, ch), 0)
    col = lax.broadcasted_iota(jnp.int32, (ch, ch), 1)
    shuffle = jnp.where(col == per * (row % N_CLS) + row // N_CLS, 1.0, 0.0).astype(BF16)
    for i in range(SEQ // ch):
        slab = gather(fin, [(r * CLS_LEN + per * i, per) for r in range(N_CLS)])
        o_ref[0, i * ch:(i + 1) * ch, :] = jnp.dot(
            shuffle, slab.astype(BF16), preferred_element_type=F32).astype(o_ref.dtype)


def _attention(q16, k16, v16):
    blk = pl.BlockSpec((1, SEQ, LANES), lambda b, h: (b, 0, h))
    cls_blk = pl.BlockSpec((1, N_CLS, CLS_LEN, LANES), lambda b, h: (b, 0, 0, h))
    f32_scr = pltpu.VMEM((SEQ, LANES), F32)
    masks = _attn_masks()
    return pl.pallas_call(
        _attn_kernel,
        grid=(BATCH, ATT_WIDTH // LANES),
        in_specs=[cls_blk, cls_blk, cls_blk, _const_spec(masks.shape)],
        out_specs=blk,
        out_shape=jax.ShapeDtypeStruct((BATCH, SEQ, ATT_WIDTH), BF16),
        scratch_shapes=[f32_scr] * 5,
        compiler_params=pltpu.CompilerParams(
            dimension_semantics=("arbitrary", "arbitrary"), vmem_limit_bytes=VMEM_LIMIT),
        name="dilated_attention",
    )(q16, k16, v16, masks)


def _ssm_kernel(us_ref, perm_ref, lam_re_ref, lam_im_ref, wb_ref, wc_re_ref, wc_im_ref,
                d_ref, gw_ref, gb_ref, o_ref, xr, xi, st_re, st_im, y_tb, y_bt):
    tc = SSM_TC
    rows = tc * BATCH
    n_chunk = SSM_WIDTH // LANES
    cpc = N_STATE // n_chunk

    @pl.when(pl.program_id(0) == 0)
    def _():
        st_re[...] = jnp.zeros_like(st_re)
        st_im[...] = jnp.zeros_like(st_im)

    ub = us_ref[:, :, :SSM_WIDTH].reshape(rows, SSM_WIDTH)
    u_tb = jnp.dot(perm_ref[...], ub, preferred_element_type=F32).astype(BF16)
    def b_proj(c):
        cols = slice(c * cpc, (c + 1) * cpc)
        bu = jnp.dot(u_tb[:, c * LANES:(c + 1) * LANES], wb_ref[c], preferred_element_type=F32)
        xr[:, cols] = bu[:, :cpc]
        xi[:, cols] = bu[:, cpc:]

    def recurrence(c):
        cols = slice(c * cpc, (c + 1) * cpc)
        lr = lam_re_ref[:, cols]
        li = lam_im_ref[:, cols]
        sr = st_re[:, cols]
        si = st_im[:, cols]
        for t in range(tc):
            trow = slice(t * BATCH, (t + 1) * BATCH)
            sr, si = (lr * sr - li * si + xr[trow, cols], lr * si + li * sr + xi[trow, cols])
            xr[trow, cols] = sr
            xi[trow, cols] = si
        st_re[:, cols] = sr
        st_im[:, cols] = si

    def c_proj(c):
        cols = slice(c * cpc, (c + 1) * cpc)
        y_tb[c] = (
            jnp.dot(xr[:, cols].astype(BF16), wc_re_ref[c], preferred_element_type=F32)
            + jnp.dot(xi[:, cols].astype(BF16), wc_im_ref[c], preferred_element_type=F32))

    def skip_gelu(c):
        lanes = slice(c * LANES, (c + 1) * LANES)
        y_c = jnp.concatenate(
            [y_tb[c, pl.ds(b, tc, stride=BATCH), :] for b in range(BATCH)], axis=0)
        y_bt[:, lanes] = jax.nn.gelu(y_c + d_ref[:, lanes] * ub[:, lanes].astype(F32))

    b_proj(0)
    gate = _silu(us_ref[:, :, SSM_WIDTH:].reshape(rows, SSM_WIDTH).astype(F32))
    for c in range(n_chunk):
        if c + 1 < n_chunk:
            b_proj(c + 1)
        recurrence(c)
        if c > 0:
            skip_gelu(c - 1)
        c_proj(c)
    skip_gelu(n_chunk - 1)
    y = y_bt[...]
    z = jnp.dot(y.astype(BF16), gw_ref[...], preferred_element_type=F32) + gb_ref[...]
    out = y * _sigmoid(z) * gate
    o_ref[...] = out.reshape(BATCH, tc, SSM_WIDTH).astype(o_ref.dtype)


def _ssm(us3, params, i):
    tc = SSM_TC
    rows = tc * BATCH
    dst_row = np.arange(rows)
    perm = jnp.asarray((dst_row[:, None] % BATCH) * tc + dst_row[:, None] // BATCH
                       == dst_row[None, :], dtype=BF16)
    return pl.pallas_call(
        _ssm_kernel,
        grid=(SEQ // tc,),
        in_specs=[pl.BlockSpec((BATCH, tc, 2 * SSM_WIDTH), lambda t: (0, t, 0)),
                  _const_spec((rows, rows))] + [_layer_spec(p, i) for p in params],
        out_specs=pl.BlockSpec((BATCH, tc, SSM_WIDTH), lambda t: (0, t, 0)),
        out_shape=jax.ShapeDtypeStruct((BATCH, SEQ, SSM_WIDTH), BF16),
        scratch_shapes=[pltpu.VMEM((rows, N_STATE), F32), pltpu.VMEM((rows, N_STATE), F32),
                        pltpu.VMEM((BATCH, N_STATE), F32), pltpu.VMEM((BATCH, N_STATE), F32),
                        pltpu.VMEM((SSM_WIDTH // LANES, rows, LANES), F32),
                        pltpu.VMEM((rows, SSM_WIDTH), F32)],
        compiler_params=pltpu.CompilerParams(
            dimension_semantics=("arbitrary",), vmem_limit_bytes=VMEM_LIMIT),
        name="s5_branch",
    )(us3, perm, *params)


def _tail_kernel(x_ref, att_ref, g_ref, ssm_ref, w_eo_ref, post_e_ref,
                 pre_ref, w_in_ref, pw_ref, ps_ref, w_out_ref, post_ref, o_ref,
                 ext, tmp0, tmp1):
    tm = ODD_TILE
    hist = POOL_HIST
    total = hist + tm
    j = pl.program_id(1)

    @pl.when(j == 0)
    def _():
        ext[0:hist, :] = jnp.zeros((hist, POOL_WIDTH), F32)

    tmps = (tmp0, tmp1)
    sub = ODD_SUB
    for s in range(tm // sub):
        lo = hist + s * sub
        hi = lo + sub
        rows = slice(s * sub, (s + 1) * sub)
        half = sub // 2
        y_even = []
        for h in range(2):
            r = slice(s * sub + h * half, s * sub + (h + 1) * half)
            gated = (att_ref[r, :].astype(F32) * _silu(g_ref[r, :].astype(F32))).astype(BF16)
            y_even.append(
                jnp.dot(gated, w_eo_ref[:ATT_WIDTH, :], preferred_element_type=F32)
                + jnp.dot(ssm_ref[r, :], w_eo_ref[ATT_WIDTH:, :], preferred_element_type=F32))
        x_parts, hb_parts = [], []
        for h in range(2):
            r = slice(s * sub + h * half, s * sub + (h + 1) * half)
            x_h = x_ref[r, :] + _rms(y_even[h], post_e_ref[...])
            hb_h = _rms(x_h, pre_ref[...]).astype(BF16)
            ext[lo + h * half:lo + (h + 1) * half, :] = jnp.dot(
                hb_h, w_in_ref[:, :POOL_WIDTH], preferred_element_type=F32)
            x_parts.append(x_h)
            hb_parts.append(hb_h)
        x = jnp.concatenate(x_parts, axis=0)
        hb = jnp.concatenate(hb_parts, axis=0)
        t_idx = j * tm + s * sub + lax.broadcasted_iota(jnp.int32, (sub, 1), 0)
        inv_count = 1.0 / (t_idx + 1).astype(F32)
        y = None
        for g, w in enumerate(POOL_WINDOWS):
            cols = slice(g * POOL_GROUP, (g + 1) * POOL_GROUP)
            levels = int(math.log2(w))
            src_ref, src_cols = ext, cols
            for lev in range(levels):
                shift = 2 ** lev
                start = lo - 8 * (levels - 1 - lev)
                val = (src_ref[start:hi, src_cols] + src_ref[start - shift:hi - shift, src_cols])
                if lev < levels - 1:
                    dst = tmps[lev % 2]
                    dst[start:hi, :] = val
                    src_ref, src_cols = dst, slice(None)
            u_g = ext[lo:hi, cols]
            mixed = val * jnp.maximum(inv_count, 1.0 / w) - u_g
            yg = jnp.dot(mixed.astype(BF16), pw_ref[g], preferred_element_type=F32)
            gate = jnp.dot(hb, w_in_ref[:, POOL_WIDTH + g * POOL_GROUP:
                                        POOL_WIDTH + (g + 1) * POOL_GROUP],
                           preferred_element_type=F32)
            yg = yg * ps_ref[:, cols] * _silu(gate)
            part = jnp.dot(yg.astype(BF16), w_out_ref[cols, :], preferred_element_type=F32)
            y = part if y is None else y + part
        o_ref[rows, :] = x + _rms(y, post_ref[...])
    ext[0:hist, :] = ext[tm:total, :]


def _even_tail_odd(x2, att2, g2, ssm2, w_even_out, pre, post, w_in, pool_w, pool_scale, w_out, i,
                   layer):
    tm = ODD_TILE
    nj = SEQ // tm
    row_spec = pl.BlockSpec((tm, D_MODEL), lambda b, j: (b * nj + j, 0))
    att_spec = pl.BlockSpec((tm, ATT_WIDTH), lambda b, j: (b * nj + j, 0))
    return pl.pallas_call(
        _tail_kernel,
        grid=(BATCH, nj),
        in_specs=[row_spec, att_spec, att_spec,
                  pl.BlockSpec((tm, SSM_WIDTH), lambda b, j: (b * nj + j, 0)),
                  _layer_spec(w_even_out, i), _layer_spec(post, layer),
                  _layer_spec(pre, layer + 1), _layer_spec(w_in, i),
                  _layer_spec(pool_w, i), _layer_spec(pool_scale, i),
                  _layer_spec(w_out, i), _layer_spec(post, layer + 1)],
        out_specs=row_spec,
        out_shape=jax.ShapeDtypeStruct((BATCH * SEQ, D_MODEL), F32),
        scratch_shapes=[pltpu.VMEM((POOL_HIST + tm, POOL_WIDTH), F32),
                        pltpu.VMEM((POOL_HIST + tm, POOL_GROUP), F32),
                        pltpu.VMEM((POOL_HIST + tm, POOL_GROUP), F32)],
        compiler_params=pltpu.CompilerParams(
            dimension_semantics=("arbitrary", "arbitrary"), vmem_limit_bytes=VMEM_LIMIT),
        name="even_tail_odd_layer",
    )(x2, att2, g2, ssm2, w_even_out, post, pre, w_in, pool_w, pool_scale, w_out, post)


def _rope_tables():
    half = ROT_DIM // 2
    inv_freq = (np.float32(ROPE_THETA)
                ** (-np.arange(0, ROT_DIM, 2, dtype=np.float32) / np.float32(ROT_DIM)))
    ang = np.arange(SEQ, dtype=np.float32)[:, None] * inv_freq[None, :].astype(np.float32)
    cos, sin = np.cos(ang).astype(np.float32), np.sin(ang).astype(np.float32)
    zeros = np.zeros((SEQ, HEAD_DIM - ROT_DIM), np.float32)
    zeros_h = np.zeros((SEQ, half), np.float32)
    cos_h = np.concatenate([cos, cos, np.ones_like(zeros)], axis=1)
    sa_h = np.concatenate([-sin, zeros_h, zeros], axis=1)
    sb_h = np.concatenate([zeros_h, sin, zeros], axis=1)
    rep = LANES // HEAD_DIM
    pos = np.arange(SEQ).reshape(SEQ // ROW_TILE, ROW_TILE // N_CLS, N_CLS)
    order = pos.transpose(0, 2, 1).reshape(SEQ)
    return tuple(jnp.asarray(np.tile(t, (1, rep))[order]) for t in (cos_h, sa_h, sb_h))


def _ssm_params(a_re, a_im, log_dt, b_re, b_im, c_re, c_im):
    n_layer = a_re.shape[0]
    lam = lax.complex(a_re.astype(F32), a_im.astype(F32))
    dt = jnp.exp(log_dt.astype(F32))[..., None]
    lam_bar = jnp.exp(lam * dt)
    b_bar = ((lam_bar - 1.0) / lam)[..., None] * lax.complex(b_re.astype(F32), b_im.astype(F32))
    lam_re = jnp.broadcast_to(jnp.real(lam_bar).reshape(n_layer, 1, N_STATE),
                              (n_layer, BATCH, N_STATE))
    lam_im = jnp.broadcast_to(jnp.imag(lam_bar).reshape(n_layer, 1, N_STATE),
                              (n_layer, BATCH, N_STATE))
    gpc = LANES // SSM_GROUP
    n_chunk = SSM_GROUPS // gpc
    eye = jnp.eye(gpc, dtype=F32)

    def block_diag_in(m):
        m = m.reshape(n_layer, n_chunk, gpc, SSM_STATE, SSM_GROUP)
        return jnp.einsum('lcgph,gk->lcghkp', m, eye).reshape(
            n_layer, n_chunk, gpc * SSM_GROUP, gpc * SSM_STATE)

    def block_diag_out(m):
        m = m.reshape(n_layer, n_chunk, gpc, SSM_GROUP, SSM_STATE)
        return jnp.einsum('lcghp,gk->lcgpkh', m, eye).reshape(
            n_layer, n_chunk, gpc * SSM_STATE, gpc * SSM_GROUP)

    wb = jnp.concatenate([block_diag_in(jnp.real(b_bar)), block_diag_in(jnp.imag(b_bar))],
                         axis=3).astype(BF16)
    wc_re = block_diag_out(c_re.astype(F32)).astype(BF16)
    wc_im = block_diag_out(-c_im.astype(F32)).astype(BF16)
    return lam_re, lam_im, wb, wc_re, wc_im


def kernel(x, pre_norm, post_norm, even_w_in, even_w_out, ssm_a_re, ssm_a_im, ssm_log_dt,
           ssm_b_re, ssm_b_im, ssm_c_re, ssm_c_im, ssm_d, ssm_glu_w, ssm_glu_b,
           odd_w_in, pool_w, pool_scale, odd_w_out):
    cos_t, sa_t, sb_t = _rope_tables()
    pre = pre_norm.astype(F32).reshape(DEPTH, 1, D_MODEL)
    post = post_norm.astype(F32).reshape(DEPTH, 1, D_MODEL)
    even_w_in_b = even_w_in.astype(BF16)
    even_w_out_b = even_w_out.astype(BF16)
    odd_w_in_b = odd_w_in.astype(BF16)
    odd_w_out_b = odd_w_out.astype(BF16)
    pool_w_b = pool_w.astype(BF16)
    pool_scale3 = pool_scale.astype(F32).reshape(-1, 1, POOL_WIDTH)
    ssm_params = _ssm_params(ssm_a_re, ssm_a_im, ssm_log_dt, ssm_b_re, ssm_b_im,
                             ssm_c_re, ssm_c_im) + (
        ssm_d.astype(F32).reshape(-1, 1, SSM_WIDTH), ssm_glu_w.astype(BF16),
        ssm_glu_b.astype(F32).reshape(-1, 1, SSM_WIDTH))
    x2 = x.reshape(BATCH * SEQ, D_MODEL)
    assert DEPTH % 2 == 0
    for layer in range(0, DEPTH, 2):
        i = layer // 2
        q16, k16, v16, g, us = _even_in(x2, pre, layer, even_w_in_b, i, cos_t, sa_t, sb_t)
        att = _attention(q16, k16, v16)
        ssm = _ssm(us.reshape(BATCH, SEQ, 2 * SSM_WIDTH), ssm_params, i)
        x2 = _even_tail_odd(x2, att.reshape(BATCH * SEQ, ATT_WIDTH), g,
                            ssm.reshape(BATCH * SEQ, SSM_WIDTH), even_w_out_b, pre, post,
                            odd_w_in_b, pool_w_b, pool_scale3, odd_w_out_b, i, layer)
    return x2.reshape(BATCH, SEQ, D_MODEL)
```

```python
import functools
import math

import jax
import jax.numpy as jnp
import numpy as np
from jax import lax
from jax.experimental import pallas as pl
from jax.experimental.pallas import tpu as pltpu

F32 = jnp.float32
BF16 = jnp.bfloat16

D_MODEL = 1024
BATCH = 8
SEQ = 2048
DEPTH = 4
HEAD_DIM = 64
ATT_WIDTH = 1024
ROT_DIM = 16
ROPE_THETA = 500000.0
SSM_WIDTH = 512
SSM_GROUP = 16
SSM_GROUPS = 32
SSM_STATE = 64
N_STATE = SSM_GROUPS * SSM_STATE
POOL_WIDTH = 2048
POOL_WINDOWS = (2, 4, 8, 16)
POOL_GROUP = 512
EVEN_IN = 4 * ATT_WIDTH + 2 * SSM_WIDTH
RMS_EPS = 1e-6
LOG2_E = math.log2(math.e)

LANES = 128
ATT_CHUNK = 128
ATT_GROUP = 4
NEG_BIG = -1e30
VMEM_LIMIT = 56 * 1024 * 1024
WEIGHT_CHUNK_BYTES = 3 << 19
BF16_SUBLANES = 16

ROW_TILE = 512
SSM_TC = 64
ODD_TILE = 512
ODD_SUB = 512
POOL_HIST = 32


def _sigmoid(x):
    return 0.5 + 0.5 * jnp.tanh(0.5 * x)


def _silu(x):
    h = 0.5 * x
    return h + h * jnp.tanh(h)


def _rms(x, gain):
    return x * lax.rsqrt(jnp.mean(x * x, axis=-1, keepdims=True) + RMS_EPS) * gain


def _const_spec(shape):
    nd = len(shape)
    return pl.BlockSpec(shape, lambda *_: (0,) * nd, pipeline_mode=pl.Buffered(1))


def _load_weight_bf16(w_hbm, layer, dst, stage, sem):
    rows = dst.shape[0]
    chunk = stage.shape[1]
    n_chunks = rows // chunk

    def copy(k):
        return pltpu.make_async_copy(
            w_hbm.at[layer, pl.ds(k * chunk, chunk), :], stage.at[k % 2], sem.at[k % 2])

    copy(0).start()
    for k in range(n_chunks):
        if k + 1 < n_chunks:
            copy(k + 1).start()
        copy(k).wait()
        dst[k * chunk:(k + 1) * chunk, :] = stage[k % 2].astype(BF16)


def _weight_scratch(rows, cols):
    chunk = rows
    while chunk * cols * 4 > WEIGHT_CHUNK_BYTES:
        chunk //= 2
    assert rows % chunk == 0 and chunk % BF16_SUBLANES == 0
    return [pltpu.VMEM((rows, cols), BF16), pltpu.VMEM((2, chunk, cols), F32),
            pltpu.SemaphoreType.DMA((2,))]


def _first_step():
    return jnp.logical_and(pl.program_id(0) == 0, pl.program_id(1) == 0)


def _layer_spec(stacked, layer):
    shape = stacked.shape[1:]
    nd = len(shape)
    return pl.BlockSpec((None,) + tuple(shape), lambda *_: (layer,) + (0,) * nd,
                        pipeline_mode=pl.Buffered(1))


def _even_in_kernel(layer_idx, x_ref, gain_ref, w_hbm, cos_ref, sa_ref, sb_ref,
                    q16_ref, k16_ref, v16_ref, g_ref, us_ref, h_nat, h_cls, w_ref, stage, sem):
    tm = ROW_TILE
    per_cls = tm // N_CLS

    @pl.when(_first_step())
    def _():
        _load_weight_bf16(w_hbm, layer_idx, w_ref, stage, sem)

    def proj(lhs, lo, width):
        return jnp.dot(lhs, w_ref[:, lo:lo + width], preferred_element_type=F32)

    half = tm // 2
    for hh in range(2):
        r = slice(hh * half, (hh + 1) * half)
        h = _rms(x_ref[r, :], gain_ref[...])
        for c in range(D_MODEL // LANES):
            h_nat[c, r, :] = h[:, c * LANES:(c + 1) * LANES]
        hb = h.astype(BF16)
        g_ref[r, :] = proj(hb, 3 * ATT_WIDTH, ATT_WIDTH).astype(g_ref.dtype)
        us_ref[r, :] = proj(hb, 4 * ATT_WIDTH, 2 * SSM_WIDTH).astype(us_ref.dtype)
    for c in range(D_MODEL // LANES):
        for cls in range(N_CLS):
            h_cls[cls * per_cls:(cls + 1) * per_cls, c * LANES:(c + 1) * LANES] = (
                h_nat[c, pl.ds(cls, per_cls, stride=N_CLS), :].astype(BF16))
    hb_cls = h_cls[...]
    cos = cos_ref[...]
    sa = sa_ref[...]
    sb = sb_ref[...]

    def rope_store(dst, y, scale):
        for c in range(ATT_WIDTH // LANES):
            yc = y[:, c * LANES:(c + 1) * LANES]
            r = (yc * cos + pltpu.roll(yc, LANES - ROT_DIM // 2, 1) * sa
                 + pltpu.roll(yc, ROT_DIM // 2, 1) * sb)
            dst[0, :, :, c * LANES:(c + 1) * LANES] = (r * scale).reshape(N_CLS, per_cls, LANES)

    rope_store(q16_ref, proj(hb_cls, 0, ATT_WIDTH), HEAD_DIM ** -0.5 * LOG2_E)
    rope_store(k16_ref, proj(hb_cls, ATT_WIDTH, ATT_WIDTH), 1.0)
    v16_ref[0] = proj(hb_cls, 2 * ATT_WIDTH, ATT_WIDTH).reshape(N_CLS, per_cls, ATT_WIDTH)


def _even_in(x2, gains, layer, w_in, i, cos_t, sa_t, sb_t):
    tm = ROW_TILE
    nj = SEQ // tm
    row_spec = pl.BlockSpec((tm, D_MODEL), lambda b, j: (b * nj + j, 0))
    tab_spec = pl.BlockSpec((tm, LANES), lambda b, j: (j, 0))
    att_spec = pl.BlockSpec((tm, ATT_WIDTH), lambda b, j: (b * nj + j, 0))
    cls_spec = pl.BlockSpec((1, N_CLS, tm // N_CLS, ATT_WIDTH), lambda b, j: (b, 0, j, 0))
    att_shape = jax.ShapeDtypeStruct((BATCH * SEQ, ATT_WIDTH), BF16)
    cls_shape = jax.ShapeDtypeStruct((BATCH, N_CLS, CLS_LEN, ATT_WIDTH), F32)
    return pl.pallas_call(
        functools.partial(_even_in_kernel, i),
        grid=(BATCH, nj),
        in_specs=[row_spec, _layer_spec(gains, layer), pl.BlockSpec(memory_space=pl.ANY),
                  tab_spec, tab_spec, tab_spec],
        out_specs=[cls_spec, cls_spec, cls_spec, att_spec, att_spec],
        out_shape=[cls_shape, cls_shape, cls_shape, att_shape, att_shape],
        scratch_shapes=[pltpu.VMEM((D_MODEL // LANES, tm, LANES), F32),
                        pltpu.VMEM((tm, D_MODEL), BF16)] + _weight_scratch(D_MODEL, EVEN_IN),
        compiler_params=pltpu.CompilerParams(
            dimension_semantics=("arbitrary", "arbitrary"), vmem_limit_bytes=VMEM_LIMIT),
        name="even_in_proj",
    )(x2, gains, w_in, cos_t, sa_t, sb_t)


N_CLS = 16
CLS_LEN = SEQ // N_CLS


def _attn_masks():
    ch = ATT_CHUNK
    rho = np.arange(ch)
    tau = 16 * (rho % 8) + rho // 8
    kap = np.arange(2 * ch)
    kk = 16 * (kap % 16) + kap // 16
    p1_full = np.where(kk[None, :] < ch, kk[None, :] >= tau[:, None],
                       (kk[None, :] - ch) <= tau[:, None])
    kap1 = np.arange(ch)
    p1_first = (16 * (kap1 % 8) + kap1 // 8)[None, :] <= tau[:, None]
    a = 4 * (rho % 32) + rho // 32
    kap = np.arange(2 * ch)
    kk2 = 4 * (kap % 64) + kap // 64
    p2_full = (kk2[None, :] >= a[:, None]) & (kk2[None, :] <= a[:, None] + ch)
    kap1 = np.arange(ch)
    kk1 = 4 * (kap1 % 32) + kap1 // 32
    p2_first = kk1[None, :] <= a[:, None]
    p3 = np.arange(ch)[None, :] <= rho[:, None]
    valid = np.concatenate([p1_full, p1_first, p2_full, p2_first, p3], axis=1)
    return jnp.asarray(np.where(valid, 0.0, NEG_BIG), dtype=F32)


def _attn_kernel(q16_ref, k16_ref, v16_ref, mask_ref, o_ref, m_a, m_b, acc_a, acc_b, fin):
    ch = ATT_CHUNK
    lane = lax.broadcasted_iota(jnp.int32, (1, LANES), 1)
    is_a = lane < HEAD_DIM
    is_b = jnp.logical_not(is_a)
    state = ((m_a, acc_a), (m_b, acc_b))
    ones_row = jnp.ones((1, LANES), BF16)
    zeros_row = jnp.zeros((1, LANES), BF16)
    mask_p1_full = slice(0, 2 * ch)
    mask_p1_first = slice(2 * ch, 3 * ch)
    mask_p2_full = slice(3 * ch, 5 * ch)
    mask_p2_first = slice(5 * ch, 6 * ch)
    mask_p3 = slice(6 * ch, 7 * ch)

    def gather(ref, slabs):
        parts = [ref[s:s + n, :] for s, n in slabs]
        return parts[0] if len(parts) == 1 else jnp.concatenate(parts, axis=0)

    def gather_cls(ref, slabs):
        parts = [ref[0, s // CLS_LEN, s % CLS_LEN:s % CLS_LEN + n, :] for s, n in slabs]
        return parts[0] if len(parts) == 1 else jnp.concatenate(parts, axis=0)

    def scatter(ref, slabs, val):
        off = 0
        for s, n in slabs:
            ref[s:s + n, :] = val[off:off + n, :]
            off += n

    def phase_scores(items):
        scores = []
        for load_q, load_k, load_v, mask, slabs in items:
            qc = load_q()
            kc = load_k()
            for is_h in (is_a, is_b):
                qh = jnp.where(is_h, qc, zeros_row)
                s = lax.dot_general(qh, kc, (((1,), (1,)), ((), ())),
                                    preferred_element_type=F32) + mask_ref[:, mask]
                scores.append((s, jnp.max(s, axis=1, keepdims=True)))
        return scores

    def phase_update(items, scores, has_state, finalize):
        partial = []
        n = 0
        for load_q, load_k, load_v, mask, slabs in items:
            vc = load_v()
            for h, is_h in enumerate((is_a, is_b)):
                s, m_c = scores[n]
                n += 1
                if has_state:
                    m_prev = gather(state[h][0], slabs)
                    m_new = jnp.maximum(m_prev, m_c)
                else:
                    m_prev = None
                    m_new = jnp.broadcast_to(m_c, (ch, LANES))
                m_k = jnp.concatenate([m_new] * (s.shape[1] // LANES), axis=1)
                p = jnp.exp2(s - m_k)
                vh = jnp.where(is_h, vc, ones_row)
                pv = jnp.dot(p.astype(BF16), vh, preferred_element_type=F32)
                partial.append((m_prev, m_new, pv))
        n = 0
        for load_q, load_k, load_v, mask, slabs in items:
            res = []
            for h in range(2):
                m_prev, m_new, pv = partial[n]
                n += 1
                if has_state:
                    a_new = jnp.exp2(m_prev - m_new) * gather(state[h][1], slabs) + pv
                else:
                    a_new = pv
                res.append(a_new)
                if not finalize:
                    scatter(state[h][0], slabs, m_new)
                    scatter(state[h][1], slabs, a_new)
            if finalize:
                num = jnp.where(is_a, res[0], res[1])
                den = jnp.where(is_a, pltpu.roll(res[0], HEAD_DIM, 1),
                                pltpu.roll(res[1], HEAD_DIM, 1))
                scatter(fin, slabs, num / den)

    groups = []

    def add_groups(items, has_state, finalize):
        for i in range(0, len(items), ATT_GROUP):
            groups.append((items[i:i + ATT_GROUP], has_state, finalize))

    def block(slabs, kslabs, mask):
        return (lambda: gather_cls(q16_ref, slabs).astype(BF16),
                lambda: gather_cls(k16_ref, kslabs).astype(BF16),
                lambda: gather_cls(v16_ref, kslabs).astype(BF16), mask, slabs)

    per = ch // N_CLS
    items = []
    for i in range(SEQ // ch):
        slabs = [(r * CLS_LEN + per * i, per) for r in range(N_CLS)]
        if i > 0:
            kslabs = [(r * CLS_LEN + per * (i - 1), 2 * per) for r in range(N_CLS)]
            items.append(block(slabs, kslabs, mask_p1_full))
        else:
            items.append(block(slabs, slabs, mask_p1_first))
    add_groups(items, False, False)

    quarter = ch // 4
    items = []
    for r4 in range(4):
        for ic in range(SEQ // (4 * ch)):
            cls = [(4 * j + r4) * CLS_LEN for j in range(4)]
            slabs = [(c + quarter * ic, quarter) for c in cls]
            if ic > 0:
                kslabs = [(c + quarter * (ic - 1), 2 * quarter) for c in cls]
                items.append(block(slabs, kslabs, mask_p2_full))
            else:
                items.append(block(slabs, slabs, mask_p2_first))
    add_groups(items, True, False)

    items = []
    for r in range(N_CLS):
        slabs = [(r * CLS_LEN, CLS_LEN)]
        items.append(block(slabs, slabs, mask_p3))
    add_groups(items, True, True)

    pending = None
    for items, has_state, finalize in groups:
        scores = phase_scores(items)
        if pending is not None:
            phase_update(*pending)
        pending = (items, scores, has_state, finalize)
    phase_update(*pending)

    row = lax.broadcasted_iota(jnp.int32, (ch, ch), 0)
    col = lax.broadcasted_iota(jnp.int32, (ch, ch), 1)
    shuffle = jnp.where(col == per * (row % N_CLS) + row // N_CLS, 1.0, 0.0).astype(BF16)
    for i in range(SEQ // ch):
        slab = gather(fin, [(r * CLS_LEN + per * i, per) for r in range(N_CLS)])
        o_ref[0, i * ch:(i + 1) * ch, :] = jnp.dot(
            shuffle, slab.astype(BF16), preferred_element_type=F32).astype(o_ref.dtype)


def _attention(q16, k16, v16):
    blk = pl.BlockSpec((1, SEQ, LANES), lambda b, h: (b, 0, h))
    cls_blk = pl.BlockSpec((1, N_CLS, CLS_LEN, LANES), lambda b, h: (b, 0, 0, h))
    f32_scr = pltpu.VMEM((SEQ, LANES), F32)
    masks = _attn_masks()
    return pl.pallas_call(
        _attn_kernel,
        grid=(BATCH, ATT_WIDTH // LANES),
        in_specs=[cls_blk, cls_blk, cls_blk, _const_spec(masks.shape)],
        out_specs=blk,
        out_shape=jax.ShapeDtypeStruct((BATCH, SEQ, ATT_WIDTH), BF16),
        scratch_shapes=[f32_scr] * 5,
        compiler_params=pltpu.CompilerParams(
            dimension_semantics=("arbitrary", "arbitrary"), vmem_limit_bytes=VMEM_LIMIT),
        name="dilated_attention",
    )(q16, k16, v16, masks)


def _ssm_kernel(us_ref, perm_ref, lam_re_ref, lam_im_ref, wb_ref, wc_re_ref, wc_im_ref,
                d_ref, gw_ref, gb_ref, o_ref, xr, xi, st_re, st_im, y_tb, y_bt):
    tc = SSM_TC
    rows = tc * BATCH
    n_chunk = SSM_WIDTH // LANES
    cpc = N_STATE // n_chunk

    @pl.when(pl.program_id(0) == 0)
    def _():
        st_re[...] = jnp.zeros_like(st_re)
        st_im[...] = jnp.zeros_like(st_im)

    ub = us_ref[:, :, :SSM_WIDTH].reshape(rows, SSM_WIDTH)
    u_tb = jnp.dot(perm_ref[...], ub, preferred_element_type=F32).astype(BF16)
    def b_proj(c):
        cols = slice(c * cpc, (c + 1) * cpc)
        bu = jnp.dot(u_tb[:, c * LANES:(c + 1) * LANES], wb_ref[c], preferred_element_type=F32)
        xr[:, cols] = bu[:, :cpc]
        xi[:, cols] = bu[:, cpc:]

    def recurrence(c):
        cols = slice(c * cpc, (c + 1) * cpc)
        lr = lam_re_ref[:, cols]
        li = lam_im_ref[:, cols]
        sr = st_re[:, cols]
        si = st_im[:, cols]
        for t in range(tc):
            trow = slice(t * BATCH, (t + 1) * BATCH)
            sr, si = (lr * sr - li * si + xr[trow, cols], lr * si + li * sr + xi[trow, cols])
            xr[trow, cols] = sr
            xi[trow, cols] = si
        st_re[:, cols] = sr
        st_im[:, cols] = si

    def c_proj(c):
        cols = slice(c * cpc, (c + 1) * cpc)
        y_tb[c] = (
            jnp.dot(xr[:, cols].astype(BF16), wc_re_ref[c], preferred_element_type=F32)
            + jnp.dot(xi[:, cols].astype(BF16), wc_im_ref[c], preferred_element_type=F32))

    def skip_gelu(c):
        lanes = slice(c * LANES, (c + 1) * LANES)
        y_c = jnp.concatenate(
            [y_tb[c, pl.ds(b, tc, stride=BATCH), :] for b in range(BATCH)], axis=0)
        y_bt[:, lanes] = jax.nn.gelu(y_c + d_ref[:, lanes] * ub[:, lanes].astype(F32))

    b_proj(0)
    gate = _silu(us_ref[:, :, SSM_WIDTH:].reshape(rows, SSM_WIDTH).astype(F32))
    for c in range(n_chunk):
        if c + 1 < n_chunk:
            b_proj(c + 1)
        recurrence(c)
        if c > 0:
            skip_gelu(c - 1)
        c_proj(c)
    skip_gelu(n_chunk - 1)
    y = y_bt[...]
    z = jnp.dot(y.astype(BF16), gw_ref[...], preferred_element_type=F32) + gb_ref[...]
    out = y * _sigmoid(z) * gate
    o_ref[...] = out.reshape(BATCH, tc, SSM_WIDTH).astype(o_ref.dtype)


def _ssm(us3, params, i):
    tc = SSM_TC
    rows = tc * BATCH
    dst_row = np.arange(rows)
    perm = jnp.asarray((dst_row[:, None] % BATCH) * tc + dst_row[:, None] // BATCH
                       == dst_row[None, :], dtype=BF16)
    return pl.pallas_call(
        _ssm_kernel,
        grid=(SEQ // tc,),
        in_specs=[pl.BlockSpec((BATCH, tc, 2 * SSM_WIDTH), lambda t: (0, t, 0)),
                  _const_spec((rows, rows))] + [_layer_spec(p, i) for p in params],
        out_specs=pl.BlockSpec((BATCH, tc, SSM_WIDTH), lambda t: (0, t, 0)),
        out_shape=jax.ShapeDtypeStruct((BATCH, SEQ, SSM_WIDTH), BF16),
        scratch_shapes=[pltpu.VMEM((rows, N_STATE), F32), pltpu.VMEM((rows, N_STATE), F32),
                        pltpu.VMEM((BATCH, N_STATE), F32), pltpu.VMEM((BATCH, N_STATE), F32),
                        pltpu.VMEM((SSM_WIDTH // LANES, rows, LANES), F32),
                        pltpu.VMEM((rows, SSM_WIDTH), F32)],
        compiler_params=pltpu.CompilerParams(
            dimension_semantics=("arbitrary",), vmem_limit_bytes=VMEM_LIMIT),
        name="s5_branch",
    )(us3, perm, *params)


def _tail_kernel(layer_idx, x_ref, att_ref, g_ref, ssm_ref, w_eo_hbm, post_e_ref,
                 pre_ref, w_in_hbm, pw_hbm, ps_ref, w_out_hbm, post_ref, o_ref,
                 ext, tmp0, tmp1,
                 w_eo_ref, stage_eo, sem_eo, w_in_ref, stage_in, sem_in,
                 pw_ref, stage_pw, sem_pw, w_out_ref, stage_out, sem_out):
    tm = ODD_TILE
    hist = POOL_HIST
    total = hist + tm
    j = pl.program_id(1)

    @pl.when(_first_step())
    def _():
        _load_weight_bf16(w_eo_hbm, layer_idx, w_eo_ref, stage_eo, sem_eo)
        _load_weight_bf16(w_in_hbm, layer_idx, w_in_ref, stage_in, sem_in)
        _load_weight_bf16(pw_hbm, layer_idx, pw_ref, stage_pw, sem_pw)
        _load_weight_bf16(w_out_hbm, layer_idx, w_out_ref, stage_out, sem_out)

    @pl.when(j == 0)
    def _():
        ext[0:hist, :] = jnp.zeros((hist, POOL_WIDTH), F32)

    tmps = (tmp0, tmp1)
    sub = ODD_SUB
    for s in range(tm // sub):
        lo = hist + s * sub
        hi = lo + sub
        rows = slice(s * sub, (s + 1) * sub)
        half = sub // 2
        y_even = []
        for h in range(2):
            r = slice(s * sub + h * half, s * sub + (h + 1) * half)
            gated = (att_ref[r, :].astype(F32) * _silu(g_ref[r, :].astype(F32))).astype(BF16)
            y_even.append(
                jnp.dot(gated, w_eo_ref[:ATT_WIDTH, :], preferred_element_type=F32)
                + jnp.dot(ssm_ref[r, :], w_eo_ref[ATT_WIDTH:, :], preferred_element_type=F32))
        x_parts, hb_parts = [], []
        for h in range(2):
            r = slice(s * sub + h * half, s * sub + (h + 1) * half)
            x_h = x_ref[r, :] + _rms(y_even[h], post_e_ref[...])
            hb_h = _rms(x_h, pre_ref[...]).astype(BF16)
            ext[lo + h * half:lo + (h + 1) * half, :] = jnp.dot(
                hb_h, w_in_ref[:, :POOL_WIDTH], preferred_element_type=F32)
            x_parts.append(x_h)
            hb_parts.append(hb_h)
        x = jnp.concatenate(x_parts, axis=0)
        hb = jnp.concatenate(hb_parts, axis=0)
        t_idx = j * tm + s * sub + lax.broadcasted_iota(jnp.int32, (sub, 1), 0)
        inv_count = 1.0 / (t_idx + 1).astype(F32)
        y = None
        for g, w in enumerate(POOL_WINDOWS):
            cols = slice(g * POOL_GROUP, (g + 1) * POOL_GROUP)
            levels = int(math.log2(w))
            src_ref, src_cols = ext, cols
            for lev in range(levels):
                shift = 2 ** lev
                start = lo - 8 * (levels - 1 - lev)
                val = (src_ref[start:hi, src_cols] + src_ref[start - shift:hi - shift, src_cols])
                if lev < levels - 1:
                    dst = tmps[lev % 2]
                    dst[start:hi, :] = val
                    src_ref, src_cols = dst, slice(None)
            u_g = ext[lo:hi, cols]
            mixed = val * jnp.maximum(inv_count, 1.0 / w) - u_g
            yg = jnp.dot(mixed.astype(BF16), pw_ref[cols, :], preferred_element_type=F32)
            gate = jnp.dot(hb, w_in_ref[:, POOL_WIDTH + g * POOL_GROUP:
                                        POOL_WIDTH + (g + 1) * POOL_GROUP],
                           preferred_element_type=F32)
            yg = yg * ps_ref[:, cols] * _silu(gate)
            part = jnp.dot(yg.astype(BF16), w_out_ref[cols, :], preferred_element_type=F32)
            y = part if y is None else y + part
        o_ref[rows, :] = x + _rms(y, post_ref[...])
    ext[0:hist, :] = ext[tm:total, :]


def _even_tail_odd(x2, att2, g2, ssm2, w_even_out, pre, post, w_in, pool_w, pool_scale, w_out, i,
                   layer):
    tm = ODD_TILE
    nj = SEQ // tm
    row_spec = pl.BlockSpec((tm, D_MODEL), lambda b, j: (b * nj + j, 0))
    att_spec = pl.BlockSpec((tm, ATT_WIDTH), lambda b, j: (b * nj + j, 0))
    hbm = pl.BlockSpec(memory_space=pl.ANY)
    return pl.pallas_call(
        functools.partial(_tail_kernel, i),
        grid=(BATCH, nj),
        in_specs=[row_spec, att_spec, att_spec,
                  pl.BlockSpec((tm, SSM_WIDTH), lambda b, j: (b * nj + j, 0)),
                  hbm, _layer_spec(post, layer),
                  _layer_spec(pre, layer + 1), hbm,
                  hbm, _layer_spec(pool_scale, i),
                  hbm, _layer_spec(post, layer + 1)],
        out_specs=row_spec,
        out_shape=jax.ShapeDtypeStruct((BATCH * SEQ, D_MODEL), F32),
        scratch_shapes=[pltpu.VMEM((POOL_HIST + tm, POOL_WIDTH), F32),
                        pltpu.VMEM((POOL_HIST + tm, POOL_GROUP), F32),
                        pltpu.VMEM((POOL_HIST + tm, POOL_GROUP), F32)]
        + _weight_scratch(*w_even_out.shape[1:]) + _weight_scratch(*w_in.shape[1:])
        + _weight_scratch(*pool_w.shape[1:]) + _weight_scratch(*w_out.shape[1:]),
        compiler_params=pltpu.CompilerParams(
            dimension_semantics=("arbitrary", "arbitrary"), vmem_limit_bytes=VMEM_LIMIT),
        name="even_tail_odd_layer",
    )(x2, att2, g2, ssm2, w_even_out, post, pre, w_in, pool_w, pool_scale, w_out, post)


def _rope_tables():
    half = ROT_DIM // 2
    inv_freq = (np.float32(ROPE_THETA)
                ** (-np.arange(0, ROT_DIM, 2, dtype=np.float32) / np.float32(ROT_DIM)))
    ang = np.arange(SEQ, dtype=np.float32)[:, None] * inv_freq[None, :].astype(np.float32)
    cos, sin = np.cos(ang).astype(np.float32), np.sin(ang).astype(np.float32)
    zeros = np.zeros((SEQ, HEAD_DIM - ROT_DIM), np.float32)
    zeros_h = np.zeros((SEQ, half), np.float32)
    cos_h = np.concatenate([cos, cos, np.ones_like(zeros)], axis=1)
    sa_h = np.concatenate([-sin, zeros_h, zeros], axis=1)
    sb_h = np.concatenate([zeros_h, sin, zeros], axis=1)
    rep = LANES // HEAD_DIM
    pos = np.arange(SEQ).reshape(SEQ // ROW_TILE, ROW_TILE // N_CLS, N_CLS)
    order = pos.transpose(0, 2, 1).reshape(SEQ)
    return tuple(jnp.asarray(np.tile(t, (1, rep))[order]) for t in (cos_h, sa_h, sb_h))


def _ssm_params(a_re, a_im, log_dt, b_re, b_im, c_re, c_im):
    n_layer = a_re.shape[0]
    lam = lax.complex(a_re.astype(F32), a_im.astype(F32))
    dt = jnp.exp(log_dt.astype(F32))[..., None]
    lam_bar = jnp.exp(lam * dt)
    b_bar = ((lam_bar - 1.0) / lam)[..., None] * lax.complex(b_re.astype(F32), b_im.astype(F32))
    lam_re = jnp.broadcast_to(jnp.real(lam_bar).reshape(n_layer, 1, N_STATE),
                              (n_layer, BATCH, N_STATE))
    lam_im = jnp.broadcast_to(jnp.imag(lam_bar).reshape(n_layer, 1, N_STATE),
                              (n_layer, BATCH, N_STATE))
    gpc = LANES // SSM_GROUP
    n_chunk = SSM_GROUPS // gpc
    eye = jnp.eye(gpc, dtype=F32)

    def block_diag_in(m):
        m = m.reshape(n_layer, n_chunk, gpc, SSM_STATE, SSM_GROUP)
        return jnp.einsum('lcgph,gk->lcghkp', m, eye).reshape(
            n_layer, n_chunk, gpc * SSM_GROUP, gpc * SSM_STATE)

    def block_diag_out(m):
        m = m.reshape(n_layer, n_chunk, gpc, SSM_GROUP, SSM_STATE)
        return jnp.einsum('lcghp,gk->lcgpkh', m, eye).reshape(
            n_layer, n_chunk, gpc * SSM_STATE, gpc * SSM_GROUP)

    wb = jnp.concatenate([block_diag_in(jnp.real(b_bar)), block_diag_in(jnp.imag(b_bar))],
                         axis=3).astype(BF16)
    wc_re = block_diag_out(c_re.astype(F32)).astype(BF16)
    wc_im = block_diag_out(-c_im.astype(F32)).astype(BF16)
    return lam_re, lam_im, wb, wc_re, wc_im


def kernel(x, pre_norm, post_norm, even_w_in, even_w_out, ssm_a_re, ssm_a_im, ssm_log_dt,
           ssm_b_re, ssm_b_im, ssm_c_re, ssm_c_im, ssm_d, ssm_glu_w, ssm_glu_b,
           odd_w_in, pool_w, pool_scale, odd_w_out):
    cos_t, sa_t, sb_t = _rope_tables()
    pre = pre_norm.astype(F32).reshape(DEPTH, 1, D_MODEL)
    post = post_norm.astype(F32).reshape(DEPTH, 1, D_MODEL)
    even_w_in_f = even_w_in.astype(F32)
    even_w_out_f = even_w_out.astype(F32)
    odd_w_in_f = odd_w_in.astype(F32)
    odd_w_out_f = odd_w_out.astype(F32)
    pool_w_f = pool_w.astype(F32).reshape(-1, POOL_WIDTH, POOL_GROUP)
    pool_scale3 = pool_scale.astype(F32).reshape(-1, 1, POOL_WIDTH)
    ssm_params = _ssm_params(ssm_a_re, ssm_a_im, ssm_log_dt, ssm_b_re, ssm_b_im,
                             ssm_c_re, ssm_c_im) + (
        ssm_d.astype(F32).reshape(-1, 1, SSM_WIDTH), ssm_glu_w.astype(BF16),
        ssm_glu_b.astype(F32).reshape(-1, 1, SSM_WIDTH))
    x2 = x.reshape(BATCH * SEQ, D_MODEL)
    assert DEPTH % 2 == 0
    for layer in range(0, DEPTH, 2):
        i = layer // 2
        q16, k16, v16, g, us = _even_in(x2, pre, layer, even_w_in_f, i, cos_t, sa_t, sb_t)
        att = _attention(q16, k16, v16)
        ssm = _ssm(us.reshape(BATCH, SEQ, 2 * SSM_WIDTH), ssm_params, i)
        x2 = _even_tail_odd(x2, att.reshape(BATCH * SEQ, ATT_WIDTH), g,
                            ssm.reshape(BATCH * SEQ, SSM_WIDTH), even_w_out_f, pre, post,
                            odd_w_in_f, pool_w_f, pool_scale3, odd_w_out_f, i, layer)
    return x2.reshape(BATCH, SEQ, D_MODEL)
```

```python
import functools
import math

import jax
import jax.numpy as jnp
import numpy as np
from jax import lax
from jax.experimental import pallas as pl
from jax.experimental.pallas import tpu as pltpu

F32 = jnp.float32
BF16 = jnp.bfloat16

D_MODEL = 1024
BATCH = 8
SEQ = 2048
DEPTH = 4
HEAD_DIM = 64
ATT_WIDTH = 1024
ROT_DIM = 16
ROPE_THETA = 500000.0
SSM_WIDTH = 512
SSM_GROUP = 16
SSM_GROUPS = 32
SSM_STATE = 64
N_STATE = SSM_GROUPS * SSM_STATE
POOL_WIDTH = 2048
POOL_WINDOWS = (2, 4, 8, 16)
POOL_GROUP = 512
EVEN_IN = 4 * ATT_WIDTH + 2 * SSM_WIDTH
RMS_EPS = 1e-6
LOG2_E = math.log2(math.e)

LANES = 128
ATT_CHUNK = 128
ATT_GROUP = 4
NEG_BIG = -1e30
VMEM_LIMIT = 56 * 1024 * 1024
WEIGHT_CHUNK_BYTES = 3 << 19
BF16_SUBLANES = 16

ROW_TILE = 512
SSM_TC = 64
ODD_TILE = 512
ODD_SUB = 512
POOL_HIST = 32


def _sigmoid(x):
    return 0.5 + 0.5 * jnp.tanh(0.5 * x)


def _silu(x):
    h = 0.5 * x
    return h + h * jnp.tanh(h)


def _rms(x, gain):
    return x * lax.rsqrt(jnp.mean(x * x, axis=-1, keepdims=True) + RMS_EPS) * gain


def _const_spec(shape):
    nd = len(shape)
    return pl.BlockSpec(shape, lambda *_: (0,) * nd, pipeline_mode=pl.Buffered(1))


def _load_weights_bf16(layer, weights):
    def copy(w_hbm, stage, sem, k):
        slots, chunk = stage.shape[0], stage.shape[1]
        return pltpu.make_async_copy(
            w_hbm.at[layer, pl.ds(k * chunk, chunk), :], stage.at[k % slots], sem.at[k % slots])

    n_chunks = [dst.shape[0] // stage.shape[1] for _, dst, stage, _ in weights]
    for (w_hbm, dst, stage, sem), n in zip(weights, n_chunks):
        for k in range(min(stage.shape[0], n)):
            copy(w_hbm, stage, sem, k).start()
    for k in range(max(n_chunks)):
        for (w_hbm, dst, stage, sem), n in zip(weights, n_chunks):
            if k >= n:
                continue
            slots, chunk = stage.shape[0], stage.shape[1]
            copy(w_hbm, stage, sem, k).wait()
            dst[k * chunk:(k + 1) * chunk, :] = stage[k % slots].astype(BF16)
            if k + slots < n:
                copy(w_hbm, stage, sem, k + slots).start()


def _weight_scratch(rows, cols, slots=2):
    chunk = rows
    while chunk * cols * 4 > WEIGHT_CHUNK_BYTES:
        chunk //= 2
    assert rows % chunk == 0 and chunk % BF16_SUBLANES == 0
    return [pltpu.VMEM((rows, cols), BF16), pltpu.VMEM((slots, chunk, cols), F32),
            pltpu.SemaphoreType.DMA((slots,))]


def _first_step():
    return jnp.logical_and(pl.program_id(0) == 0, pl.program_id(1) == 0)


def _layer_spec(stacked, layer):
    shape = stacked.shape[1:]
    nd = len(shape)
    return pl.BlockSpec((None,) + tuple(shape), lambda *_: (layer,) + (0,) * nd,
                        pipeline_mode=pl.Buffered(1))


def _even_in_kernel(layer_idx, x_ref, gain_ref, w_hbm, cos_ref, sa_ref, sb_ref,
                    q16_ref, k16_ref, v16_ref, g_ref, us_ref, h_nat, h_cls, w_ref, stage, sem):
    tm = ROW_TILE
    per_cls = tm // N_CLS

    @pl.when(_first_step())
    def _():
        _load_weights_bf16(layer_idx, [(w_hbm, w_ref, stage, sem)])

    def proj(lhs, lo, width):
        return jnp.dot(lhs, w_ref[:, lo:lo + width], preferred_element_type=F32)

    half = tm // 2
    for hh in range(2):
        r = slice(hh * half, (hh + 1) * half)
        h = _rms(x_ref[r, :], gain_ref[...])
        for c in range(D_MODEL // LANES):
            h_nat[c, r, :] = h[:, c * LANES:(c + 1) * LANES]
        hb = h.astype(BF16)
        g_ref[r, :] = proj(hb, 3 * ATT_WIDTH, ATT_WIDTH).astype(g_ref.dtype)
        us_ref[r, :] = proj(hb, 4 * ATT_WIDTH, 2 * SSM_WIDTH).astype(us_ref.dtype)
    for c in range(D_MODEL // LANES):
        for cls in range(N_CLS):
            h_cls[cls * per_cls:(cls + 1) * per_cls, c * LANES:(c + 1) * LANES] = (
                h_nat[c, pl.ds(cls, per_cls, stride=N_CLS), :].astype(BF16))
    hb_cls = h_cls[...]
    cos = cos_ref[...]
    sa = sa_ref[...]
    sb = sb_ref[...]

    def rope_store(dst, y, scale):
        for c in range(ATT_WIDTH // LANES):
            yc = y[:, c * LANES:(c + 1) * LANES]
            r = (yc * cos + pltpu.roll(yc, LANES - ROT_DIM // 2, 1) * sa
                 + pltpu.roll(yc, ROT_DIM // 2, 1) * sb)
            dst[0, :, :, c * LANES:(c + 1) * LANES] = (r * scale).reshape(N_CLS, per_cls, LANES)

    rope_store(q16_ref, proj(hb_cls, 0, ATT_WIDTH), HEAD_DIM ** -0.5 * LOG2_E)
    rope_store(k16_ref, proj(hb_cls, ATT_WIDTH, ATT_WIDTH), 1.0)
    v16_ref[0] = proj(hb_cls, 2 * ATT_WIDTH, ATT_WIDTH).reshape(N_CLS, per_cls, ATT_WIDTH)


def _even_in(x2, gains, layer, w_in, i, cos_t, sa_t, sb_t):
    tm = ROW_TILE
    nj = SEQ // tm
    row_spec = pl.BlockSpec((tm, D_MODEL), lambda b, j: (b * nj + j, 0))
    tab_spec = pl.BlockSpec((tm, LANES), lambda b, j: (j, 0))
    att_spec = pl.BlockSpec((tm, ATT_WIDTH), lambda b, j: (b * nj + j, 0))
    cls_spec = pl.BlockSpec((1, N_CLS, tm // N_CLS, ATT_WIDTH), lambda b, j: (b, 0, j, 0))
    att_shape = jax.ShapeDtypeStruct((BATCH * SEQ, ATT_WIDTH), BF16)
    cls_shape = jax.ShapeDtypeStruct((BATCH, N_CLS, CLS_LEN, ATT_WIDTH), F32)
    return pl.pallas_call(
        functools.partial(_even_in_kernel, i),
        grid=(BATCH, nj),
        in_specs=[row_spec, _layer_spec(gains, layer), pl.BlockSpec(memory_space=pl.ANY),
                  tab_spec, tab_spec, tab_spec],
        out_specs=[cls_spec, cls_spec, cls_spec, att_spec, att_spec],
        out_shape=[cls_shape, cls_shape, cls_shape, att_shape, att_shape],
        scratch_shapes=[pltpu.VMEM((D_MODEL // LANES, tm, LANES), F32),
                        pltpu.VMEM((tm, D_MODEL), BF16)]
        + _weight_scratch(D_MODEL, EVEN_IN, slots=4),
        compiler_params=pltpu.CompilerParams(
            dimension_semantics=("arbitrary", "arbitrary"), vmem_limit_bytes=VMEM_LIMIT),
        name="even_in_proj",
    )(x2, gains, w_in, cos_t, sa_t, sb_t)


N_CLS = 16
CLS_LEN = SEQ // N_CLS


def _attn_masks():
    ch = ATT_CHUNK
    rho = np.arange(ch)
    tau = 16 * (rho % 8) + rho // 8
    kap = np.arange(2 * ch)
    kk = 16 * (kap % 16) + kap // 16
    p1_full = np.where(kk[None, :] < ch, kk[None, :] >= tau[:, None],
                       (kk[None, :] - ch) <= tau[:, None])
    kap1 = np.arange(ch)
    p1_first = (16 * (kap1 % 8) + kap1 // 8)[None, :] <= tau[:, None]
    a = 4 * (rho % 32) + rho // 32
    kap = np.arange(2 * ch)
    kk2 = 4 * (kap % 64) + kap // 64
    p2_full = (kk2[None, :] >= a[:, None]) & (kk2[None, :] <= a[:, None] + ch)
    kap1 = np.arange(ch)
    kk1 = 4 * (kap1 % 32) + kap1 // 32
    p2_first = kk1[None, :] <= a[:, None]
    p3 = np.arange(ch)[None, :] <= rho[:, None]
    valid = np.concatenate([p1_full, p1_first, p2_full, p2_first, p3], axis=1)
    return jnp.asarray(np.where(valid, 0.0, NEG_BIG), dtype=F32)


def _attn_kernel(q16_ref, k16_ref, v16_ref, mask_ref, o_ref, m_a, m_b, acc_a, acc_b, fin):
    ch = ATT_CHUNK
    per = ch // N_CLS
    lane = lax.broadcasted_iota(jnp.int32, (1, LANES), 1)
    is_a = lane < HEAD_DIM
    is_b = jnp.logical_not(is_a)
    state = ((m_a, acc_a), (m_b, acc_b))
    ones_row = jnp.ones((1, LANES), BF16)
    zeros_row = jnp.zeros((1, LANES), BF16)
    row = lax.broadcasted_iota(jnp.int32, (ch, ch), 0)
    col = lax.broadcasted_iota(jnp.int32, (ch, ch), 1)
    shuffle = jnp.where(col == per * (row % N_CLS) + row // N_CLS, 1.0, 0.0).astype(BF16)
    mask_p1_full = slice(0, 2 * ch)
    mask_p1_first = slice(2 * ch, 3 * ch)
    mask_p2_full = slice(3 * ch, 5 * ch)
    mask_p2_first = slice(5 * ch, 6 * ch)
    mask_p3 = slice(6 * ch, 7 * ch)

    def gather(ref, slabs):
        parts = [ref[s:s + n, :] for s, n in slabs]
        return parts[0] if len(parts) == 1 else jnp.concatenate(parts, axis=0)

    def gather_cls(ref, slabs):
        parts = [ref[0, s // CLS_LEN, s % CLS_LEN:s % CLS_LEN + n, :] for s, n in slabs]
        return parts[0] if len(parts) == 1 else jnp.concatenate(parts, axis=0)

    def scatter(ref, slabs, val):
        off = 0
        for s, n in slabs:
            ref[s:s + n, :] = val[off:off + n, :]
            off += n

    def phase_scores(items):
        scores = []
        for load_q, load_k, load_v, mask, slabs, has_state, finalize in items:
            qc = load_q()
            kc = load_k()
            for is_h in (is_a, is_b):
                qh = jnp.where(is_h, qc, zeros_row)
                s = lax.dot_general(qh, kc, (((1,), (1,)), ((), ())),
                                    preferred_element_type=F32) + mask_ref[:, mask]
                scores.append((s, jnp.max(s, axis=1, keepdims=True)))
        return scores

    def phase_update(items, scores):
        partial = []
        n = 0
        for load_q, load_k, load_v, mask, slabs, has_state, finalize in items:
            vc = load_v()
            for h, is_h in enumerate((is_a, is_b)):
                s, m_c = scores[n]
                n += 1
                if has_state:
                    m_prev = gather(state[h][0], slabs)
                    m_new = jnp.maximum(m_prev, m_c)
                else:
                    m_prev = None
                    m_new = jnp.broadcast_to(m_c, (ch, LANES))
                m_k = jnp.concatenate([m_new] * (s.shape[1] // LANES), axis=1)
                p = jnp.exp2(s - m_k)
                vh = jnp.where(is_h, vc, ones_row)
                pv = jnp.dot(p.astype(BF16), vh, preferred_element_type=F32)
                partial.append((m_prev, m_new, pv))
        n = 0
        for load_q, load_k, load_v, mask, slabs, has_state, finalize in items:
            res = []
            for h in range(2):
                m_prev, m_new, pv = partial[n]
                n += 1
                if has_state:
                    a_new = jnp.exp2(m_prev - m_new) * gather(state[h][1], slabs) + pv
                else:
                    a_new = pv
                res.append(a_new)
                if not finalize:
                    scatter(state[h][0], slabs, m_new)
                    scatter(state[h][1], slabs, a_new)
            if finalize:
                num = jnp.where(is_a, res[0], res[1])
                den = pltpu.roll(jnp.where(is_a, res[1], res[0]), HEAD_DIM, 1)
                scatter(fin, slabs, num / den)

    def block(slabs, kslabs, mask, has_state, finalize):
        return (lambda: gather_cls(q16_ref, slabs).astype(BF16),
                lambda: gather_cls(k16_ref, kslabs).astype(BF16),
                lambda: gather_cls(v16_ref, kslabs).astype(BF16), mask, slabs, has_state, finalize)

    order = []
    for i in range(SEQ // ch):
        slabs = [(r * CLS_LEN + per * i, per) for r in range(N_CLS)]
        if i > 0:
            kslabs = [(r * CLS_LEN + per * (i - 1), 2 * per) for r in range(N_CLS)]
            order.append(block(slabs, kslabs, mask_p1_full, False, False))
        else:
            order.append(block(slabs, slabs, mask_p1_first, False, False))

    quarter = ch // 4
    dil4 = []
    for r4 in range(4):
        for ic in range(SEQ // (4 * ch)):
            cls = [(4 * j + r4) * CLS_LEN for j in range(4)]
            slabs = [(c + quarter * ic, quarter) for c in cls]
            if ic > 0:
                kslabs = [(c + quarter * (ic - 1), 2 * quarter) for c in cls]
                dil4.append(block(slabs, kslabs, mask_p2_full, True, False))
            else:
                dil4.append(block(slabs, slabs, mask_p2_first, True, False))

    order += dil4

    for r in range(N_CLS):
        slabs = [(r * CLS_LEN, CLS_LEN)]
        order.append(block(slabs, slabs, mask_p3, True, True))

    assert (SEQ // ch) % ATT_GROUP == 0
    pending = None
    for i in range(0, len(order), ATT_GROUP):
        items = order[i:i + ATT_GROUP]
        scores = phase_scores(items)
        if pending is not None:
            phase_update(*pending)
        pending = (items, scores)
    phase_update(*pending)

    for i in range(SEQ // ch):
        slab = gather(fin, [(r * CLS_LEN + per * i, per) for r in range(N_CLS)])
        o_ref[0, i * ch:(i + 1) * ch, :] = jnp.dot(
            shuffle, slab.astype(BF16), preferred_element_type=F32).astype(o_ref.dtype)


def _attention(q16, k16, v16):
    blk = pl.BlockSpec((1, SEQ, LANES), lambda b, h: (b, 0, h))
    cls_blk = pl.BlockSpec((1, N_CLS, CLS_LEN, LANES), lambda b, h: (b, 0, 0, h))
    f32_scr = pltpu.VMEM((SEQ, LANES), F32)
    masks = _attn_masks()
    return pl.pallas_call(
        _attn_kernel,
        grid=(BATCH, ATT_WIDTH // LANES),
        in_specs=[cls_blk, cls_blk, cls_blk, _const_spec(masks.shape)],
        out_specs=blk,
        out_shape=jax.ShapeDtypeStruct((BATCH, SEQ, ATT_WIDTH), BF16),
        scratch_shapes=[f32_scr] * 5,
        compiler_params=pltpu.CompilerParams(
            dimension_semantics=("arbitrary", "arbitrary"), vmem_limit_bytes=VMEM_LIMIT),
        name="dilated_attention",
    )(q16, k16, v16, masks)


def _ssm_kernel(us_ref, perm_ref, lam_re_ref, lam_im_ref, wb_ref, wc_re_ref, wc_im_ref,
                d_ref, gw_ref, gb_ref, o_ref, xr, xi, st_re, st_im, y_tb, y_bt):
    tc = SSM_TC
    rows = tc * BATCH
    n_chunk = SSM_WIDTH // LANES
    cpc = N_STATE // n_chunk

    @pl.when(pl.program_id(0) == 0)
    def _():
        st_re[...] = jnp.zeros_like(st_re)
        st_im[...] = jnp.zeros_like(st_im)

    ub = us_ref[:, :, :SSM_WIDTH].reshape(rows, SSM_WIDTH)
    u_tb = jnp.dot(perm_ref[...], ub, preferred_element_type=F32).astype(BF16)
    def b_proj(c):
        cols = slice(c * cpc, (c + 1) * cpc)
        bu = jnp.dot(u_tb[:, c * LANES:(c + 1) * LANES], wb_ref[c], preferred_element_type=F32)
        xr[:, cols] = bu[:, :cpc]
        xi[:, cols] = bu[:, cpc:]

    def recurrence(c):
        cols = slice(c * cpc, (c + 1) * cpc)
        lr = lam_re_ref[:, cols]
        li = lam_im_ref[:, cols]
        sr = st_re[:, cols]
        si = st_im[:, cols]
        for t in range(tc):
            trow = slice(t * BATCH, (t + 1) * BATCH)
            sr, si = (lr * sr - li * si + xr[trow, cols], lr * si + li * sr + xi[trow, cols])
            xr[trow, cols] = sr
            xi[trow, cols] = si
        st_re[:, cols] = sr
        st_im[:, cols] = si

    def c_proj(c):
        cols = slice(c * cpc, (c + 1) * cpc)
        y_tb[c] = (
            jnp.dot(xr[:, cols].astype(BF16), wc_re_ref[c], preferred_element_type=F32)
            + jnp.dot(xi[:, cols].astype(BF16), wc_im_ref[c], preferred_element_type=F32))

    def skip_gelu(c):
        lanes = slice(c * LANES, (c + 1) * LANES)
        y_c = jnp.concatenate(
            [y_tb[c, pl.ds(b, tc, stride=BATCH), :] for b in range(BATCH)], axis=0)
        y_bt[:, lanes] = jax.nn.gelu(y_c + d_ref[:, lanes] * ub[:, lanes].astype(F32))

    b_proj(0)
    gate = _silu(us_ref[:, :, SSM_WIDTH:].reshape(rows, SSM_WIDTH).astype(F32))
    for c in range(n_chunk):
        if c + 1 < n_chunk:
            b_proj(c + 1)
        recurrence(c)
        if c > 0:
            skip_gelu(c - 1)
        c_proj(c)
    skip_gelu(n_chunk - 1)
    y = y_bt[...]
    z = jnp.dot(y.astype(BF16), gw_ref[...], preferred_element_type=F32) + gb_ref[...]
    out = y * _sigmoid(z) * gate
    o_ref[...] = out.reshape(BATCH, tc, SSM_WIDTH).astype(o_ref.dtype)


def _ssm(us3, params, i):
    tc = SSM_TC
    rows = tc * BATCH
    dst_row = np.arange(rows)
    perm = jnp.asarray((dst_row[:, None] % BATCH) * tc + dst_row[:, None] // BATCH
                       == dst_row[None, :], dtype=BF16)
    return pl.pallas_call(
        _ssm_kernel,
        grid=(SEQ // tc,),
        in_specs=[pl.BlockSpec((BATCH, tc, 2 * SSM_WIDTH), lambda t: (0, t, 0)),
                  _const_spec((rows, rows))] + [_layer_spec(p, i) for p in params],
        out_specs=pl.BlockSpec((BATCH, tc, SSM_WIDTH), lambda t: (0, t, 0)),
        out_shape=jax.ShapeDtypeStruct((BATCH, SEQ, SSM_WIDTH), BF16),
        scratch_shapes=[pltpu.VMEM((rows, N_STATE), F32), pltpu.VMEM((rows, N_STATE), F32),
                        pltpu.VMEM((BATCH, N_STATE), F32), pltpu.VMEM((BATCH, N_STATE), F32),
                        pltpu.VMEM((SSM_WIDTH // LANES, rows, LANES), F32),
                        pltpu.VMEM((rows, SSM_WIDTH), F32)],
        compiler_params=pltpu.CompilerParams(
            dimension_semantics=("arbitrary",), vmem_limit_bytes=VMEM_LIMIT),
        name="s5_branch",
    )(us3, perm, *params)


def _tail_kernel(layer_idx, x_ref, att_ref, g_ref, ssm_ref, w_eo_hbm, post_e_ref,
                 pre_ref, w_in_hbm, pw_hbm, ps_ref, w_out_hbm, post_ref, o_ref,
                 ext, tmp0, tmp1,
                 w_eo_ref, stage_eo, sem_eo, w_in_ref, stage_in, sem_in,
                 pw_ref, stage_pw, sem_pw, w_out_ref, stage_out, sem_out):
    tm = ODD_TILE
    hist = POOL_HIST
    total = hist + tm
    j = pl.program_id(1)

    @pl.when(_first_step())
    def _():
        _load_weights_bf16(layer_idx, [(w_eo_hbm, w_eo_ref, stage_eo, sem_eo),
                                       (w_in_hbm, w_in_ref, stage_in, sem_in),
                                       (pw_hbm, pw_ref, stage_pw, sem_pw),
                                       (w_out_hbm, w_out_ref, stage_out, sem_out)])

    @pl.when(j == 0)
    def _():
        ext[0:hist, :] = jnp.zeros((hist, POOL_WIDTH), F32)

    tmps = (tmp0, tmp1)
    sub = ODD_SUB
    for s in range(tm // sub):
        lo = hist + s * sub
        hi = lo + sub
        rows = slice(s * sub, (s + 1) * sub)
        half = sub // 2
        y_even = []
        for h in range(2):
            r = slice(s * sub + h * half, s * sub + (h + 1) * half)
            gated = (att_ref[r, :].astype(F32) * _silu(g_ref[r, :].astype(F32))).astype(BF16)
            y_even.append(
                jnp.dot(gated, w_eo_ref[:ATT_WIDTH, :], preferred_element_type=F32)
                + jnp.dot(ssm_ref[r, :], w_eo_ref[ATT_WIDTH:, :], preferred_element_type=F32))
        x_parts, hb_parts = [], []
        for h in range(2):
            r = slice(s * sub + h * half, s * sub + (h + 1) * half)
            x_h = x_ref[r, :] + _rms(y_even[h], post_e_ref[...])
            hb_h = _rms(x_h, pre_ref[...]).astype(BF16)
            ext[lo + h * half:lo + (h + 1) * half, :] = jnp.dot(
                hb_h, w_in_ref[:, :POOL_WIDTH], preferred_element_type=F32)
            x_parts.append(x_h)
            hb_parts.append(hb_h)
        x = jnp.concatenate(x_parts, axis=0)
        hb = jnp.concatenate(hb_parts, axis=0)
        t_idx = j * tm + s * sub + lax.broadcasted_iota(jnp.int32, (sub, 1), 0)
        inv_count = 1.0 / (t_idx + 1).astype(F32)
        y = None
        for g, w in enumerate(POOL_WINDOWS):
            cols = slice(g * POOL_GROUP, (g + 1) * POOL_GROUP)
            levels = int(math.log2(w))
            src_ref, src_cols = ext, cols
            for lev in range(levels):
                shift = 2 ** lev
                start = lo - 8 * (levels - 1 - lev)
                val = (src_ref[start:hi, src_cols] + src_ref[start - shift:hi - shift, src_cols])
                if lev < levels - 1:
                    dst = tmps[lev % 2]
                    dst[start:hi, :] = val
                    src_ref, src_cols = dst, slice(None)
            u_g = ext[lo:hi, cols]
            mixed = val * jnp.maximum(inv_count, 1.0 / w) - u_g
            yg = jnp.dot(mixed.astype(BF16), pw_ref[cols, :], preferred_element_type=F32)
            gate = jnp.dot(hb, w_in_ref[:, POOL_WIDTH + g * POOL_GROUP:
                                        POOL_WIDTH + (g + 1) * POOL_GROUP],
                           preferred_element_type=F32)
            yg = yg * ps_ref[:, cols] * _silu(gate)
            part = jnp.dot(yg.astype(BF16), w_out_ref[cols, :], preferred_element_type=F32)
            y = part if y is None else y + part
        o_ref[rows, :] = x + _rms(y, post_ref[...])
    ext[0:hist, :] = ext[tm:total, :]


def _even_tail_odd(x2, att2, g2, ssm2, w_even_out, pre, post, w_in, pool_w, pool_scale, w_out, i,
                   layer):
    tm = ODD_TILE
    nj = SEQ // tm
    row_spec = pl.BlockSpec((tm, D_MODEL), lambda b, j: (b * nj + j, 0))
    att_spec = pl.BlockSpec((tm, ATT_WIDTH), lambda b, j: (b * nj + j, 0))
    hbm = pl.BlockSpec(memory_space=pl.ANY)
    return pl.pallas_call(
        functools.partial(_tail_kernel, i),
        grid=(BATCH, nj),
        in_specs=[row_spec, att_spec, att_spec,
                  pl.BlockSpec((tm, SSM_WIDTH), lambda b, j: (b * nj + j, 0)),
                  hbm, _layer_spec(post, layer),
                  _layer_spec(pre, layer + 1), hbm,
                  hbm, _layer_spec(pool_scale, i),
                  hbm, _layer_spec(post, layer + 1)],
        out_specs=row_spec,
        out_shape=jax.ShapeDtypeStruct((BATCH * SEQ, D_MODEL), F32),
        scratch_shapes=[pltpu.VMEM((POOL_HIST + tm, POOL_WIDTH), F32),
                        pltpu.VMEM((POOL_HIST + tm, POOL_GROUP), F32),
                        pltpu.VMEM((POOL_HIST + tm, POOL_GROUP), F32)]
        + _weight_scratch(*w_even_out.shape[1:]) + _weight_scratch(*w_in.shape[1:])
        + _weight_scratch(*pool_w.shape[1:]) + _weight_scratch(*w_out.shape[1:]),
        compiler_params=pltpu.CompilerParams(
            dimension_semantics=("arbitrary", "arbitrary"), vmem_limit_bytes=VMEM_LIMIT),
        name="even_tail_odd_layer",
    )(x2, att2, g2, ssm2, w_even_out, post, pre, w_in, pool_w, pool_scale, w_out, post)


def _rope_tables():
    half = ROT_DIM // 2
    inv_freq = (np.float32(ROPE_THETA)
                ** (-np.arange(0, ROT_DIM, 2, dtype=np.float32) / np.float32(ROT_DIM)))
    ang = np.arange(SEQ, dtype=np.float32)[:, None] * inv_freq[None, :].astype(np.float32)
    cos, sin = np.cos(ang).astype(np.float32), np.sin(ang).astype(np.float32)
    zeros = np.zeros((SEQ, HEAD_DIM - ROT_DIM), np.float32)
    zeros_h = np.zeros((SEQ, half), np.float32)
    cos_h = np.concatenate([cos, cos, np.ones_like(zeros)], axis=1)
    sa_h = np.concatenate([-sin, zeros_h, zeros], axis=1)
    sb_h = np.concatenate([zeros_h, sin, zeros], axis=1)
    rep = LANES // HEAD_DIM
    pos = np.arange(SEQ).reshape(SEQ // ROW_TILE, ROW_TILE // N_CLS, N_CLS)
    order = pos.transpose(0, 2, 1).reshape(SEQ)
    return tuple(jnp.asarray(np.tile(t, (1, rep))[order]) for t in (cos_h, sa_h, sb_h))


def _ssm_params(a_re, a_im, log_dt, b_re, b_im, c_re, c_im):
    n_layer = a_re.shape[0]
    lam = lax.complex(a_re.astype(F32), a_im.astype(F32))
    dt = jnp.exp(log_dt.astype(F32))[..., None]
    lam_bar = jnp.exp(lam * dt)
    b_bar = ((lam_bar - 1.0) / lam)[..., None] * lax.complex(b_re.astype(F32), b_im.astype(F32))
    lam_re = jnp.broadcast_to(jnp.real(lam_bar).reshape(n_layer, 1, N_STATE),
                              (n_layer, BATCH, N_STATE))
    lam_im = jnp.broadcast_to(jnp.imag(lam_bar).reshape(n_layer, 1, N_STATE),
                              (n_layer, BATCH, N_STATE))
    gpc = LANES // SSM_GROUP
    n_chunk = SSM_GROUPS // gpc
    eye = jnp.eye(gpc, dtype=F32)

    def block_diag_in(m):
        m = m.reshape(n_layer, n_chunk, gpc, SSM_STATE, SSM_GROUP)
        return jnp.einsum('lcgph,gk->lcghkp', m, eye).reshape(
            n_layer, n_chunk, gpc * SSM_GROUP, gpc * SSM_STATE)

    def block_diag_out(m):
        m = m.reshape(n_layer, n_chunk, gpc, SSM_GROUP, SSM_STATE)
        return jnp.einsum('lcghp,gk->lcgpkh', m, eye).reshape(
            n_layer, n_chunk, gpc * SSM_STATE, gpc * SSM_GROUP)

    wb = jnp.concatenate([block_diag_in(jnp.real(b_bar)), block_diag_in(jnp.imag(b_bar))],
                         axis=3).astype(BF16)
    wc_re = block_diag_out(c_re.astype(F32)).astype(BF16)
    wc_im = block_diag_out(-c_im.astype(F32)).astype(BF16)
    return lam_re, lam_im, wb, wc_re, wc_im


def kernel(x, pre_norm, post_norm, even_w_in, even_w_out, ssm_a_re, ssm_a_im, ssm_log_dt,
           ssm_b_re, ssm_b_im, ssm_c_re, ssm_c_im, ssm_d, ssm_glu_w, ssm_glu_b,
           odd_w_in, pool_w, pool_scale, odd_w_out):
    cos_t, sa_t, sb_t = _rope_tables()
    pre = pre_norm.astype(F32).reshape(DEPTH, 1, D_MODEL)
    post = post_norm.astype(F32).reshape(DEPTH, 1, D_MODEL)
    even_w_in_f = even_w_in.astype(F32)
    even_w_out_f = even_w_out.astype(F32)
    odd_w_in_f = odd_w_in.astype(F32)
    odd_w_out_f = odd_w_out.astype(F32)
    pool_w_f = pool_w.astype(F32).reshape(-1, POOL_WIDTH, POOL_GROUP)
    pool_scale3 = pool_scale.astype(F32).reshape(-1, 1, POOL_WIDTH)
    ssm_params = _ssm_params(ssm_a_re, ssm_a_im, ssm_log_dt, ssm_b_re, ssm_b_im,
                             ssm_c_re, ssm_c_im) + (
        ssm_d.astype(F32).reshape(-1, 1, SSM_WIDTH), ssm_glu_w.astype(BF16),
        ssm_glu_b.astype(F32).reshape(-1, 1, SSM_WIDTH))
    x2 = x.reshape(BATCH * SEQ, D_MODEL)
    assert DEPTH % 2 == 0
    for layer in range(0, DEPTH, 2):
        i = layer // 2
        q16, k16, v16, g, us = _even_in(x2, pre, layer, even_w_in_f, i, cos_t, sa_t, sb_t)
        att = _attention(q16, k16, v16)
        ssm = _ssm(us.reshape(BATCH, SEQ, 2 * SSM_WIDTH), ssm_params, i)
        x2 = _even_tail_odd(x2, att.reshape(BATCH * SEQ, ATT_WIDTH), g,
                            ssm.reshape(BATCH * SEQ, SSM_WIDTH), even_w_out_f, pre, post,
                            odd_w_in_f, pool_w_f, pool_scale3, odd_w_out_f, i, layer)
    return x2.reshape(BATCH, SEQ, D_MODEL)
```

```python
import functools
import math

import jax
import jax.numpy as jnp
import numpy as np
from jax import lax
from jax.experimental import pallas as pl
from jax.experimental.pallas import tpu as pltpu

F32 = jnp.float32
BF16 = jnp.bfloat16

D_MODEL = 1024
BATCH = 8
SEQ = 2048
DEPTH = 4
HEAD_DIM = 64
ATT_WIDTH = 1024
ROT_DIM = 16
ROPE_THETA = 500000.0
SSM_WIDTH = 512
SSM_GROUP = 16
SSM_GROUPS = 32
SSM_STATE = 64
N_STATE = SSM_GROUPS * SSM_STATE
POOL_WIDTH = 2048
POOL_WINDOWS = (2, 4, 8, 16)
POOL_GROUP = 512
EVEN_IN = 4 * ATT_WIDTH + 2 * SSM_WIDTH
RMS_EPS = 1e-6
LOG2_E = math.log2(math.e)

LANES = 128
ATT_CHUNK = 128
ATT_GROUP = 4
ATT_PAIRS = 2
NEG_BIG = -1e30
VMEM_LIMIT = 56 * 1024 * 1024
WEIGHT_CHUNK_BYTES = 3 << 19
BF16_SUBLANES = 16

ROW_TILE = 512
SSM_TC = 64
ODD_TILE = 512
ODD_SUB = 512
POOL_HIST = 32


def _sigmoid(x):
    return 0.5 + 0.5 * jnp.tanh(0.5 * x)


def _silu(x):
    h = 0.5 * x
    return h + h * jnp.tanh(h)


def _rms(x, gain):
    return x * lax.rsqrt(jnp.mean(x * x, axis=-1, keepdims=True) + RMS_EPS) * gain


def _const_spec(shape):
    nd = len(shape)
    return pl.BlockSpec(shape, lambda *_: (0,) * nd, pipeline_mode=pl.Buffered(1))


def _load_weights_bf16(layer, weights):
    def copy(w_hbm, stage, sem, k):
        slots, chunk = stage.shape[0], stage.shape[1]
        return pltpu.make_async_copy(
            w_hbm.at[layer, pl.ds(k * chunk, chunk), :], stage.at[k % slots], sem.at[k % slots])

    n_chunks = [dst.shape[0] // stage.shape[1] for _, dst, stage, _ in weights]
    for (w_hbm, dst, stage, sem), n in zip(weights, n_chunks):
        for k in range(min(stage.shape[0], n)):
            copy(w_hbm, stage, sem, k).start()
    for k in range(max(n_chunks)):
        for (w_hbm, dst, stage, sem), n in zip(weights, n_chunks):
            if k >= n:
                continue
            slots, chunk = stage.shape[0], stage.shape[1]
            copy(w_hbm, stage, sem, k).wait()
            dst[k * chunk:(k + 1) * chunk, :] = stage[k % slots].astype(BF16)
            if k + slots < n:
                copy(w_hbm, stage, sem, k + slots).start()


def _weight_scratch(rows, cols, slots=2):
    chunk = rows
    while chunk * cols * 4 > WEIGHT_CHUNK_BYTES:
        chunk //= 2
    assert rows % chunk == 0 and chunk % BF16_SUBLANES == 0
    return [pltpu.VMEM((rows, cols), BF16), pltpu.VMEM((slots, chunk, cols), F32),
            pltpu.SemaphoreType.DMA((slots,))]


def _first_step():
    return jnp.logical_and(pl.program_id(0) == 0, pl.program_id(1) == 0)


def _layer_spec(stacked, layer):
    shape = stacked.shape[1:]
    nd = len(shape)
    return pl.BlockSpec((None,) + tuple(shape), lambda *_: (layer,) + (0,) * nd,
                        pipeline_mode=pl.Buffered(1))


def _even_in_kernel(layer_idx, x_ref, gain_ref, w_hbm, cos_ref, sa_ref, sb_ref,
                    q16_ref, k16_ref, v16_ref, g_ref, us_ref, h_nat, h_cls, w_ref, stage, sem):
    tm = ROW_TILE
    per_cls = tm // N_CLS

    @pl.when(_first_step())
    def _():
        _load_weights_bf16(layer_idx, [(w_hbm, w_ref, stage, sem)])

    def proj(lhs, lo, width):
        return jnp.dot(lhs, w_ref[:, lo:lo + width], preferred_element_type=F32)

    half = tm // 2
    for hh in range(2):
        r = slice(hh * half, (hh + 1) * half)
        h = _rms(x_ref[r, :], gain_ref[...])
        for c in range(D_MODEL // LANES):
            h_nat[c, r, :] = h[:, c * LANES:(c + 1) * LANES]
        hb = h.astype(BF16)
        g_ref[r, :] = proj(hb, 3 * ATT_WIDTH, ATT_WIDTH).astype(g_ref.dtype)
        us_ref[r, :] = proj(hb, 4 * ATT_WIDTH, 2 * SSM_WIDTH).astype(us_ref.dtype)
    for c in range(D_MODEL // LANES):
        for cls in range(N_CLS):
            h_cls[cls * per_cls:(cls + 1) * per_cls, c * LANES:(c + 1) * LANES] = (
                h_nat[c, pl.ds(cls, per_cls, stride=N_CLS), :].astype(BF16))
    hb_cls = h_cls[...]
    cos = cos_ref[...]
    sa = sa_ref[...]
    sb = sb_ref[...]

    def rope_store(dst, y, scale):
        for c in range(ATT_WIDTH // LANES):
            yc = y[:, c * LANES:(c + 1) * LANES]
            r = (yc * cos + pltpu.roll(yc, LANES - ROT_DIM // 2, 1) * sa
                 + pltpu.roll(yc, ROT_DIM // 2, 1) * sb)
            dst[0, :, :, c * LANES:(c + 1) * LANES] = (r * scale).reshape(N_CLS, per_cls, LANES)

    rope_store(q16_ref, proj(hb_cls, 0, ATT_WIDTH), HEAD_DIM ** -0.5 * LOG2_E)
    rope_store(k16_ref, proj(hb_cls, ATT_WIDTH, ATT_WIDTH), 1.0)
    v16_ref[0] = proj(hb_cls, 2 * ATT_WIDTH, ATT_WIDTH).reshape(N_CLS, per_cls, ATT_WIDTH)


def _even_in(x2, gains, layer, w_in, i, cos_t, sa_t, sb_t):
    tm = ROW_TILE
    nj = SEQ // tm
    row_spec = pl.BlockSpec((tm, D_MODEL), lambda b, j: (b * nj + j, 0))
    tab_spec = pl.BlockSpec((tm, LANES), lambda b, j: (j, 0))
    att_spec = pl.BlockSpec((tm, ATT_WIDTH), lambda b, j: (b * nj + j, 0))
    cls_spec = pl.BlockSpec((1, N_CLS, tm // N_CLS, ATT_WIDTH), lambda b, j: (b, 0, j, 0))
    att_shape = jax.ShapeDtypeStruct((BATCH * SEQ, ATT_WIDTH), BF16)
    cls_shape = jax.ShapeDtypeStruct((BATCH, N_CLS, CLS_LEN, ATT_WIDTH), F32)
    return pl.pallas_call(
        functools.partial(_even_in_kernel, i),
        grid=(BATCH, nj),
        in_specs=[row_spec, _layer_spec(gains, layer), pl.BlockSpec(memory_space=pl.ANY),
                  tab_spec, tab_spec, tab_spec],
        out_specs=[cls_spec, cls_spec, cls_spec, att_spec, att_spec],
        out_shape=[cls_shape, cls_shape, cls_shape, att_shape, att_shape],
        scratch_shapes=[pltpu.VMEM((D_MODEL // LANES, tm, LANES), F32),
                        pltpu.VMEM((tm, D_MODEL), BF16)]
        + _weight_scratch(D_MODEL, EVEN_IN, slots=4),
        compiler_params=pltpu.CompilerParams(
            dimension_semantics=("arbitrary", "arbitrary"), vmem_limit_bytes=VMEM_LIMIT),
        name="even_in_proj",
    )(x2, gains, w_in, cos_t, sa_t, sb_t)


N_CLS = 16
CLS_LEN = SEQ // N_CLS


def _attn_masks():
    ch = ATT_CHUNK
    per = ch // N_CLS
    quarter = ch // 4
    row = np.arange(ch)
    col1 = np.arange(ch)
    col2 = np.arange(2 * ch)
    tau = N_CLS * (row % per) + row // per
    kk = N_CLS * (col2 % (2 * per)) + col2 // (2 * per)
    p1_full = np.where(kk[None, :] < ch, kk[None, :] >= tau[:, None],
                       (kk[None, :] - ch) <= tau[:, None])
    p1_first = (N_CLS * (col1 % per) + col1 // per)[None, :] <= tau[:, None]
    off = 4 * (row % quarter) + row // quarter
    kk2 = 4 * (col2 % (2 * quarter)) + col2 // (2 * quarter)
    p2_full = (kk2[None, :] >= off[:, None]) & (kk2[None, :] <= off[:, None] + ch)
    p2_first = (4 * (col1 % quarter) + col1 // quarter)[None, :] <= off[:, None]
    p3 = col1[None, :] <= row[:, None]
    valid = np.concatenate([p1_full, p1_first, p2_full, p2_first, p3], axis=1)
    return jnp.asarray(np.where(valid, 0.0, NEG_BIG), dtype=F32)


def _attn_kernel(q16_ref, k16_ref, v16_ref, mask_ref, o_ref, *scratch):
    per_pair = len(scratch) // ATT_PAIRS
    for pair in range(ATT_PAIRS):
        _attn_head_pair(q16_ref, k16_ref, v16_ref, mask_ref, o_ref,
                        slice(pair * LANES, (pair + 1) * LANES),
                        *scratch[pair * per_pair:(pair + 1) * per_pair])


def _attn_head_pair(q16_ref, k16_ref, v16_ref, mask_ref, o_ref, lanes,
                    m_a, m_b, acc_a, acc_b, fin):
    ch = ATT_CHUNK
    per = ch // N_CLS
    lane = lax.broadcasted_iota(jnp.int32, (1, LANES), 1)
    is_a = lane < HEAD_DIM
    is_b = jnp.logical_not(is_a)
    state = ((m_a, acc_a), (m_b, acc_b))
    ones_row = jnp.ones((1, LANES), BF16)
    zeros_row = jnp.zeros((1, LANES), BF16)
    row = lax.broadcasted_iota(jnp.int32, (ch, ch), 0)
    col = lax.broadcasted_iota(jnp.int32, (ch, ch), 1)
    shuffle = jnp.where(col == per * (row % N_CLS) + row // N_CLS, 1.0, 0.0).astype(BF16)
    mask_p1_full = slice(0, 2 * ch)
    mask_p1_first = slice(2 * ch, 3 * ch)
    mask_p2_full = slice(3 * ch, 5 * ch)
    mask_p2_first = slice(5 * ch, 6 * ch)
    mask_p3 = slice(6 * ch, 7 * ch)

    def gather(ref, slabs):
        parts = [ref[s:s + n, :] for s, n in slabs]
        return parts[0] if len(parts) == 1 else jnp.concatenate(parts, axis=0)

    def gather_cls(ref, slabs):
        parts = [ref[0, s // CLS_LEN, s % CLS_LEN:s % CLS_LEN + n, lanes] for s, n in slabs]
        return parts[0] if len(parts) == 1 else jnp.concatenate(parts, axis=0)

    def scatter(ref, slabs, val):
        off = 0
        for s, n in slabs:
            ref[s:s + n, :] = val[off:off + n, :]
            off += n

    def phase_scores(items):
        scores = []
        for load_q, load_k, load_v, mask, slabs, has_state, finalize in items:
            qc = load_q()
            kc = load_k()
            for is_h in (is_a, is_b):
                qh = jnp.where(is_h, qc, zeros_row)
                s = lax.dot_general(qh, kc, (((1,), (1,)), ((), ())),
                                    preferred_element_type=F32) + mask_ref[:, mask]
                scores.append((s, jnp.max(s, axis=1, keepdims=True)))
        return scores

    def phase_update(items, scores):
        partial = []
        n = 0
        for load_q, load_k, load_v, mask, slabs, has_state, finalize in items:
            vc = load_v()
            for h, is_h in enumerate((is_a, is_b)):
                s, m_c = scores[n]
                n += 1
                if has_state:
                    m_prev = gather(state[h][0], slabs)
                    m_new = jnp.maximum(m_prev, m_c)
                else:
                    m_prev = None
                    m_new = jnp.broadcast_to(m_c, (ch, LANES))
                m_k = jnp.concatenate([m_new] * (s.shape[1] // LANES), axis=1)
                p = jnp.exp2(s - m_k)
                vh = jnp.where(is_h, vc, ones_row)
                pv = jnp.dot(p.astype(BF16), vh, preferred_element_type=F32)
                partial.append((m_prev, m_new, pv))
        n = 0
        for load_q, load_k, load_v, mask, slabs, has_state, finalize in items:
            res = []
            for h in range(2):
                m_prev, m_new, pv = partial[n]
                n += 1
                if has_state:
                    a_new = jnp.exp2(m_prev - m_new) * gather(state[h][1], slabs) + pv
                else:
                    a_new = pv
                res.append(a_new)
                if not finalize:
                    scatter(state[h][0], slabs, m_new)
                    scatter(state[h][1], slabs, a_new)
            if finalize:
                num = jnp.where(is_a, res[0], res[1])
                den = pltpu.roll(jnp.where(is_a, res[1], res[0]), HEAD_DIM, 1)
                scatter(fin, slabs, num / den)

    def block(slabs, kslabs, mask, has_state, finalize):
        return (lambda: gather_cls(q16_ref, slabs).astype(BF16),
                lambda: gather_cls(k16_ref, kslabs).astype(BF16),
                lambda: gather_cls(v16_ref, kslabs).astype(BF16), mask, slabs, has_state, finalize)

    order = []
    for i in range(SEQ // ch):
        slabs = [(r * CLS_LEN + per * i, per) for r in range(N_CLS)]
        if i > 0:
            kslabs = [(r * CLS_LEN + per * (i - 1), 2 * per) for r in range(N_CLS)]
            order.append(block(slabs, kslabs, mask_p1_full, False, False))
        else:
            order.append(block(slabs, slabs, mask_p1_first, False, False))

    quarter = ch // 4
    dil4 = []
    for r4 in range(4):
        for ic in range(SEQ // (4 * ch)):
            cls = [(4 * j + r4) * CLS_LEN for j in range(4)]
            slabs = [(c + quarter * ic, quarter) for c in cls]
            if ic > 0:
                kslabs = [(c + quarter * (ic - 1), 2 * quarter) for c in cls]
                dil4.append(block(slabs, kslabs, mask_p2_full, True, False))
            else:
                dil4.append(block(slabs, slabs, mask_p2_first, True, False))

    order += dil4

    for r in range(N_CLS):
        slabs = [(r * CLS_LEN, CLS_LEN)]
        order.append(block(slabs, slabs, mask_p3, True, True))

    assert (SEQ // ch) % ATT_GROUP == 0
    pending = None
    for i in range(0, len(order), ATT_GROUP):
        items = order[i:i + ATT_GROUP]
        scores = phase_scores(items)
        if pending is not None:
            phase_update(*pending)
        pending = (items, scores)
    phase_update(*pending)

    for i in range(SEQ // ch):
        slab = gather(fin, [(r * CLS_LEN + per * i, per) for r in range(N_CLS)])
        o_ref[0, i * ch:(i + 1) * ch, lanes] = jnp.dot(
            shuffle, slab.astype(BF16), preferred_element_type=F32).astype(o_ref.dtype)


def _attention(q16, k16, v16):
    width = ATT_PAIRS * LANES
    blk = pl.BlockSpec((1, SEQ, width), lambda b, h: (b, 0, h))
    cls_blk = pl.BlockSpec((1, N_CLS, CLS_LEN, width), lambda b, h: (b, 0, 0, h))
    f32_scr = pltpu.VMEM((SEQ, LANES), F32)
    masks = _attn_masks()
    return pl.pallas_call(
        _attn_kernel,
        grid=(BATCH, ATT_WIDTH // width),
        in_specs=[cls_blk, cls_blk, cls_blk, _const_spec(masks.shape)],
        out_specs=blk,
        out_shape=jax.ShapeDtypeStruct((BATCH, SEQ, ATT_WIDTH), BF16),
        scratch_shapes=[f32_scr] * (5 * ATT_PAIRS),
        compiler_params=pltpu.CompilerParams(
            dimension_semantics=("arbitrary", "arbitrary"), vmem_limit_bytes=VMEM_LIMIT),
        name="dilated_attention",
    )(q16, k16, v16, masks)


def _ssm_kernel(us_ref, perm_ref, lam_re_ref, lam_im_ref, wb_ref, wc_re_ref, wc_im_ref,
                d_ref, gw_ref, gb_ref, o_ref, xr, xi, st_re, st_im, y_tb, y_bt):
    tc = SSM_TC
    rows = tc * BATCH
    n_chunk = SSM_WIDTH // LANES
    cpc = N_STATE // n_chunk

    @pl.when(pl.program_id(0) == 0)
    def _():
        st_re[...] = jnp.zeros_like(st_re)
        st_im[...] = jnp.zeros_like(st_im)

    ub = us_ref[:, :, :SSM_WIDTH].reshape(rows, SSM_WIDTH)
    u_tb = jnp.dot(perm_ref[...], ub, preferred_element_type=F32).astype(BF16)
    def b_proj(c):
        cols = slice(c * cpc, (c + 1) * cpc)
        bu = jnp.dot(u_tb[:, c * LANES:(c + 1) * LANES], wb_ref[c], preferred_element_type=F32)
        xr[:, cols] = bu[:, :cpc]
        xi[:, cols] = bu[:, cpc:]

    def recurrence(c):
        cols = slice(c * cpc, (c + 1) * cpc)
        lr = lam_re_ref[:, cols]
        li = lam_im_ref[:, cols]
        sr = st_re[:, cols]
        si = st_im[:, cols]
        for t in range(tc):
            trow = slice(t * BATCH, (t + 1) * BATCH)
            sr, si = (lr * sr - li * si + xr[trow, cols], lr * si + li * sr + xi[trow, cols])
            xr[trow, cols] = sr
            xi[trow, cols] = si
        st_re[:, cols] = sr
        st_im[:, cols] = si

    def c_proj(c):
        cols = slice(c * cpc, (c + 1) * cpc)
        y_tb[c] = (
            jnp.dot(xr[:, cols].astype(BF16), wc_re_ref[c], preferred_element_type=F32)
            + jnp.dot(xi[:, cols].astype(BF16), wc_im_ref[c], preferred_element_type=F32))

    def skip_gelu(c):
        lanes = slice(c * LANES, (c + 1) * LANES)
        y_c = jnp.concatenate(
            [y_tb[c, pl.ds(b, tc, stride=BATCH), :] for b in range(BATCH)], axis=0)
        y_bt[:, lanes] = jax.nn.gelu(y_c + d_ref[:, lanes] * ub[:, lanes].astype(F32))

    b_proj(0)
    gate = _silu(us_ref[:, :, SSM_WIDTH:].reshape(rows, SSM_WIDTH).astype(F32))
    for c in range(n_chunk):
        if c + 1 < n_chunk:
            b_proj(c + 1)
        recurrence(c)
        if c > 0:
            skip_gelu(c - 1)
        c_proj(c)
    skip_gelu(n_chunk - 1)
    y = y_bt[...]
    z = jnp.dot(y.astype(BF16), gw_ref[...], preferred_element_type=F32) + gb_ref[...]
    out = y * _sigmoid(z) * gate
    o_ref[...] = out.reshape(BATCH, tc, SSM_WIDTH).astype(o_ref.dtype)


def _ssm(us3, params, i):
    tc = SSM_TC
    rows = tc * BATCH
    dst_row = np.arange(rows)
    perm = jnp.asarray((dst_row[:, None] % BATCH) * tc + dst_row[:, None] // BATCH
                       == dst_row[None, :], dtype=BF16)
    return pl.pallas_call(
        _ssm_kernel,
        grid=(SEQ // tc,),
        in_specs=[pl.BlockSpec((BATCH, tc, 2 * SSM_WIDTH), lambda t: (0, t, 0)),
                  _const_spec((rows, rows))] + [_layer_spec(p, i) for p in params],
        out_specs=pl.BlockSpec((BATCH, tc, SSM_WIDTH), lambda t: (0, t, 0)),
        out_shape=jax.ShapeDtypeStruct((BATCH, SEQ, SSM_WIDTH), BF16),
        scratch_shapes=[pltpu.VMEM((rows, N_STATE), F32), pltpu.VMEM((rows, N_STATE), F32),
                        pltpu.VMEM((BATCH, N_STATE), F32), pltpu.VMEM((BATCH, N_STATE), F32),
                        pltpu.VMEM((SSM_WIDTH // LANES, rows, LANES), F32),
                        pltpu.VMEM((rows, SSM_WIDTH), F32)],
        compiler_params=pltpu.CompilerParams(
            dimension_semantics=("arbitrary",), vmem_limit_bytes=VMEM_LIMIT),
        name="s5_branch",
    )(us3, perm, *params)


def _tail_kernel(layer_idx, x_ref, att_ref, g_ref, ssm_ref, w_eo_hbm, post_e_ref,
                 pre_ref, w_in_hbm, pw_hbm, ps_ref, w_out_hbm, post_ref, o_ref,
                 ext, tmp0, tmp1,
                 w_eo_ref, stage_eo, sem_eo, w_in_ref, stage_in, sem_in,
                 pw_ref, stage_pw, sem_pw, w_out_ref, stage_out, sem_out):
    tm = ODD_TILE
    hist = POOL_HIST
    total = hist + tm
    j = pl.program_id(1)

    @pl.when(_first_step())
    def _():
        _load_weights_bf16(layer_idx, [(w_eo_hbm, w_eo_ref, stage_eo, sem_eo),
                                       (w_in_hbm, w_in_ref, stage_in, sem_in),
                                       (pw_hbm, pw_ref, stage_pw, sem_pw),
                                       (w_out_hbm, w_out_ref, stage_out, sem_out)])

    @pl.when(j == 0)
    def _():
        ext[0:hist, :] = jnp.zeros((hist, POOL_WIDTH), F32)

    tmps = (tmp0, tmp1)
    sub = ODD_SUB
    for s in range(tm // sub):
        lo = hist + s * sub
        hi = lo + sub
        rows = slice(s * sub, (s + 1) * sub)
        half = sub // 2
        y_even = []
        for h in range(2):
            r = slice(s * sub + h * half, s * sub + (h + 1) * half)
            gated = (att_ref[r, :].astype(F32) * _silu(g_ref[r, :].astype(F32))).astype(BF16)
            y_even.append(
                jnp.dot(gated, w_eo_ref[:ATT_WIDTH, :], preferred_element_type=F32)
                + jnp.dot(ssm_ref[r, :], w_eo_ref[ATT_WIDTH:, :], preferred_element_type=F32))
        x_parts, hb_parts = [], []
        for h in range(2):
            r = slice(s * sub + h * half, s * sub + (h + 1) * half)
            x_h = x_ref[r, :] + _rms(y_even[h], post_e_ref[...])
            hb_h = _rms(x_h, pre_ref[...]).astype(BF16)
            ext[lo + h * half:lo + (h + 1) * half, :] = jnp.dot(
                hb_h, w_in_ref[:, :POOL_WIDTH], preferred_element_type=F32)
            x_parts.append(x_h)
            hb_parts.append(hb_h)
        x = jnp.concatenate(x_parts, axis=0)
        hb = jnp.concatenate(hb_parts, axis=0)
        t_idx = j * tm + s * sub + lax.broadcasted_iota(jnp.int32, (sub, 1), 0)
        inv_count = 1.0 / (t_idx + 1).astype(F32)
        y = None
        for g, w in enumerate(POOL_WINDOWS):
            cols = slice(g * POOL_GROUP, (g + 1) * POOL_GROUP)
            levels = int(math.log2(w))
            src_ref, src_cols = ext, cols
            for lev in range(levels):
                shift = 2 ** lev
                start = lo - 8 * (levels - 1 - lev)
                val = (src_ref[start:hi, src_cols] + src_ref[start - shift:hi - shift, src_cols])
                if lev < levels - 1:
                    dst = tmps[lev % 2]
                    dst[start:hi, :] = val
                    src_ref, src_cols = dst, slice(None)
            u_g = ext[lo:hi, cols]
            mixed = val * jnp.maximum(inv_count, 1.0 / w) - u_g
            yg = jnp.dot(mixed.astype(BF16), pw_ref[cols, :], preferred_element_type=F32)
            gate = jnp.dot(hb, w_in_ref[:, POOL_WIDTH + g * POOL_GROUP:
                                        POOL_WIDTH + (g + 1) * POOL_GROUP],
                           preferred_element_type=F32)
            yg = yg * ps_ref[:, cols] * _silu(gate)
            part = jnp.dot(yg.astype(BF16), w_out_ref[cols, :], preferred_element_type=F32)
            y = part if y is None else y + part
        o_ref[rows, :] = x + _rms(y, post_ref[...])
    ext[0:hist, :] = ext[tm:total, :]


def _even_tail_odd(x2, att2, g2, ssm2, w_even_out, pre, post, w_in, pool_w, pool_scale, w_out, i,
                   layer):
    tm = ODD_TILE
    nj = SEQ // tm
    row_spec = pl.BlockSpec((tm, D_MODEL), lambda b, j: (b * nj + j, 0))
    att_spec = pl.BlockSpec((tm, ATT_WIDTH), lambda b, j: (b * nj + j, 0))
    hbm = pl.BlockSpec(memory_space=pl.ANY)
    return pl.pallas_call(
        functools.partial(_tail_kernel, i),
        grid=(BATCH, nj),
        in_specs=[row_spec, att_spec, att_spec,
                  pl.BlockSpec((tm, SSM_WIDTH), lambda b, j: (b * nj + j, 0)),
                  hbm, _layer_spec(post, layer),
                  _layer_spec(pre, layer + 1), hbm,
                  hbm, _layer_spec(pool_scale, i),
                  hbm, _layer_spec(post, layer + 1)],
        out_specs=row_spec,
        out_shape=jax.ShapeDtypeStruct((BATCH * SEQ, D_MODEL), F32),
        scratch_shapes=[pltpu.VMEM((POOL_HIST + tm, POOL_WIDTH), F32),
                        pltpu.VMEM((POOL_HIST + tm, POOL_GROUP), F32),
                        pltpu.VMEM((POOL_HIST + tm, POOL_GROUP), F32)]
        + _weight_scratch(*w_even_out.shape[1:]) + _weight_scratch(*w_in.shape[1:])
        + _weight_scratch(*pool_w.shape[1:]) + _weight_scratch(*w_out.shape[1:]),
        compiler_params=pltpu.CompilerParams(
            dimension_semantics=("arbitrary", "arbitrary"), vmem_limit_bytes=VMEM_LIMIT),
        name="even_tail_odd_layer",
    )(x2, att2, g2, ssm2, w_even_out, post, pre, w_in, pool_w, pool_scale, w_out, post)


def _rope_tables():
    half = ROT_DIM // 2
    inv_freq = (np.float32(ROPE_THETA)
                ** (-np.arange(0, ROT_DIM, 2, dtype=np.float32) / np.float32(ROT_DIM)))
    ang = np.arange(SEQ, dtype=np.float32)[:, None] * inv_freq[None, :].astype(np.float32)
    cos, sin = np.cos(ang).astype(np.float32), np.sin(ang).astype(np.float32)
    zeros = np.zeros((SEQ, HEAD_DIM - ROT_DIM), np.float32)
    zeros_h = np.zeros((SEQ, half), np.float32)
    cos_h = np.concatenate([cos, cos, np.ones_like(zeros)], axis=1)
    sa_h = np.concatenate([-sin, zeros_h, zeros], axis=1)
    sb_h = np.concatenate([zeros_h, sin, zeros], axis=1)
    rep = LANES // HEAD_DIM
    pos = np.arange(SEQ).reshape(SEQ // ROW_TILE, ROW_TILE // N_CLS, N_CLS)
    order = pos.transpose(0, 2, 1).reshape(SEQ)
    return tuple(jnp.asarray(np.tile(t, (1, rep))[order]) for t in (cos_h, sa_h, sb_h))


def _ssm_params(a_re, a_im, log_dt, b_re, b_im, c_re, c_im):
    n_layer = a_re.shape[0]
    lam = lax.complex(a_re.astype(F32), a_im.astype(F32))
    dt = jnp.exp(log_dt.astype(F32))[..., None]
    lam_bar = jnp.exp(lam * dt)
    b_bar = ((lam_bar - 1.0) / lam)[..., None] * lax.complex(b_re.astype(F32), b_im.astype(F32))
    lam_re = jnp.broadcast_to(jnp.real(lam_bar).reshape(n_layer, 1, N_STATE),
                              (n_layer, BATCH, N_STATE))
    lam_im = jnp.broadcast_to(jnp.imag(lam_bar).reshape(n_layer, 1, N_STATE),
                              (n_layer, BATCH, N_STATE))
    gpc = LANES // SSM_GROUP
    n_chunk = SSM_GROUPS // gpc
    eye = jnp.eye(gpc, dtype=F32)

    def block_diag_in(m):
        m = m.reshape(n_layer, n_chunk, gpc, SSM_STATE, SSM_GROUP)
        return jnp.einsum('lcgph,gk->lcghkp', m, eye).reshape(
            n_layer, n_chunk, gpc * SSM_GROUP, gpc * SSM_STATE)

    def block_diag_out(m):
        m = m.reshape(n_layer, n_chunk, gpc, SSM_GROUP, SSM_STATE)
        return jnp.einsum('lcghp,gk->lcgpkh', m, eye).reshape(
            n_layer, n_chunk, gpc * SSM_STATE, gpc * SSM_GROUP)

    wb = jnp.concatenate([block_diag_in(jnp.real(b_bar)), block_diag_in(jnp.imag(b_bar))],
                         axis=3).astype(BF16)
    wc_re = block_diag_out(c_re.astype(F32)).astype(BF16)
    wc_im = block_diag_out(-c_im.astype(F32)).astype(BF16)
    return lam_re, lam_im, wb, wc_re, wc_im


def kernel(x, pre_norm, post_norm, even_w_in, even_w_out, ssm_a_re, ssm_a_im, ssm_log_dt,
           ssm_b_re, ssm_b_im, ssm_c_re, ssm_c_im, ssm_d, ssm_glu_w, ssm_glu_b,
           odd_w_in, pool_w, pool_scale, odd_w_out):
    cos_t, sa_t, sb_t = _rope_tables()
    pre = pre_norm.astype(F32).reshape(DEPTH, 1, D_MODEL)
    post = post_norm.astype(F32).reshape(DEPTH, 1, D_MODEL)
    even_w_in_f = even_w_in.astype(F32)
    even_w_out_f = even_w_out.astype(F32)
    odd_w_in_f = odd_w_in.astype(F32)
    odd_w_out_f = odd_w_out.astype(F32)
    pool_w_f = pool_w.astype(F32).reshape(-1, POOL_WIDTH, POOL_GROUP)
    pool_scale3 = pool_scale.astype(F32).reshape(-1, 1, POOL_WIDTH)
    ssm_params = _ssm_params(ssm_a_re, ssm_a_im, ssm_log_dt, ssm_b_re, ssm_b_im,
                             ssm_c_re, ssm_c_im) + (
        ssm_d.astype(F32).reshape(-1, 1, SSM_WIDTH), ssm_glu_w.astype(BF16),
        ssm_glu_b.astype(F32).reshape(-1, 1, SSM_WIDTH))
    x2 = x.reshape(BATCH * SEQ, D_MODEL)
    assert DEPTH % 2 == 0
    for layer in range(0, DEPTH, 2):
        i = layer // 2
        q16, k16, v16, g, us = _even_in(x2, pre, layer, even_w_in_f, i, cos_t, sa_t, sb_t)
        att = _attention(q16, k16, v16)
        ssm = _ssm(us.reshape(BATCH, SEQ, 2 * SSM_WIDTH), ssm_params, i)
        x2 = _even_tail_odd(x2, att.reshape(BATCH * SEQ, ATT_WIDTH), g,
                            ssm.reshape(BATCH * SEQ, SSM_WIDTH), even_w_out_f, pre, post,
                            odd_w_in_f, pool_w_f, pool_scale3, odd_w_out_f, i, layer)
    return x2.reshape(BATCH, SEQ, D_MODEL)
```

```python
import functools
import math

import jax
import jax.numpy as jnp
import numpy as np
from jax import lax
from jax.experimental import pallas as pl
from jax.experimental.pallas import tpu as pltpu

F32 = jnp.float32
BF16 = jnp.bfloat16

D_MODEL = 1024
BATCH = 8
SEQ = 2048
DEPTH = 4
HEAD_DIM = 64
ATT_WIDTH = 1024
ROT_DIM = 16
ROPE_THETA = 500000.0
SSM_WIDTH = 512
SSM_GROUP = 16
SSM_GROUPS = 32
SSM_STATE = 64
N_STATE = SSM_GROUPS * SSM_STATE
POOL_WIDTH = 2048
POOL_WINDOWS = (2, 4, 8, 16)
POOL_GROUP = 512
EVEN_IN = 4 * ATT_WIDTH + 2 * SSM_WIDTH
RMS_EPS = 1e-6
LOG2_E = math.log2(math.e)

LANES = 128
N_CLS = 16
CLS_LEN = SEQ // N_CLS
ATT_CHUNK = 128
ATT_GROUP = 4
ATT_PAIRS = 2
NEG_BIG = -1e30
VMEM_LIMIT = 56 * 1024 * 1024
WEIGHT_CHUNK_BYTES = 3 << 19
BF16_SUBLANES = 16

ROW_TILE = 512
SSM_TC = 64
ODD_TILE = 512
ODD_SUB = 512
POOL_HIST = 32


def _sigmoid(x):
    return 0.5 + 0.5 * jnp.tanh(0.5 * x)


def _silu(x):
    h = 0.5 * x
    return h + h * jnp.tanh(h)


def _rms(x, gain):
    return x * lax.rsqrt(jnp.mean(x * x, axis=-1, keepdims=True) + RMS_EPS) * gain


def _const_spec(shape):
    nd = len(shape)
    return pl.BlockSpec(shape, lambda *_: (0,) * nd, pipeline_mode=pl.Buffered(1))


def _load_weights_bf16(layer, weights):
    def copy(w_hbm, stage, sem, k):
        slots, chunk = stage.shape[0], stage.shape[1]
        return pltpu.make_async_copy(
            w_hbm.at[layer, pl.ds(k * chunk, chunk), :], stage.at[k % slots], sem.at[k % slots])

    n_chunks = [dst.shape[0] // stage.shape[1] for _, dst, stage, _ in weights]
    for (w_hbm, dst, stage, sem), n in zip(weights, n_chunks):
        for k in range(min(stage.shape[0], n)):
            copy(w_hbm, stage, sem, k).start()
    for k in range(max(n_chunks)):
        for (w_hbm, dst, stage, sem), n in zip(weights, n_chunks):
            if k >= n:
                continue
            slots, chunk = stage.shape[0], stage.shape[1]
            copy(w_hbm, stage, sem, k).wait()
            dst[k * chunk:(k + 1) * chunk, :] = stage[k % slots].astype(BF16)
            if k + slots < n:
                copy(w_hbm, stage, sem, k + slots).start()


def _weight_scratch(rows, cols, slots=2):
    chunk = rows
    while chunk * cols * 4 > WEIGHT_CHUNK_BYTES:
        chunk //= 2
    assert rows % chunk == 0 and chunk % BF16_SUBLANES == 0
    return [pltpu.VMEM((rows, cols), BF16), pltpu.VMEM((slots, chunk, cols), F32),
            pltpu.SemaphoreType.DMA((slots,))]


def _first_step():
    return jnp.logical_and(pl.program_id(0) == 0, pl.program_id(1) == 0)


def _layer_spec(stacked, layer):
    shape = stacked.shape[1:]
    nd = len(shape)
    return pl.BlockSpec((None,) + tuple(shape), lambda *_: (layer,) + (0,) * nd,
                        pipeline_mode=pl.Buffered(1))


def _even_in_kernel(layer_idx, x_ref, gain_ref, w_hbm, cos_ref, sa_ref, sb_ref,
                    q16_ref, k16_ref, v16_ref, g_ref, us_ref, h_nat, h_cls, w_ref, stage, sem):
    tm = ROW_TILE
    per_cls = tm // N_CLS

    @pl.when(_first_step())
    def _():
        _load_weights_bf16(layer_idx, [(w_hbm, w_ref, stage, sem)])

    def proj(lhs, lo, width):
        return jnp.dot(lhs, w_ref[:, lo:lo + width], preferred_element_type=F32)

    half = tm // 2
    for hh in range(2):
        r = slice(hh * half, (hh + 1) * half)
        h = _rms(x_ref[r, :], gain_ref[...])
        for c in range(D_MODEL // LANES):
            h_nat[c, r, :] = h[:, c * LANES:(c + 1) * LANES]
        hb = h.astype(BF16)
        g_ref[r, :] = proj(hb, 3 * ATT_WIDTH, ATT_WIDTH).astype(g_ref.dtype)
        us_ref[r, :] = proj(hb, 4 * ATT_WIDTH, 2 * SSM_WIDTH).astype(us_ref.dtype)
    for c in range(D_MODEL // LANES):
        for cls in range(N_CLS):
            h_cls[cls * per_cls:(cls + 1) * per_cls, c * LANES:(c + 1) * LANES] = (
                h_nat[c, pl.ds(cls, per_cls, stride=N_CLS), :].astype(BF16))
    hb_cls = h_cls[...]
    cos = cos_ref[...]
    sa = sa_ref[...]
    sb = sb_ref[...]

    def rope_store(dst, y, scale):
        for c in range(ATT_WIDTH // LANES):
            yc = y[:, c * LANES:(c + 1) * LANES]
            r = (yc * cos + pltpu.roll(yc, LANES - ROT_DIM // 2, 1) * sa
                 + pltpu.roll(yc, ROT_DIM // 2, 1) * sb)
            dst[0, :, :, c * LANES:(c + 1) * LANES] = (r * scale).reshape(N_CLS, per_cls, LANES)

    rope_store(q16_ref, proj(hb_cls, 0, ATT_WIDTH), HEAD_DIM ** -0.5 * LOG2_E)
    rope_store(k16_ref, proj(hb_cls, ATT_WIDTH, ATT_WIDTH), 1.0)
    v16_ref[0] = proj(hb_cls, 2 * ATT_WIDTH, ATT_WIDTH).reshape(N_CLS, per_cls, ATT_WIDTH)


def _even_in(x2, gains, layer, w_in, i, cos_t, sa_t, sb_t):
    tm = ROW_TILE
    nj = SEQ // tm
    row_spec = pl.BlockSpec((tm, D_MODEL), lambda b, j: (b * nj + j, 0))
    tab_spec = pl.BlockSpec((tm, LANES), lambda b, j: (j, 0))
    att_spec = pl.BlockSpec((tm, ATT_WIDTH), lambda b, j: (b * nj + j, 0))
    cls_spec = pl.BlockSpec((1, N_CLS, tm // N_CLS, ATT_WIDTH), lambda b, j: (b, 0, j, 0))
    att_shape = jax.ShapeDtypeStruct((BATCH * SEQ, ATT_WIDTH), BF16)
    cls_shape = jax.ShapeDtypeStruct((BATCH, N_CLS, CLS_LEN, ATT_WIDTH), F32)
    return pl.pallas_call(
        functools.partial(_even_in_kernel, i),
        grid=(BATCH, nj),
        in_specs=[row_spec, _layer_spec(gains, layer), pl.BlockSpec(memory_space=pl.ANY),
                  tab_spec, tab_spec, tab_spec],
        out_specs=[cls_spec, cls_spec, cls_spec, att_spec, att_spec],
        out_shape=[cls_shape, cls_shape, cls_shape, att_shape, att_shape],
        scratch_shapes=[pltpu.VMEM((D_MODEL // LANES, tm, LANES), F32),
                        pltpu.VMEM((tm, D_MODEL), BF16)]
        + _weight_scratch(D_MODEL, EVEN_IN, slots=4),
        compiler_params=pltpu.CompilerParams(
            dimension_semantics=("arbitrary", "arbitrary"), vmem_limit_bytes=VMEM_LIMIT),
        name="even_in_proj",
    )(x2, gains, w_in, cos_t, sa_t, sb_t)


def _attn_masks():
    ch = ATT_CHUNK
    per = ch // N_CLS
    quarter = ch // 4
    row = np.arange(ch)
    col1 = np.arange(ch)
    col2 = np.arange(2 * ch)
    tau = N_CLS * (row % per) + row // per
    kk = N_CLS * (col2 % (2 * per)) + col2 // (2 * per)
    p1_full = np.where(kk[None, :] < ch, kk[None, :] >= tau[:, None],
                       (kk[None, :] - ch) <= tau[:, None])
    p1_first = (N_CLS * (col1 % per) + col1 // per)[None, :] <= tau[:, None]
    off = 4 * (row % quarter) + row // quarter
    kk2 = 4 * (col2 % (2 * quarter)) + col2 // (2 * quarter)
    p2_full = (kk2[None, :] >= off[:, None]) & (kk2[None, :] <= off[:, None] + ch)
    p2_first = (4 * (col1 % quarter) + col1 // quarter)[None, :] <= off[:, None]
    p3 = col1[None, :] <= row[:, None]
    valid = np.concatenate([p1_full, p1_first, p2_full, p2_first, p3], axis=1)
    return jnp.asarray(np.where(valid, 0.0, NEG_BIG), dtype=F32)


def _attn_kernel(q16_ref, k16_ref, v16_ref, mask_ref, o_ref, *scratch):
    per_pair = len(scratch) // ATT_PAIRS
    for pair in range(ATT_PAIRS):
        _attn_head_pair(q16_ref, k16_ref, v16_ref, mask_ref, o_ref,
                        slice(pair * LANES, (pair + 1) * LANES),
                        *scratch[pair * per_pair:(pair + 1) * per_pair])


def _attn_head_pair(q16_ref, k16_ref, v16_ref, mask_ref, o_ref, lanes,
                    m_a, m_b, acc_a, acc_b, fin):
    ch = ATT_CHUNK
    per = ch // N_CLS
    lane = lax.broadcasted_iota(jnp.int32, (1, LANES), 1)
    is_a = lane < HEAD_DIM
    is_b = jnp.logical_not(is_a)
    state = ((m_a, acc_a), (m_b, acc_b))
    ones_row = jnp.ones((1, LANES), BF16)
    zeros_row = jnp.zeros((1, LANES), BF16)
    row = lax.broadcasted_iota(jnp.int32, (ch, ch), 0)
    col = lax.broadcasted_iota(jnp.int32, (ch, ch), 1)
    shuffle = jnp.where(col == per * (row % N_CLS) + row // N_CLS, 1.0, 0.0).astype(BF16)
    mask_p1_full = slice(0, 2 * ch)
    mask_p1_first = slice(2 * ch, 3 * ch)
    mask_p2_full = slice(3 * ch, 5 * ch)
    mask_p2_first = slice(5 * ch, 6 * ch)
    mask_p3 = slice(6 * ch, 7 * ch)

    def gather(ref, slabs):
        parts = [ref[s:s + n, :] for s, n in slabs]
        return parts[0] if len(parts) == 1 else jnp.concatenate(parts, axis=0)

    def gather_cls(ref, slabs):
        parts = [ref[0, s // CLS_LEN, s % CLS_LEN:s % CLS_LEN + n, lanes] for s, n in slabs]
        return parts[0] if len(parts) == 1 else jnp.concatenate(parts, axis=0)

    def scatter(ref, slabs, val):
        off = 0
        for s, n in slabs:
            ref[s:s + n, :] = val[off:off + n, :]
            off += n

    def phase_scores(items):
        scores = []
        for load_q, load_k, load_v, mask, slabs, has_state, finalize in items:
            qc = load_q()
            kc = load_k()
            for is_h in (is_a, is_b):
                qh = jnp.where(is_h, qc, zeros_row)
                s = lax.dot_general(qh, kc, (((1,), (1,)), ((), ())),
                                    preferred_element_type=F32) + mask_ref[:, mask]
                scores.append((s, jnp.max(s, axis=1, keepdims=True)))
        return scores

    def phase_update(items, scores):
        partial = []
        n = 0
        for load_q, load_k, load_v, mask, slabs, has_state, finalize in items:
            vc = load_v()
            for h, is_h in enumerate((is_a, is_b)):
                s, m_c = scores[n]
                n += 1
                if has_state:
                    m_prev = gather(state[h][0], slabs)
                    m_new = jnp.maximum(m_prev, m_c)
                else:
                    m_prev = None
                    m_new = jnp.broadcast_to(m_c, (ch, LANES))
                m_k = jnp.concatenate([m_new] * (s.shape[1] // LANES), axis=1)
                p = jnp.exp2(s - m_k)
                vh = jnp.where(is_h, vc, ones_row)
                pv = jnp.dot(p.astype(BF16), vh, preferred_element_type=F32)
                partial.append((m_prev, m_new, pv))
        n = 0
        for load_q, load_k, load_v, mask, slabs, has_state, finalize in items:
            res = []
            for h in range(2):
                m_prev, m_new, pv = partial[n]
                n += 1
                if has_state:
                    a_new = jnp.exp2(m_prev - m_new) * gather(state[h][1], slabs) + pv
                else:
                    a_new = pv
                res.append(a_new)
                if not finalize:
                    scatter(state[h][0], slabs, m_new)
                    scatter(state[h][1], slabs, a_new)
            if finalize:
                num = jnp.where(is_a, res[0], res[1])
                den = pltpu.roll(jnp.where(is_a, res[1], res[0]), HEAD_DIM, 1)
                scatter(fin, slabs, num / den)

    def block(slabs, kslabs, mask, has_state, finalize):
        return (lambda: gather_cls(q16_ref, slabs).astype(BF16),
                lambda: gather_cls(k16_ref, kslabs).astype(BF16),
                lambda: gather_cls(v16_ref, kslabs).astype(BF16), mask, slabs, has_state, finalize)

    order = []
    for i in range(SEQ // ch):
        slabs = [(r * CLS_LEN + per * i, per) for r in range(N_CLS)]
        if i > 0:
            kslabs = [(r * CLS_LEN + per * (i - 1), 2 * per) for r in range(N_CLS)]
            order.append(block(slabs, kslabs, mask_p1_full, False, False))
        else:
            order.append(block(slabs, slabs, mask_p1_first, False, False))

    quarter = ch // 4
    dil4 = []
    for r4 in range(4):
        for ic in range(SEQ // (4 * ch)):
            cls = [(4 * j + r4) * CLS_LEN for j in range(4)]
            slabs = [(c + quarter * ic, quarter) for c in cls]
            if ic > 0:
                kslabs = [(c + quarter * (ic - 1), 2 * quarter) for c in cls]
                dil4.append(block(slabs, kslabs, mask_p2_full, True, False))
            else:
                dil4.append(block(slabs, slabs, mask_p2_first, True, False))

    order += dil4

    for r in range(N_CLS):
        slabs = [(r * CLS_LEN, CLS_LEN)]
        order.append(block(slabs, slabs, mask_p3, True, True))

    assert (SEQ // ch) % ATT_GROUP == 0
    pending = None
    for i in range(0, len(order), ATT_GROUP):
        items = order[i:i + ATT_GROUP]
        scores = phase_scores(items)
        if pending is not None:
            phase_update(*pending)
        pending = (items, scores)
    phase_update(*pending)

    for i in range(SEQ // ch):
        slab = gather(fin, [(r * CLS_LEN + per * i, per) for r in range(N_CLS)])
        o_ref[0, i * ch:(i + 1) * ch, lanes] = jnp.dot(
            shuffle, slab.astype(BF16), preferred_element_type=F32).astype(o_ref.dtype)


def _attention(q16, k16, v16):
    assert CLS_LEN == ATT_CHUNK and SEQ % (4 * ATT_CHUNK) == 0
    width = ATT_PAIRS * LANES
    blk = pl.BlockSpec((1, SEQ, width), lambda b, h: (b, 0, h))
    cls_blk = pl.BlockSpec((1, N_CLS, CLS_LEN, width), lambda b, h: (b, 0, 0, h))
    f32_scr = pltpu.VMEM((SEQ, LANES), F32)
    masks = _attn_masks()
    return pl.pallas_call(
        _attn_kernel,
        grid=(BATCH, ATT_WIDTH // width),
        in_specs=[cls_blk, cls_blk, cls_blk, _const_spec(masks.shape)],
        out_specs=blk,
        out_shape=jax.ShapeDtypeStruct((BATCH, SEQ, ATT_WIDTH), BF16),
        scratch_shapes=[f32_scr] * (5 * ATT_PAIRS),
        compiler_params=pltpu.CompilerParams(
            dimension_semantics=("arbitrary", "arbitrary"), vmem_limit_bytes=VMEM_LIMIT),
        name="dilated_attention",
    )(q16, k16, v16, masks)


def _ssm_kernel(us_ref, perm_ref, lam_re_ref, lam_im_ref, wb_ref, wc_re_ref, wc_im_ref,
                d_ref, gw_ref, gb_ref, o_ref, xr, xi, st_re, st_im, y_tb, y_bt):
    tc = SSM_TC
    rows = tc * BATCH
    n_chunk = SSM_WIDTH // LANES
    cpc = N_STATE // n_chunk

    @pl.when(pl.program_id(0) == 0)
    def _():
        st_re[...] = jnp.zeros_like(st_re)
        st_im[...] = jnp.zeros_like(st_im)

    ub = us_ref[:, :, :SSM_WIDTH].reshape(rows, SSM_WIDTH)
    u_tb = jnp.dot(perm_ref[...], ub, preferred_element_type=F32).astype(BF16)
    def b_proj(c):
        cols = slice(c * cpc, (c + 1) * cpc)
        bu = jnp.dot(u_tb[:, c * LANES:(c + 1) * LANES], wb_ref[c], preferred_element_type=F32)
        xr[:, cols] = bu[:, :cpc]
        xi[:, cols] = bu[:, cpc:]

    def recurrence(c):
        cols = slice(c * cpc, (c + 1) * cpc)
        lr = lam_re_ref[:, cols]
        li = lam_im_ref[:, cols]
        sr = st_re[:, cols]
        si = st_im[:, cols]
        for t in range(tc):
            trow = slice(t * BATCH, (t + 1) * BATCH)
            sr, si = (lr * sr - li * si + xr[trow, cols], lr * si + li * sr + xi[trow, cols])
            xr[trow, cols] = sr
            xi[trow, cols] = si
        st_re[:, cols] = sr
        st_im[:, cols] = si

    def c_proj(c):
        cols = slice(c * cpc, (c + 1) * cpc)
        y_tb[c] = (
            jnp.dot(xr[:, cols].astype(BF16), wc_re_ref[c], preferred_element_type=F32)
            + jnp.dot(xi[:, cols].astype(BF16), wc_im_ref[c], preferred_element_type=F32))

    def skip_gelu(c):
        lanes = slice(c * LANES, (c + 1) * LANES)
        y_c = jnp.concatenate(
            [y_tb[c, pl.ds(b, tc, stride=BATCH), :] for b in range(BATCH)], axis=0)
        y_bt[:, lanes] = jax.nn.gelu(y_c + d_ref[:, lanes] * ub[:, lanes].astype(F32))

    b_proj(0)
    gate = _silu(us_ref[:, :, SSM_WIDTH:].reshape(rows, SSM_WIDTH).astype(F32))
    for c in range(n_chunk):
        if c + 1 < n_chunk:
            b_proj(c + 1)
        recurrence(c)
        if c > 0:
            skip_gelu(c - 1)
        c_proj(c)
    skip_gelu(n_chunk - 1)
    y = y_bt[...]
    z = jnp.dot(y.astype(BF16), gw_ref[...], preferred_element_type=F32) + gb_ref[...]
    out = y * _sigmoid(z) * gate
    o_ref[...] = out.reshape(BATCH, tc, SSM_WIDTH).astype(o_ref.dtype)


def _ssm(us3, params, i):
    tc = SSM_TC
    rows = tc * BATCH
    dst_row = np.arange(rows)
    perm = jnp.asarray((dst_row[:, None] % BATCH) * tc + dst_row[:, None] // BATCH
                       == dst_row[None, :], dtype=BF16)
    return pl.pallas_call(
        _ssm_kernel,
        grid=(SEQ // tc,),
        in_specs=[pl.BlockSpec((BATCH, tc, 2 * SSM_WIDTH), lambda t: (0, t, 0)),
                  _const_spec((rows, rows))] + [_layer_spec(p, i) for p in params],
        out_specs=pl.BlockSpec((BATCH, tc, SSM_WIDTH), lambda t: (0, t, 0)),
        out_shape=jax.ShapeDtypeStruct((BATCH, SEQ, SSM_WIDTH), BF16),
        scratch_shapes=[pltpu.VMEM((rows, N_STATE), F32), pltpu.VMEM((rows, N_STATE), F32),
                        pltpu.VMEM((BATCH, N_STATE), F32), pltpu.VMEM((BATCH, N_STATE), F32),
                        pltpu.VMEM((SSM_WIDTH // LANES, rows, LANES), F32),
                        pltpu.VMEM((rows, SSM_WIDTH), F32)],
        compiler_params=pltpu.CompilerParams(
            dimension_semantics=("arbitrary",), vmem_limit_bytes=VMEM_LIMIT),
        name="s5_branch",
    )(us3, perm, *params)


def _tail_kernel(layer_idx, x_ref, att_ref, g_ref, ssm_ref, w_eo_hbm, post_e_ref,
                 pre_ref, w_in_hbm, pw_hbm, ps_ref, w_out_hbm, post_ref, o_ref,
                 ext, tmp0, tmp1,
                 w_eo_ref, stage_eo, sem_eo, w_in_ref, stage_in, sem_in,
                 pw_ref, stage_pw, sem_pw, w_out_ref, stage_out, sem_out):
    tm = ODD_TILE
    hist = POOL_HIST
    total = hist + tm
    j = pl.program_id(1)

    @pl.when(_first_step())
    def _():
        _load_weights_bf16(layer_idx, [(w_eo_hbm, w_eo_ref, stage_eo, sem_eo),
                                       (w_in_hbm, w_in_ref, stage_in, sem_in),
                                       (pw_hbm, pw_ref, stage_pw, sem_pw),
                                       (w_out_hbm, w_out_ref, stage_out, sem_out)])

    @pl.when(j == 0)
    def _():
        ext[0:hist, :] = jnp.zeros((hist, POOL_WIDTH), F32)

    tmps = (tmp0, tmp1)
    sub = ODD_SUB
    for s in range(tm // sub):
        lo = hist + s * sub
        hi = lo + sub
        rows = slice(s * sub, (s + 1) * sub)
        half = sub // 2
        y_even = []
        for h in range(2):
            r = slice(s * sub + h * half, s * sub + (h + 1) * half)
            gated = (att_ref[r, :].astype(F32) * _silu(g_ref[r, :].astype(F32))).astype(BF16)
            y_even.append(
                jnp.dot(gated, w_eo_ref[:ATT_WIDTH, :], preferred_element_type=F32)
                + jnp.dot(ssm_ref[r, :], w_eo_ref[ATT_WIDTH:, :], preferred_element_type=F32))
        x_parts, hb_parts = [], []
        for h in range(2):
            r = slice(s * sub + h * half, s * sub + (h + 1) * half)
            x_h = x_ref[r, :] + _rms(y_even[h], post_e_ref[...])
            hb_h = _rms(x_h, pre_ref[...]).astype(BF16)
            ext[lo + h * half:lo + (h + 1) * half, :] = jnp.dot(
                hb_h, w_in_ref[:, :POOL_WIDTH], preferred_element_type=F32)
            x_parts.append(x_h)
            hb_parts.append(hb_h)
        x = jnp.concatenate(x_parts, axis=0)
        hb = jnp.concatenate(hb_parts, axis=0)
        t_idx = j * tm + s * sub + lax.broadcasted_iota(jnp.int32, (sub, 1), 0)
        inv_count = 1.0 / (t_idx + 1).astype(F32)
        y = None
        for g, w in enumerate(POOL_WINDOWS):
            cols = slice(g * POOL_GROUP, (g + 1) * POOL_GROUP)
            levels = int(math.log2(w))
            src_ref, src_cols = ext, cols
            for lev in range(levels):
                shift = 2 ** lev
                start = lo - 8 * (levels - 1 - lev)
                val = (src_ref[start:hi, src_cols] + src_ref[start - shift:hi - shift, src_cols])
                if lev < levels - 1:
                    dst = tmps[lev % 2]
                    dst[start:hi, :] = val
                    src_ref, src_cols = dst, slice(None)
            u_g = ext[lo:hi, cols]
            mixed = val * jnp.maximum(inv_count, 1.0 / w) - u_g
            yg = jnp.dot(mixed.astype(BF16), pw_ref[cols, :], preferred_element_type=F32)
            gate = jnp.dot(hb, w_in_ref[:, POOL_WIDTH + g * POOL_GROUP:
                                        POOL_WIDTH + (g + 1) * POOL_GROUP],
                           preferred_element_type=F32)
            yg = yg * ps_ref[:, cols] * _silu(gate)
            part = jnp.dot(yg.astype(BF16), w_out_ref[cols, :], preferred_element_type=F32)
            y = part if y is None else y + part
        o_ref[rows, :] = x + _rms(y, post_ref[...])
    ext[0:hist, :] = ext[tm:total, :]


def _even_tail_odd(x2, att2, g2, ssm2, w_even_out, pre, post, w_in, pool_w, pool_scale, w_out, i,
                   layer):
    tm = ODD_TILE
    nj = SEQ // tm
    row_spec = pl.BlockSpec((tm, D_MODEL), lambda b, j: (b * nj + j, 0))
    att_spec = pl.BlockSpec((tm, ATT_WIDTH), lambda b, j: (b * nj + j, 0))
    hbm = pl.BlockSpec(memory_space=pl.ANY)
    return pl.pallas_call(
        functools.partial(_tail_kernel, i),
        grid=(BATCH, nj),
        in_specs=[row_spec, att_spec, att_spec,
                  pl.BlockSpec((tm, SSM_WIDTH), lambda b, j: (b * nj + j, 0)),
                  hbm, _layer_spec(post, layer),
                  _layer_spec(pre, layer + 1), hbm,
                  hbm, _layer_spec(pool_scale, i),
                  hbm, _layer_spec(post, layer + 1)],
        out_specs=row_spec,
        out_shape=jax.ShapeDtypeStruct((BATCH * SEQ, D_MODEL), F32),
        scratch_shapes=[pltpu.VMEM((POOL_HIST + tm, POOL_WIDTH), F32),
                        pltpu.VMEM((POOL_HIST + tm, POOL_GROUP), F32),
                        pltpu.VMEM((POOL_HIST + tm, POOL_GROUP), F32)]
        + _weight_scratch(*w_even_out.shape[1:]) + _weight_scratch(*w_in.shape[1:])
        + _weight_scratch(*pool_w.shape[1:]) + _weight_scratch(*w_out.shape[1:]),
        compiler_params=pltpu.CompilerParams(
            dimension_semantics=("arbitrary", "arbitrary"), vmem_limit_bytes=VMEM_LIMIT),
        name="even_tail_odd_layer",
    )(x2, att2, g2, ssm2, w_even_out, post, pre, w_in, pool_w, pool_scale, w_out, post)


def _rope_tables():
    half = ROT_DIM // 2
    inv_freq = (np.float32(ROPE_THETA)
                ** (-np.arange(0, ROT_DIM, 2, dtype=np.float32) / np.float32(ROT_DIM)))
    ang = np.arange(SEQ, dtype=np.float32)[:, None] * inv_freq[None, :].astype(np.float32)
    cos, sin = np.cos(ang).astype(np.float32), np.sin(ang).astype(np.float32)
    zeros = np.zeros((SEQ, HEAD_DIM - ROT_DIM), np.float32)
    zeros_h = np.zeros((SEQ, half), np.float32)
    cos_h = np.concatenate([cos, cos, np.ones_like(zeros)], axis=1)
    sa_h = np.concatenate([-sin, zeros_h, zeros], axis=1)
    sb_h = np.concatenate([zeros_h, sin, zeros], axis=1)
    rep = LANES // HEAD_DIM
    pos = np.arange(SEQ).reshape(SEQ // ROW_TILE, ROW_TILE // N_CLS, N_CLS)
    order = pos.transpose(0, 2, 1).reshape(SEQ)
    return tuple(jnp.asarray(np.tile(t, (1, rep))[order]) for t in (cos_h, sa_h, sb_h))


def _ssm_params(a_re, a_im, log_dt, b_re, b_im, c_re, c_im):
    n_layer = a_re.shape[0]
    lam = lax.complex(a_re.astype(F32), a_im.astype(F32))
    dt = jnp.exp(log_dt.astype(F32))[..., None]
    lam_bar = jnp.exp(lam * dt)
    b_bar = ((lam_bar - 1.0) / lam)[..., None] * lax.complex(b_re.astype(F32), b_im.astype(F32))
    lam_re = jnp.broadcast_to(jnp.real(lam_bar).reshape(n_layer, 1, N_STATE),
                              (n_layer, BATCH, N_STATE))
    lam_im = jnp.broadcast_to(jnp.imag(lam_bar).reshape(n_layer, 1, N_STATE),
                              (n_layer, BATCH, N_STATE))
    gpc = LANES // SSM_GROUP
    n_chunk = SSM_GROUPS // gpc
    eye = jnp.eye(gpc, dtype=F32)

    def block_diag_in(m):
        m = m.reshape(n_layer, n_chunk, gpc, SSM_STATE, SSM_GROUP)
        return jnp.einsum('lcgph,gk->lcghkp', m, eye).reshape(
            n_layer, n_chunk, gpc * SSM_GROUP, gpc * SSM_STATE)

    def block_diag_out(m):
        m = m.reshape(n_layer, n_chunk, gpc, SSM_GROUP, SSM_STATE)
        return jnp.einsum('lcghp,gk->lcgpkh', m, eye).reshape(
            n_layer, n_chunk, gpc * SSM_STATE, gpc * SSM_GROUP)

    wb = jnp.concatenate([block_diag_in(jnp.real(b_bar)), block_diag_in(jnp.imag(b_bar))],
                         axis=3).astype(BF16)
    wc_re = block_diag_out(c_re.astype(F32)).astype(BF16)
    wc_im = block_diag_out(-c_im.astype(F32)).astype(BF16)
    return lam_re, lam_im, wb, wc_re, wc_im


def kernel(x, pre_norm, post_norm, even_w_in, even_w_out, ssm_a_re, ssm_a_im, ssm_log_dt,
           ssm_b_re, ssm_b_im, ssm_c_re, ssm_c_im, ssm_d, ssm_glu_w, ssm_glu_b,
           odd_w_in, pool_w, pool_scale, odd_w_out):
    cos_t, sa_t, sb_t = _rope_tables()
    pre = pre_norm.astype(F32).reshape(DEPTH, 1, D_MODEL)
    post = post_norm.astype(F32).reshape(DEPTH, 1, D_MODEL)
    even_w_in_f = even_w_in.astype(F32)
    even_w_out_f = even_w_out.astype(F32)
    odd_w_in_f = odd_w_in.astype(F32)
    odd_w_out_f = odd_w_out.astype(F32)
    pool_w_f = pool_w.astype(F32).reshape(-1, POOL_WIDTH, POOL_GROUP)
    pool_scale3 = pool_scale.astype(F32).reshape(-1, 1, POOL_WIDTH)
    ssm_params = _ssm_params(ssm_a_re, ssm_a_im, ssm_log_dt, ssm_b_re, ssm_b_im,
                             ssm_c_re, ssm_c_im) + (
        ssm_d.astype(F32).reshape(-1, 1, SSM_WIDTH), ssm_glu_w.astype(BF16),
        ssm_glu_b.astype(F32).reshape(-1, 1, SSM_WIDTH))
    x2 = x.reshape(BATCH * SEQ, D_MODEL)
    assert DEPTH % 2 == 0
    for layer in range(0, DEPTH, 2):
        i = layer // 2
        q16, k16, v16, g, us = _even_in(x2, pre, layer, even_w_in_f, i, cos_t, sa_t, sb_t)
        att = _attention(q16, k16, v16)
        ssm = _ssm(us.reshape(BATCH, SEQ, 2 * SSM_WIDTH), ssm_params, i)
        x2 = _even_tail_odd(x2, att.reshape(BATCH * SEQ, ATT_WIDTH), g,
                            ssm.reshape(BATCH * SEQ, SSM_WIDTH), even_w_out_f, pre, post,
                            odd_w_in_f, pool_w_f, pool_scale3, odd_w_out_f, i, layer)
    return x2.reshape(BATCH, SEQ, D_MODEL)
```

```python
import functools
import math

import jax
import jax.numpy as jnp
import numpy as np
from jax import lax
from jax.experimental import pallas as pl
from jax.experimental.pallas import tpu as pltpu

F32 = jnp.float32
BF16 = jnp.bfloat16

D_MODEL = 1024
BATCH = 8
SEQ = 2048
DEPTH = 4
HEAD_DIM = 64
ATT_WIDTH = 1024
ROT_DIM = 16
ROPE_THETA = 500000.0
SSM_WIDTH = 512
SSM_GROUP = 16
SSM_GROUPS = 32
SSM_STATE = 64
N_STATE = SSM_GROUPS * SSM_STATE
POOL_WIDTH = 2048
POOL_WINDOWS = (2, 4, 8, 16)
POOL_GROUP = 512
EVEN_IN = 4 * ATT_WIDTH + 2 * SSM_WIDTH
RMS_EPS = 1e-6
LOG2_E = math.log2(math.e)

LANES = 128
N_CLS = 16
CLS_LEN = SEQ // N_CLS
ATT_CHUNK = 128
ATT_GROUP = 4
ATT_PAIRS = 2
NEG_BIG = -1e30
VMEM_LIMIT = 56 * 1024 * 1024
WEIGHT_CHUNK_BYTES = 3 << 19
BF16_SUBLANES = 16

ROW_TILE = 512
SSM_TC = 64
ODD_TILE = 512
ODD_SUB = 512
POOL_HIST = 32


def _sigmoid(x):
    return 0.5 + 0.5 * jnp.tanh(0.5 * x)


def _silu(x):
    h = 0.5 * x
    return h + h * jnp.tanh(h)


def _rms(x, gain):
    return x * lax.rsqrt(jnp.mean(x * x, axis=-1, keepdims=True) + RMS_EPS) * gain


def _const_spec(shape):
    nd = len(shape)
    return pl.BlockSpec(shape, lambda *_: (0,) * nd, pipeline_mode=pl.Buffered(1))


def _load_weights_bf16(layer, weights):
    def copy(w_hbm, stage, sem, k):
        slots, chunk = stage.shape[0], stage.shape[1]
        return pltpu.make_async_copy(
            w_hbm.at[layer, pl.ds(k * chunk, chunk), :], stage.at[k % slots], sem.at[k % slots])

    n_chunks = [dst.shape[0] // stage.shape[1] for _, dst, stage, _ in weights]
    for (w_hbm, dst, stage, sem), n in zip(weights, n_chunks):
        for k in range(min(stage.shape[0], n)):
            copy(w_hbm, stage, sem, k).start()
    for k in range(max(n_chunks)):
        for (w_hbm, dst, stage, sem), n in zip(weights, n_chunks):
            if k >= n:
                continue
            slots, chunk = stage.shape[0], stage.shape[1]
            copy(w_hbm, stage, sem, k).wait()
            dst[k * chunk:(k + 1) * chunk, :] = stage[k % slots].astype(BF16)
            if k + slots < n:
                copy(w_hbm, stage, sem, k + slots).start()


def _weight_scratch(rows, cols, slots=2):
    chunk = rows
    while chunk * cols * 4 > WEIGHT_CHUNK_BYTES:
        chunk //= 2
    assert rows % chunk == 0 and chunk % BF16_SUBLANES == 0
    return [pltpu.VMEM((rows, cols), BF16), pltpu.VMEM((slots, chunk, cols), F32),
            pltpu.SemaphoreType.DMA((slots,))]


def _first_step():
    return jnp.logical_and(pl.program_id(0) == 0, pl.program_id(1) == 0)


def _layer_spec(stacked, layer):
    shape = stacked.shape[1:]
    nd = len(shape)
    return pl.BlockSpec((None,) + tuple(shape), lambda *_: (layer,) + (0,) * nd,
                        pipeline_mode=pl.Buffered(1))


def _even_in_kernel(layer_idx, x_ref, gain_ref, w_hbm, cos_ref, sa_ref, sb_ref,
                    q16_ref, k16_ref, v16_ref, g_ref, us_ref, h_nat, h_cls, w_ref, stage, sem):
    tm = ROW_TILE
    per_cls = tm // N_CLS

    @pl.when(_first_step())
    def _():
        _load_weights_bf16(layer_idx, [(w_hbm, w_ref, stage, sem)])

    def proj(lhs, lo, width):
        return jnp.dot(lhs, w_ref[:, lo:lo + width], preferred_element_type=F32)

    half = tm // 2
    for hh in range(2):
        r = slice(hh * half, (hh + 1) * half)
        h = _rms(x_ref[r, :], gain_ref[...])
        for c in range(D_MODEL // LANES):
            h_nat[c, r, :] = h[:, c * LANES:(c + 1) * LANES]
        hb = h.astype(BF16)
        g_ref[r, :] = proj(hb, 3 * ATT_WIDTH, ATT_WIDTH).astype(g_ref.dtype)
        us_ref[r, :] = proj(hb, 4 * ATT_WIDTH, 2 * SSM_WIDTH).astype(us_ref.dtype)
    for c in range(D_MODEL // LANES):
        for cls in range(N_CLS):
            h_cls[cls * per_cls:(cls + 1) * per_cls, c * LANES:(c + 1) * LANES] = (
                h_nat[c, pl.ds(cls, per_cls, stride=N_CLS), :].astype(BF16))
    hb_cls = h_cls[...]
    cos = cos_ref[...]
    sa = sa_ref[...]
    sb = sb_ref[...]

    def rope_store(dst, y, scale):
        for c in range(ATT_WIDTH // LANES):
            yc = y[:, c * LANES:(c + 1) * LANES]
            r = (yc * cos + pltpu.roll(yc, LANES - ROT_DIM // 2, 1) * sa
                 + pltpu.roll(yc, ROT_DIM // 2, 1) * sb)
            dst[0, :, :, c * LANES:(c + 1) * LANES] = (
                (r * scale).astype(dst.dtype).reshape(N_CLS, per_cls, LANES))

    rope_store(q16_ref, proj(hb_cls, 0, ATT_WIDTH), HEAD_DIM ** -0.5 * LOG2_E)
    rope_store(k16_ref, proj(hb_cls, ATT_WIDTH, ATT_WIDTH), 1.0)
    v16_ref[0] = proj(hb_cls, 2 * ATT_WIDTH, ATT_WIDTH).astype(v16_ref.dtype).reshape(
        N_CLS, per_cls, ATT_WIDTH)


def _even_in(x2, gains, layer, w_in, i, cos_t, sa_t, sb_t):
    tm = ROW_TILE
    nj = SEQ // tm
    row_spec = pl.BlockSpec((tm, D_MODEL), lambda b, j: (b * nj + j, 0))
    tab_spec = pl.BlockSpec((tm, LANES), lambda b, j: (j, 0))
    att_spec = pl.BlockSpec((tm, ATT_WIDTH), lambda b, j: (b * nj + j, 0))
    cls_spec = pl.BlockSpec((1, N_CLS, tm // N_CLS, ATT_WIDTH), lambda b, j: (b, 0, j, 0))
    att_shape = jax.ShapeDtypeStruct((BATCH * SEQ, ATT_WIDTH), BF16)
    cls_shape = jax.ShapeDtypeStruct((BATCH, N_CLS, CLS_LEN, ATT_WIDTH), F32)
    return pl.pallas_call(
        functools.partial(_even_in_kernel, i),
        grid=(BATCH, nj),
        in_specs=[row_spec, _layer_spec(gains, layer), pl.BlockSpec(memory_space=pl.ANY),
                  tab_spec, tab_spec, tab_spec],
        out_specs=[cls_spec, cls_spec, cls_spec, att_spec, att_spec],
        out_shape=[cls_shape, jax.ShapeDtypeStruct(cls_shape.shape, BF16),
                   jax.ShapeDtypeStruct(cls_shape.shape, BF16), att_shape, att_shape],
        scratch_shapes=[pltpu.VMEM((D_MODEL // LANES, tm, LANES), F32),
                        pltpu.VMEM((tm, D_MODEL), BF16)]
        + _weight_scratch(D_MODEL, EVEN_IN, slots=4),
        compiler_params=pltpu.CompilerParams(
            dimension_semantics=("arbitrary", "arbitrary"), vmem_limit_bytes=VMEM_LIMIT),
        name="even_in_proj",
    )(x2, gains, w_in, cos_t, sa_t, sb_t)


def _attn_masks():
    ch = ATT_CHUNK
    per = ch // N_CLS
    quarter = ch // 4
    row = np.arange(ch)
    col1 = np.arange(ch)
    col2 = np.arange(2 * ch)
    tau = N_CLS * (row % per) + row // per
    kk = N_CLS * (col2 % (2 * per)) + col2 // (2 * per)
    p1_full = np.where(kk[None, :] < ch, kk[None, :] >= tau[:, None],
                       (kk[None, :] - ch) <= tau[:, None])
    p1_first = (N_CLS * (col1 % per) + col1 // per)[None, :] <= tau[:, None]
    off = 4 * (row % quarter) + row // quarter
    kk2 = 4 * (col2 % (2 * quarter)) + col2 // (2 * quarter)
    p2_full = (kk2[None, :] >= off[:, None]) & (kk2[None, :] <= off[:, None] + ch)
    p2_first = (4 * (col1 % quarter) + col1 // quarter)[None, :] <= off[:, None]
    p3 = col1[None, :] <= row[:, None]
    valid = np.concatenate([p1_full, p1_first, p2_full, p2_first, p3], axis=1)
    return jnp.asarray(np.where(valid, 0.0, NEG_BIG), dtype=F32)


def _attn_kernel(q16_ref, k16_ref, v16_ref, mask_ref, o_ref, *scratch):
    per_pair = len(scratch) // ATT_PAIRS
    for pair in range(ATT_PAIRS):
        _attn_head_pair(q16_ref, k16_ref, v16_ref, mask_ref, o_ref,
                        slice(pair * LANES, (pair + 1) * LANES),
                        *scratch[pair * per_pair:(pair + 1) * per_pair])


def _attn_head_pair(q16_ref, k16_ref, v16_ref, mask_ref, o_ref, lanes,
                    m_a, m_b, acc_a, acc_b, fin):
    ch = ATT_CHUNK
    per = ch // N_CLS
    lane = lax.broadcasted_iota(jnp.int32, (1, LANES), 1)
    is_a = lane < HEAD_DIM
    is_b = jnp.logical_not(is_a)
    state = ((m_a, acc_a), (m_b, acc_b))
    ones_row = jnp.ones((1, LANES), BF16)
    zeros_row = jnp.zeros((1, LANES), BF16)
    row = lax.broadcasted_iota(jnp.int32, (ch, ch), 0)
    col = lax.broadcasted_iota(jnp.int32, (ch, ch), 1)
    shuffle = jnp.where(col == per * (row % N_CLS) + row // N_CLS, 1.0, 0.0).astype(BF16)
    mask_p1_full = slice(0, 2 * ch)
    mask_p1_first = slice(2 * ch, 3 * ch)
    mask_p2_full = slice(3 * ch, 5 * ch)
    mask_p2_first = slice(5 * ch, 6 * ch)
    mask_p3 = slice(6 * ch, 7 * ch)

    def gather(ref, slabs):
        parts = [ref[s:s + n, :] for s, n in slabs]
        return parts[0] if len(parts) == 1 else jnp.concatenate(parts, axis=0)

    def gather_cls(ref, slabs):
        parts = [ref[0, s // CLS_LEN, s % CLS_LEN:s % CLS_LEN + n, lanes] for s, n in slabs]
        return parts[0] if len(parts) == 1 else jnp.concatenate(parts, axis=0)

    def scatter(ref, slabs, val):
        off = 0
        for s, n in slabs:
            ref[s:s + n, :] = val[off:off + n, :]
            off += n

    def phase_scores(items):
        scores = []
        for load_q, load_k, load_v, mask, slabs, has_state, finalize in items:
            qc = load_q()
            kc = load_k()
            for is_h in (is_a, is_b):
                qh = jnp.where(is_h, qc, zeros_row)
                s = lax.dot_general(qh, kc, (((1,), (1,)), ((), ())),
                                    preferred_element_type=F32) + mask_ref[:, mask]
                scores.append((s, jnp.max(s, axis=1, keepdims=True)))
        return scores

    def phase_update(items, scores):
        partial = []
        n = 0
        for load_q, load_k, load_v, mask, slabs, has_state, finalize in items:
            vc = load_v()
            for h, is_h in enumerate((is_a, is_b)):
                s, m_c = scores[n]
                n += 1
                if has_state:
                    m_prev = gather(state[h][0], slabs)
                    m_new = jnp.maximum(m_prev, m_c)
                else:
                    m_prev = None
                    m_new = jnp.broadcast_to(m_c, (ch, LANES))
                m_k = jnp.concatenate([m_new] * (s.shape[1] // LANES), axis=1)
                p = jnp.exp2(s - m_k)
                vh = jnp.where(is_h, vc, ones_row)
                pv = jnp.dot(p.astype(BF16), vh, preferred_element_type=F32)
                partial.append((m_prev, m_new, pv))
        n = 0
        for load_q, load_k, load_v, mask, slabs, has_state, finalize in items:
            res = []
            for h in range(2):
                m_prev, m_new, pv = partial[n]
                n += 1
                if has_state:
                    a_new = jnp.exp2(m_prev - m_new) * gather(state[h][1], slabs) + pv
                else:
                    a_new = pv
                res.append(a_new)
                if not finalize:
                    scatter(state[h][0], slabs, m_new)
                    scatter(state[h][1], slabs, a_new)
            if finalize:
                num = jnp.where(is_a, res[0], res[1])
                den = pltpu.roll(jnp.where(is_a, res[1], res[0]), HEAD_DIM, 1)
                scatter(fin, slabs, num / den)

    def block(slabs, kslabs, mask, has_state, finalize):
        return (lambda: gather_cls(q16_ref, slabs).astype(BF16),
                lambda: gather_cls(k16_ref, kslabs).astype(BF16),
                lambda: gather_cls(v16_ref, kslabs).astype(BF16), mask, slabs, has_state, finalize)

    order = []
    for i in range(SEQ // ch):
        slabs = [(r * CLS_LEN + per * i, per) for r in range(N_CLS)]
        if i > 0:
            kslabs = [(r * CLS_LEN + per * (i - 1), 2 * per) for r in range(N_CLS)]
            order.append(block(slabs, kslabs, mask_p1_full, False, False))
        else:
            order.append(block(slabs, slabs, mask_p1_first, False, False))

    quarter = ch // 4
    dil4 = []
    for r4 in range(4):
        for ic in range(SEQ // (4 * ch)):
            cls = [(4 * j + r4) * CLS_LEN for j in range(4)]
            slabs = [(c + quarter * ic, quarter) for c in cls]
            if ic > 0:
                kslabs = [(c + quarter * (ic - 1), 2 * quarter) for c in cls]
                dil4.append(block(slabs, kslabs, mask_p2_full, True, False))
            else:
                dil4.append(block(slabs, slabs, mask_p2_first, True, False))

    order += dil4

    for r in range(N_CLS):
        slabs = [(r * CLS_LEN, CLS_LEN)]
        order.append(block(slabs, slabs, mask_p3, True, True))

    assert (SEQ // ch) % ATT_GROUP == 0
    pending = None
    for i in range(0, len(order), ATT_GROUP):
        items = order[i:i + ATT_GROUP]
        scores = phase_scores(items)
        if pending is not None:
            phase_update(*pending)
        pending = (items, scores)
    phase_update(*pending)

    for i in range(SEQ // ch):
        slab = gather(fin, [(r * CLS_LEN + per * i, per) for r in range(N_CLS)])
        o_ref[0, i * ch:(i + 1) * ch, lanes] = jnp.dot(
            shuffle, slab.astype(BF16), preferred_element_type=F32).astype(o_ref.dtype)


def _attention(q16, k16, v16):
    assert CLS_LEN == ATT_CHUNK and SEQ % (4 * ATT_CHUNK) == 0
    width = ATT_PAIRS * LANES
    blk = pl.BlockSpec((1, SEQ, width), lambda b, h: (b, 0, h))
    cls_blk = pl.BlockSpec((1, N_CLS, CLS_LEN, width), lambda b, h: (b, 0, 0, h))
    f32_scr = pltpu.VMEM((SEQ, LANES), F32)
    masks = _attn_masks()
    return pl.pallas_call(
        _attn_kernel,
        grid=(BATCH, ATT_WIDTH // width),
        in_specs=[cls_blk, cls_blk, cls_blk, _const_spec(masks.shape)],
        out_specs=blk,
        out_shape=jax.ShapeDtypeStruct((BATCH, SEQ, ATT_WIDTH), BF16),
        scratch_shapes=[f32_scr] * (5 * ATT_PAIRS),
        compiler_params=pltpu.CompilerParams(
            dimension_semantics=("arbitrary", "arbitrary"), vmem_limit_bytes=VMEM_LIMIT),
        name="dilated_attention",
    )(q16, k16, v16, masks)


def _ssm_kernel(us_ref, perm_ref, lam_re_ref, lam_im_ref, wb_ref, wc_re_ref, wc_im_ref,
                d_ref, gw_ref, gb_ref, o_ref, xr, xi, st_re, st_im, y_tb, y_bt):
    tc = SSM_TC
    rows = tc * BATCH
    n_chunk = SSM_WIDTH // LANES
    cpc = N_STATE // n_chunk

    @pl.when(pl.program_id(0) == 0)
    def _():
        st_re[...] = jnp.zeros_like(st_re)
        st_im[...] = jnp.zeros_like(st_im)

    ub = us_ref[:, :, :SSM_WIDTH].reshape(rows, SSM_WIDTH)
    u_tb = jnp.dot(perm_ref[...], ub, preferred_element_type=F32).astype(BF16)
    def b_proj(c):
        cols = slice(c * cpc, (c + 1) * cpc)
        bu = jnp.dot(u_tb[:, c * LANES:(c + 1) * LANES], wb_ref[c], preferred_element_type=F32)
        xr[:, cols] = bu[:, :cpc]
        xi[:, cols] = bu[:, cpc:]

    def recurrence(c):
        cols = slice(c * cpc, (c + 1) * cpc)
        lr = lam_re_ref[:, cols]
        li = lam_im_ref[:, cols]
        sr = st_re[:, cols]
        si = st_im[:, cols]
        for t in range(tc):
            trow = slice(t * BATCH, (t + 1) * BATCH)
            sr, si = (lr * sr - li * si + xr[trow, cols], lr * si + li * sr + xi[trow, cols])
            xr[trow, cols] = sr
            xi[trow, cols] = si
        st_re[:, cols] = sr
        st_im[:, cols] = si

    def c_proj(c):
        cols = slice(c * cpc, (c + 1) * cpc)
        y_tb[c] = (
            jnp.dot(xr[:, cols].astype(BF16), wc_re_ref[c], preferred_element_type=F32)
            + jnp.dot(xi[:, cols].astype(BF16), wc_im_ref[c], preferred_element_type=F32))

    def skip_gelu(c):
        lanes = slice(c * LANES, (c + 1) * LANES)
        y_c = jnp.concatenate(
            [y_tb[c, pl.ds(b, tc, stride=BATCH), :] for b in range(BATCH)], axis=0)
        y_bt[:, lanes] = jax.nn.gelu(y_c + d_ref[:, lanes] * ub[:, lanes].astype(F32))

    b_proj(0)
    gate = _silu(us_ref[:, :, SSM_WIDTH:].reshape(rows, SSM_WIDTH).astype(F32))
    for c in range(n_chunk):
        if c + 1 < n_chunk:
            b_proj(c + 1)
        recurrence(c)
        if c > 0:
            skip_gelu(c - 1)
        c_proj(c)
    skip_gelu(n_chunk - 1)
    y = y_bt[...]
    z = jnp.dot(y.astype(BF16), gw_ref[...], preferred_element_type=F32) + gb_ref[...]
    out = y * _sigmoid(z) * gate
    o_ref[...] = out.reshape(BATCH, tc, SSM_WIDTH).astype(o_ref.dtype)


def _ssm(us3, params, i):
    tc = SSM_TC
    rows = tc * BATCH
    dst_row = np.arange(rows)
    perm = jnp.asarray((dst_row[:, None] % BATCH) * tc + dst_row[:, None] // BATCH
                       == dst_row[None, :], dtype=BF16)
    return pl.pallas_call(
        _ssm_kernel,
        grid=(SEQ // tc,),
        in_specs=[pl.BlockSpec((BATCH, tc, 2 * SSM_WIDTH), lambda t: (0, t, 0)),
                  _const_spec((rows, rows))] + [_layer_spec(p, i) for p in params],
        out_specs=pl.BlockSpec((BATCH, tc, SSM_WIDTH), lambda t: (0, t, 0)),
        out_shape=jax.ShapeDtypeStruct((BATCH, SEQ, SSM_WIDTH), BF16),
        scratch_shapes=[pltpu.VMEM((rows, N_STATE), F32), pltpu.VMEM((rows, N_STATE), F32),
                        pltpu.VMEM((BATCH, N_STATE), F32), pltpu.VMEM((BATCH, N_STATE), F32),
                        pltpu.VMEM((SSM_WIDTH // LANES, rows, LANES), F32),
                        pltpu.VMEM((rows, SSM_WIDTH), F32)],
        compiler_params=pltpu.CompilerParams(
            dimension_semantics=("arbitrary",), vmem_limit_bytes=VMEM_LIMIT),
        name="s5_branch",
    )(us3, perm, *params)


def _tail_kernel(layer_idx, x_ref, att_ref, g_ref, ssm_ref, w_eo_hbm, post_e_ref,
                 pre_ref, w_in_hbm, pw_hbm, ps_ref, w_out_hbm, post_ref, o_ref,
                 ext, tmp0, tmp1,
                 w_eo_ref, stage_eo, sem_eo, w_in_ref, stage_in, sem_in,
                 pw_ref, stage_pw, sem_pw, w_out_ref, stage_out, sem_out):
    tm = ODD_TILE
    hist = POOL_HIST
    total = hist + tm
    j = pl.program_id(1)

    @pl.when(_first_step())
    def _():
        _load_weights_bf16(layer_idx, [(w_eo_hbm, w_eo_ref, stage_eo, sem_eo),
                                       (w_in_hbm, w_in_ref, stage_in, sem_in),
                                       (pw_hbm, pw_ref, stage_pw, sem_pw),
                                       (w_out_hbm, w_out_ref, stage_out, sem_out)])

    @pl.when(j == 0)
    def _():
        ext[0:hist, :] = jnp.zeros((hist, POOL_WIDTH), F32)

    tmps = (tmp0, tmp1)
    sub = ODD_SUB
    for s in range(tm // sub):
        lo = hist + s * sub
        hi = lo + sub
        rows = slice(s * sub, (s + 1) * sub)
        half = sub // 2
        y_even = []
        for h in range(2):
            r = slice(s * sub + h * half, s * sub + (h + 1) * half)
            gated = (att_ref[r, :].astype(F32) * _silu(g_ref[r, :].astype(F32))).astype(BF16)
            y_even.append(
                jnp.dot(gated, w_eo_ref[:ATT_WIDTH, :], preferred_element_type=F32)
                + jnp.dot(ssm_ref[r, :], w_eo_ref[ATT_WIDTH:, :], preferred_element_type=F32))
        x_parts, hb_parts = [], []
        for h in range(2):
            r = slice(s * sub + h * half, s * sub + (h + 1) * half)
            x_h = x_ref[r, :] + _rms(y_even[h], post_e_ref[...])
            hb_h = _rms(x_h, pre_ref[...]).astype(BF16)
            ext[lo + h * half:lo + (h + 1) * half, :] = jnp.dot(
                hb_h, w_in_ref[:, :POOL_WIDTH], preferred_element_type=F32)
            x_parts.append(x_h)
            hb_parts.append(hb_h)
        x = jnp.concatenate(x_parts, axis=0)
        hb = jnp.concatenate(hb_parts, axis=0)
        t_idx = j * tm + s * sub + lax.broadcasted_iota(jnp.int32, (sub, 1), 0)
        inv_count = 1.0 / (t_idx + 1).astype(F32)
        y = None
        for g, w in enumerate(POOL_WINDOWS):
            cols = slice(g * POOL_GROUP, (g + 1) * POOL_GROUP)
            levels = int(math.log2(w))
            src_ref, src_cols = ext, cols
            for lev in range(levels):
                shift = 2 ** lev
                start = lo - 8 * (levels - 1 - lev)
                val = (src_ref[start:hi, src_cols] + src_ref[start - shift:hi - shift, src_cols])
                if lev < levels - 1:
                    dst = tmps[lev % 2]
                    dst[start:hi, :] = val
                    src_ref, src_cols = dst, slice(None)
            u_g = ext[lo:hi, cols]
            mixed = val * jnp.maximum(inv_count, 1.0 / w) - u_g
            yg = jnp.dot(mixed.astype(BF16), pw_ref[cols, :], preferred_element_type=F32)
            gate = jnp.dot(hb, w_in_ref[:, POOL_WIDTH + g * POOL_GROUP:
                                        POOL_WIDTH + (g + 1) * POOL_GROUP],
                           preferred_element_type=F32)
            yg = yg * ps_ref[:, cols] * _silu(gate)
            part = jnp.dot(yg.astype(BF16), w_out_ref[cols, :], preferred_element_type=F32)
            y = part if y is None else y + part
        o_ref[rows, :] = x + _rms(y, post_ref[...])
    ext[0:hist, :] = ext[tm:total, :]


def _even_tail_odd(x2, att2, g2, ssm2, w_even_out, pre, post, w_in, pool_w, pool_scale, w_out, i,
                   layer):
    tm = ODD_TILE
    nj = SEQ // tm
    row_spec = pl.BlockSpec((tm, D_MODEL), lambda b, j: (b * nj + j, 0))
    att_spec = pl.BlockSpec((tm, ATT_WIDTH), lambda b, j: (b * nj + j, 0))
    hbm = pl.BlockSpec(memory_space=pl.ANY)
    return pl.pallas_call(
        functools.partial(_tail_kernel, i),
        grid=(BATCH, nj),
        in_specs=[row_spec, att_spec, att_spec,
                  pl.BlockSpec((tm, SSM_WIDTH), lambda b, j: (b * nj + j, 0)),
                  hbm, _layer_spec(post, layer),
                  _layer_spec(pre, layer + 1), hbm,
                  hbm, _layer_spec(pool_scale, i),
                  hbm, _layer_spec(post, layer + 1)],
        out_specs=row_spec,
        out_shape=jax.ShapeDtypeStruct((BATCH * SEQ, D_MODEL), F32),
        scratch_shapes=[pltpu.VMEM((POOL_HIST + tm, POOL_WIDTH), F32),
                        pltpu.VMEM((POOL_HIST + tm, POOL_GROUP), F32),
                        pltpu.VMEM((POOL_HIST + tm, POOL_GROUP), F32)]
        + _weight_scratch(*w_even_out.shape[1:]) + _weight_scratch(*w_in.shape[1:])
        + _weight_scratch(*pool_w.shape[1:]) + _weight_scratch(*w_out.shape[1:]),
        compiler_params=pltpu.CompilerParams(
            dimension_semantics=("arbitrary", "arbitrary"), vmem_limit_bytes=VMEM_LIMIT),
        name="even_tail_odd_layer",
    )(x2, att2, g2, ssm2, w_even_out, post, pre, w_in, pool_w, pool_scale, w_out, post)


def _rope_tables():
    half = ROT_DIM // 2
    inv_freq = (np.float32(ROPE_THETA)
                ** (-np.arange(0, ROT_DIM, 2, dtype=np.float32) / np.float32(ROT_DIM)))
    ang = np.arange(SEQ, dtype=np.float32)[:, None] * inv_freq[None, :].astype(np.float32)
    cos, sin = np.cos(ang).astype(np.float32), np.sin(ang).astype(np.float32)
    zeros = np.zeros((SEQ, HEAD_DIM - ROT_DIM), np.float32)
    zeros_h = np.zeros((SEQ, half), np.float32)
    cos_h = np.concatenate([cos, cos, np.ones_like(zeros)], axis=1)
    sa_h = np.concatenate([-sin, zeros_h, zeros], axis=1)
    sb_h = np.concatenate([zeros_h, sin, zeros], axis=1)
    rep = LANES // HEAD_DIM
    pos = np.arange(SEQ).reshape(SEQ // ROW_TILE, ROW_TILE // N_CLS, N_CLS)
    order = pos.transpose(0, 2, 1).reshape(SEQ)
    return tuple(jnp.asarray(np.tile(t, (1, rep))[order]) for t in (cos_h, sa_h, sb_h))


def _ssm_params(a_re, a_im, log_dt, b_re, b_im, c_re, c_im):
    n_layer = a_re.shape[0]
    lam = lax.complex(a_re.astype(F32), a_im.astype(F32))
    dt = jnp.exp(log_dt.astype(F32))[..., None]
    lam_bar = jnp.exp(lam * dt)
    b_bar = ((lam_bar - 1.0) / lam)[..., None] * lax.complex(b_re.astype(F32), b_im.astype(F32))
    lam_re = jnp.broadcast_to(jnp.real(lam_bar).reshape(n_layer, 1, N_STATE),
                              (n_layer, BATCH, N_STATE))
    lam_im = jnp.broadcast_to(jnp.imag(lam_bar).reshape(n_layer, 1, N_STATE),
                              (n_layer, BATCH, N_STATE))
    gpc = LANES // SSM_GROUP
    n_chunk = SSM_GROUPS // gpc
    eye = jnp.eye(gpc, dtype=F32)

    def block_diag_in(m):
        m = m.reshape(n_layer, n_chunk, gpc, SSM_STATE, SSM_GROUP)
        return jnp.einsum('lcgph,gk->lcghkp', m, eye).reshape(
            n_layer, n_chunk, gpc * SSM_GROUP, gpc * SSM_STATE)

    def block_diag_out(m):
        m = m.reshape(n_layer, n_chunk, gpc, SSM_GROUP, SSM_STATE)
        return jnp.einsum('lcghp,gk->lcgpkh', m, eye).reshape(
            n_layer, n_chunk, gpc * SSM_STATE, gpc * SSM_GROUP)

    wb = jnp.concatenate([block_diag_in(jnp.real(b_bar)), block_diag_in(jnp.imag(b_bar))],
                         axis=3).astype(BF16)
    wc_re = block_diag_out(c_re.astype(F32)).astype(BF16)
    wc_im = block_diag_out(-c_im.astype(F32)).astype(BF16)
    return lam_re, lam_im, wb, wc_re, wc_im


def kernel(x, pre_norm, post_norm, even_w_in, even_w_out, ssm_a_re, ssm_a_im, ssm_log_dt,
           ssm_b_re, ssm_b_im, ssm_c_re, ssm_c_im, ssm_d, ssm_glu_w, ssm_glu_b,
           odd_w_in, pool_w, pool_scale, odd_w_out):
    cos_t, sa_t, sb_t = _rope_tables()
    pre = pre_norm.astype(F32).reshape(DEPTH, 1, D_MODEL)
    post = post_norm.astype(F32).reshape(DEPTH, 1, D_MODEL)
    even_w_in_f = even_w_in.astype(F32)
    even_w_out_f = even_w_out.astype(F32)
    odd_w_in_f = odd_w_in.astype(F32)
    odd_w_out_f = odd_w_out.astype(F32)
    pool_w_f = pool_w.astype(F32).reshape(-1, POOL_WIDTH, POOL_GROUP)
    pool_scale3 = pool_scale.astype(F32).reshape(-1, 1, POOL_WIDTH)
    ssm_params = _ssm_params(ssm_a_re, ssm_a_im, ssm_log_dt, ssm_b_re, ssm_b_im,
                             ssm_c_re, ssm_c_im) + (
        ssm_d.astype(F32).reshape(-1, 1, SSM_WIDTH), ssm_glu_w.astype(BF16),
        ssm_glu_b.astype(F32).reshape(-1, 1, SSM_WIDTH))
    x2 = x.reshape(BATCH * SEQ, D_MODEL)
    assert DEPTH % 2 == 0
    for layer in range(0, DEPTH, 2):
        i = layer // 2
        q16, k16, v16, g, us = _even_in(x2, pre, layer, even_w_in_f, i, cos_t, sa_t, sb_t)
        att = _attention(q16, k16, v16)
        ssm = _ssm(us.reshape(BATCH, SEQ, 2 * SSM_WIDTH), ssm_params, i)
        x2 = _even_tail_odd(x2, att.reshape(BATCH * SEQ, ATT_WIDTH), g,
                            ssm.reshape(BATCH * SEQ, SSM_WIDTH), even_w_out_f, pre, post,
                            odd_w_in_f, pool_w_f, pool_scale3, odd_w_out_f, i, layer)
    return x2.reshape(BATCH, SEQ, D_MODEL)
```

```python
import functools
import math

import jax
import jax.numpy as jnp
import numpy as np
from jax import lax
from jax.experimental import pallas as pl
from jax.experimental.pallas import tpu as pltpu

F32 = jnp.float32
BF16 = jnp.bfloat16

D_MODEL = 1024
BATCH = 8
SEQ = 2048
DEPTH = 4
HEAD_DIM = 64
ATT_WIDTH = 1024
ROT_DIM = 16
ROPE_THETA = 500000.0
SSM_WIDTH = 512
SSM_GROUP = 16
SSM_GROUPS = 32
SSM_STATE = 64
N_STATE = SSM_GROUPS * SSM_STATE
POOL_WIDTH = 2048
POOL_WINDOWS = (2, 4, 8, 16)
POOL_GROUP = 512
EVEN_IN = 4 * ATT_WIDTH + 2 * SSM_WIDTH
RMS_EPS = 1e-6
LOG2_E = math.log2(math.e)

LANES = 128
N_CLS = 16
CLS_LEN = SEQ // N_CLS
ATT_CHUNK = 128
ATT_GROUP = 4
ATT_PAIRS = 2
NEG_BIG = -1e30
VMEM_LIMIT = 56 * 1024 * 1024
WEIGHT_CHUNK_BYTES = 3 << 19
BF16_SUBLANES = 16

ROW_TILE = 512
SSM_TC = 128
SSM_SUB = 64
ODD_TILE = 512
ODD_SUB = 512
POOL_HIST = 32


def _sigmoid(x):
    return 0.5 + 0.5 * jnp.tanh(0.5 * x)


def _silu(x):
    h = 0.5 * x
    return h + h * jnp.tanh(h)


def _rms(x, gain):
    return x * lax.rsqrt(jnp.mean(x * x, axis=-1, keepdims=True) + RMS_EPS) * gain


def _const_spec(shape):
    nd = len(shape)
    return pl.BlockSpec(shape, lambda *_: (0,) * nd, pipeline_mode=pl.Buffered(1))


def _load_weights_bf16(layer, weights):
    def copy(w_hbm, stage, sem, k):
        slots, chunk = stage.shape[0], stage.shape[1]
        return pltpu.make_async_copy(
            w_hbm.at[layer, pl.ds(k * chunk, chunk), :], stage.at[k % slots], sem.at[k % slots])

    n_chunks = [dst.shape[0] // stage.shape[1] for _, dst, stage, _ in weights]
    for (w_hbm, dst, stage, sem), n in zip(weights, n_chunks):
        for k in range(min(stage.shape[0], n)):
            copy(w_hbm, stage, sem, k).start()
    for k in range(max(n_chunks)):
        for (w_hbm, dst, stage, sem), n in zip(weights, n_chunks):
            if k >= n:
                continue
            slots, chunk = stage.shape[0], stage.shape[1]
            copy(w_hbm, stage, sem, k).wait()
            dst[k * chunk:(k + 1) * chunk, :] = stage[k % slots].astype(BF16)
            if k + slots < n:
                copy(w_hbm, stage, sem, k + slots).start()


def _weight_scratch(rows, cols, slots=2):
    chunk = rows
    while chunk * cols * 4 > WEIGHT_CHUNK_BYTES:
        chunk //= 2
    assert rows % chunk == 0 and chunk % BF16_SUBLANES == 0
    return [pltpu.VMEM((rows, cols), BF16), pltpu.VMEM((slots, chunk, cols), F32),
            pltpu.SemaphoreType.DMA((slots,))]


def _first_step():
    return jnp.logical_and(pl.program_id(0) == 0, pl.program_id(1) == 0)


def _layer_spec(stacked, layer):
    shape = stacked.shape[1:]
    nd = len(shape)
    return pl.BlockSpec((None,) + tuple(shape), lambda *_: (layer,) + (0,) * nd,
                        pipeline_mode=pl.Buffered(1))


def _even_in_kernel(layer_idx, x_ref, gain_ref, w_hbm, cos_ref, sa_ref, sb_ref,
                    q16_ref, k16_ref, v16_ref, g_ref, us_ref, h_nat, h_cls, w_ref, stage, sem):
    tm = ROW_TILE
    per_cls = tm // N_CLS

    @pl.when(_first_step())
    def _():
        _load_weights_bf16(layer_idx, [(w_hbm, w_ref, stage, sem)])

    def proj(lhs, lo, width):
        return jnp.dot(lhs, w_ref[:, lo:lo + width], preferred_element_type=F32)

    half = tm // 2
    for hh in range(2):
        r = slice(hh * half, (hh + 1) * half)
        h = _rms(x_ref[r, :], gain_ref[...])
        for c in range(D_MODEL // LANES):
            h_nat[c, r, :] = h[:, c * LANES:(c + 1) * LANES]
        hb = h.astype(BF16)
        g_ref[r, :] = proj(hb, 3 * ATT_WIDTH, ATT_WIDTH).astype(g_ref.dtype)
        us_ref[r, :] = proj(hb, 4 * ATT_WIDTH, 2 * SSM_WIDTH).astype(us_ref.dtype)
    for c in range(D_MODEL // LANES):
        for cls in range(N_CLS):
            h_cls[cls * per_cls:(cls + 1) * per_cls, c * LANES:(c + 1) * LANES] = (
                h_nat[c, pl.ds(cls, per_cls, stride=N_CLS), :].astype(BF16))
    hb_cls = h_cls[...]
    cos = cos_ref[...]
    sa = sa_ref[...]
    sb = sb_ref[...]

    def rope_store(dst, y, scale):
        for c in range(ATT_WIDTH // LANES):
            yc = y[:, c * LANES:(c + 1) * LANES]
            r = (yc * cos + pltpu.roll(yc, LANES - ROT_DIM // 2, 1) * sa
                 + pltpu.roll(yc, ROT_DIM // 2, 1) * sb)
            dst[0, :, :, c * LANES:(c + 1) * LANES] = (
                (r * scale).astype(dst.dtype).reshape(N_CLS, per_cls, LANES))

    rope_store(q16_ref, proj(hb_cls, 0, ATT_WIDTH), HEAD_DIM ** -0.5 * LOG2_E)
    rope_store(k16_ref, proj(hb_cls, ATT_WIDTH, ATT_WIDTH), 1.0)
    v16_ref[0] = proj(hb_cls, 2 * ATT_WIDTH, ATT_WIDTH).astype(v16_ref.dtype).reshape(
        N_CLS, per_cls, ATT_WIDTH)


def _even_in(x2, gains, layer, w_in, i, cos_t, sa_t, sb_t):
    tm = ROW_TILE
    nj = SEQ // tm
    row_spec = pl.BlockSpec((tm, D_MODEL), lambda b, j: (b * nj + j, 0))
    tab_spec = pl.BlockSpec((tm, LANES), lambda b, j: (j, 0))
    att_spec = pl.BlockSpec((tm, ATT_WIDTH), lambda b, j: (b * nj + j, 0))
    cls_spec = pl.BlockSpec((1, N_CLS, tm // N_CLS, ATT_WIDTH), lambda b, j: (b, 0, j, 0))
    att_shape = jax.ShapeDtypeStruct((BATCH * SEQ, ATT_WIDTH), BF16)
    cls_shape = jax.ShapeDtypeStruct((BATCH, N_CLS, CLS_LEN, ATT_WIDTH), F32)
    return pl.pallas_call(
        functools.partial(_even_in_kernel, i),
        grid=(BATCH, nj),
        in_specs=[row_spec, _layer_spec(gains, layer), pl.BlockSpec(memory_space=pl.ANY),
                  tab_spec, tab_spec, tab_spec],
        out_specs=[cls_spec, cls_spec, cls_spec, att_spec, att_spec],
        out_shape=[cls_shape, jax.ShapeDtypeStruct(cls_shape.shape, BF16),
                   jax.ShapeDtypeStruct(cls_shape.shape, BF16), att_shape, att_shape],
        scratch_shapes=[pltpu.VMEM((D_MODEL // LANES, tm, LANES), F32),
                        pltpu.VMEM((tm, D_MODEL), BF16)]
        + _weight_scratch(D_MODEL, EVEN_IN, slots=4),
        compiler_params=pltpu.CompilerParams(
            dimension_semantics=("arbitrary", "arbitrary"), vmem_limit_bytes=VMEM_LIMIT),
        name="even_in_proj",
    )(x2, gains, w_in, cos_t, sa_t, sb_t)


def _attn_masks():
    ch = ATT_CHUNK
    per = ch // N_CLS
    quarter = ch // 4
    row = np.arange(ch)
    col1 = np.arange(ch)
    col2 = np.arange(2 * ch)
    tau = N_CLS * (row % per) + row // per
    kk = N_CLS * (col2 % (2 * per)) + col2 // (2 * per)
    p1_full = np.where(kk[None, :] < ch, kk[None, :] >= tau[:, None],
                       (kk[None, :] - ch) <= tau[:, None])
    p1_first = (N_CLS * (col1 % per) + col1 // per)[None, :] <= tau[:, None]
    off = 4 * (row % quarter) + row // quarter
    kk2 = 4 * (col2 % (2 * quarter)) + col2 // (2 * quarter)
    p2_full = (kk2[None, :] >= off[:, None]) & (kk2[None, :] <= off[:, None] + ch)
    p2_first = (4 * (col1 % quarter) + col1 // quarter)[None, :] <= off[:, None]
    p3 = col1[None, :] <= row[:, None]
    valid = np.concatenate([p1_full, p1_first, p2_full, p2_first, p3], axis=1)
    return jnp.asarray(np.where(valid, 0.0, NEG_BIG), dtype=F32)


def _attn_kernel(q16_ref, k16_ref, v16_ref, mask_ref, o_ref, *scratch):
    per_pair = len(scratch) // ATT_PAIRS
    for pair in range(ATT_PAIRS):
        _attn_head_pair(q16_ref, k16_ref, v16_ref, mask_ref, o_ref,
                        slice(pair * LANES, (pair + 1) * LANES),
                        *scratch[pair * per_pair:(pair + 1) * per_pair])


def _attn_head_pair(q16_ref, k16_ref, v16_ref, mask_ref, o_ref, lanes,
                    m_a, m_b, acc_a, acc_b, fin):
    ch = ATT_CHUNK
    per = ch // N_CLS
    lane = lax.broadcasted_iota(jnp.int32, (1, LANES), 1)
    is_a = lane < HEAD_DIM
    is_b = jnp.logical_not(is_a)
    state = ((m_a, acc_a), (m_b, acc_b))
    ones_row = jnp.ones((1, LANES), BF16)
    zeros_row = jnp.zeros((1, LANES), BF16)
    row = lax.broadcasted_iota(jnp.int32, (ch, ch), 0)
    col = lax.broadcasted_iota(jnp.int32, (ch, ch), 1)
    shuffle = jnp.where(col == per * (row % N_CLS) + row // N_CLS, 1.0, 0.0).astype(BF16)
    mask_p1_full = slice(0, 2 * ch)
    mask_p1_first = slice(2 * ch, 3 * ch)
    mask_p2_full = slice(3 * ch, 5 * ch)
    mask_p2_first = slice(5 * ch, 6 * ch)
    mask_p3 = slice(6 * ch, 7 * ch)

    def gather(ref, slabs):
        parts = [ref[s:s + n, :] for s, n in slabs]
        return parts[0] if len(parts) == 1 else jnp.concatenate(parts, axis=0)

    def gather_cls(ref, slabs):
        parts = [ref[0, s // CLS_LEN, s % CLS_LEN:s % CLS_LEN + n, lanes] for s, n in slabs]
        return parts[0] if len(parts) == 1 else jnp.concatenate(parts, axis=0)

    def scatter(ref, slabs, val):
        off = 0
        for s, n in slabs:
            ref[s:s + n, :] = val[off:off + n, :]
            off += n

    def phase_scores(items):
        scores = []
        for load_q, load_k, load_v, mask, slabs, has_state, finalize in items:
            qc = load_q()
            kc = load_k()
            for is_h in (is_a, is_b):
                qh = jnp.where(is_h, qc, zeros_row)
                s = lax.dot_general(qh, kc, (((1,), (1,)), ((), ())),
                                    preferred_element_type=F32) + mask_ref[:, mask]
                scores.append((s, jnp.max(s, axis=1, keepdims=True)))
        return scores

    def phase_update(items, scores):
        partial = []
        n = 0
        for load_q, load_k, load_v, mask, slabs, has_state, finalize in items:
            vc = load_v()
            for h, is_h in enumerate((is_a, is_b)):
                s, m_c = scores[n]
                n += 1
                if has_state:
                    m_prev = gather(state[h][0], slabs)
                    m_new = jnp.maximum(m_prev, m_c)
                else:
                    m_prev = None
                    m_new = jnp.broadcast_to(m_c, (ch, LANES))
                m_k = jnp.concatenate([m_new] * (s.shape[1] // LANES), axis=1)
                p = jnp.exp2(s - m_k)
                vh = jnp.where(is_h, vc, ones_row)
                pv = jnp.dot(p.astype(BF16), vh, preferred_element_type=F32)
                partial.append((m_prev, m_new, pv))
        n = 0
        for load_q, load_k, load_v, mask, slabs, has_state, finalize in items:
            res = []
            for h in range(2):
                m_prev, m_new, pv = partial[n]
                n += 1
                if has_state:
                    a_new = jnp.exp2(m_prev - m_new) * gather(state[h][1], slabs) + pv
                else:
                    a_new = pv
                res.append(a_new)
                if not finalize:
                    scatter(state[h][0], slabs, m_new)
                    scatter(state[h][1], slabs, a_new)
            if finalize:
                num = jnp.where(is_a, res[0], res[1])
                den = pltpu.roll(jnp.where(is_a, res[1], res[0]), HEAD_DIM, 1)
                scatter(fin, slabs, num / den)

    def block(slabs, kslabs, mask, has_state, finalize):
        return (lambda: gather_cls(q16_ref, slabs).astype(BF16),
                lambda: gather_cls(k16_ref, kslabs).astype(BF16),
                lambda: gather_cls(v16_ref, kslabs).astype(BF16), mask, slabs, has_state, finalize)

    order = []
    for i in range(SEQ // ch):
        slabs = [(r * CLS_LEN + per * i, per) for r in range(N_CLS)]
        if i > 0:
            kslabs = [(r * CLS_LEN + per * (i - 1), 2 * per) for r in range(N_CLS)]
            order.append(block(slabs, kslabs, mask_p1_full, False, False))
        else:
            order.append(block(slabs, slabs, mask_p1_first, False, False))

    quarter = ch // 4
    dil4 = []
    for r4 in range(4):
        for ic in range(SEQ // (4 * ch)):
            cls = [(4 * j + r4) * CLS_LEN for j in range(4)]
            slabs = [(c + quarter * ic, quarter) for c in cls]
            if ic > 0:
                kslabs = [(c + quarter * (ic - 1), 2 * quarter) for c in cls]
                dil4.append(block(slabs, kslabs, mask_p2_full, True, False))
            else:
                dil4.append(block(slabs, slabs, mask_p2_first, True, False))

    order += dil4

    for r in range(N_CLS):
        slabs = [(r * CLS_LEN, CLS_LEN)]
        order.append(block(slabs, slabs, mask_p3, True, True))

    assert (SEQ // ch) % ATT_GROUP == 0
    pending = None
    for i in range(0, len(order), ATT_GROUP):
        items = order[i:i + ATT_GROUP]
        scores = phase_scores(items)
        if pending is not None:
            phase_update(*pending)
        pending = (items, scores)
    phase_update(*pending)

    for i in range(SEQ // ch):
        slab = gather(fin, [(r * CLS_LEN + per * i, per) for r in range(N_CLS)])
        o_ref[0, i * ch:(i + 1) * ch, lanes] = jnp.dot(
            shuffle, slab.astype(BF16), preferred_element_type=F32).astype(o_ref.dtype)


def _attention(q16, k16, v16):
    assert CLS_LEN == ATT_CHUNK and SEQ % (4 * ATT_CHUNK) == 0
    width = ATT_PAIRS * LANES
    blk = pl.BlockSpec((1, SEQ, width), lambda b, h: (b, 0, h))
    cls_blk = pl.BlockSpec((1, N_CLS, CLS_LEN, width), lambda b, h: (b, 0, 0, h))
    f32_scr = pltpu.VMEM((SEQ, LANES), F32)
    masks = _attn_masks()
    return pl.pallas_call(
        _attn_kernel,
        grid=(BATCH, ATT_WIDTH // width),
        in_specs=[cls_blk, cls_blk, cls_blk, _const_spec(masks.shape)],
        out_specs=blk,
        out_shape=jax.ShapeDtypeStruct((BATCH, SEQ, ATT_WIDTH), BF16),
        scratch_shapes=[f32_scr] * (5 * ATT_PAIRS),
        compiler_params=pltpu.CompilerParams(
            dimension_semantics=("arbitrary", "arbitrary"), vmem_limit_bytes=VMEM_LIMIT),
        name="dilated_attention",
    )(q16, k16, v16, masks)


def _ssm_kernel(us_ref, perm_ref, lam_re_ref, lam_im_ref, wb_ref, wc_re_ref, wc_im_ref,
                d_ref, gw_ref, gb_ref, o_ref, st_re, st_im, *scratch):
    @pl.when(pl.program_id(0) == 0)
    def _():
        st_re[...] = jnp.zeros_like(st_re)
        st_im[...] = jnp.zeros_like(st_im)

    n_sub = SSM_TC // SSM_SUB
    per_sub = len(scratch) // n_sub
    for sub in range(n_sub):
        _ssm_sub_chunk(sub * SSM_SUB, us_ref, perm_ref, lam_re_ref, lam_im_ref, wb_ref, wc_re_ref,
                       wc_im_ref, d_ref, gw_ref, gb_ref, o_ref, st_re, st_im,
                       *scratch[sub * per_sub:(sub + 1) * per_sub])


def _ssm_sub_chunk(t0, us_ref, perm_ref, lam_re_ref, lam_im_ref, wb_ref, wc_re_ref, wc_im_ref,
                   d_ref, gw_ref, gb_ref, o_ref, st_re, st_im, xr, xi, y_tb, y_bt):
    tc = SSM_SUB
    rows = tc * BATCH
    n_chunk = SSM_WIDTH // LANES
    cpc = N_STATE // n_chunk

    ub = us_ref[:, t0:t0 + tc, :SSM_WIDTH].reshape(rows, SSM_WIDTH)
    u_tb = jnp.dot(perm_ref[...], ub, preferred_element_type=F32).astype(BF16)
    def b_proj(c):
        cols = slice(c * cpc, (c + 1) * cpc)
        bu = jnp.dot(u_tb[:, c * LANES:(c + 1) * LANES], wb_ref[c], preferred_element_type=F32)
        xr[:, cols] = bu[:, :cpc]
        xi[:, cols] = bu[:, cpc:]

    def recurrence(c):
        cols = slice(c * cpc, (c + 1) * cpc)
        lr = lam_re_ref[:, cols]
        li = lam_im_ref[:, cols]
        sr = st_re[:, cols]
        si = st_im[:, cols]
        for t in range(tc):
            trow = slice(t * BATCH, (t + 1) * BATCH)
            sr, si = (lr * sr - li * si + xr[trow, cols], lr * si + li * sr + xi[trow, cols])
            xr[trow, cols] = sr
            xi[trow, cols] = si
        st_re[:, cols] = sr
        st_im[:, cols] = si

    def c_proj(c):
        cols = slice(c * cpc, (c + 1) * cpc)
        y_tb[c] = (
            jnp.dot(xr[:, cols].astype(BF16), wc_re_ref[c], preferred_element_type=F32)
            + jnp.dot(xi[:, cols].astype(BF16), wc_im_ref[c], preferred_element_type=F32))

    def skip_gelu(c):
        lanes = slice(c * LANES, (c + 1) * LANES)
        y_c = jnp.concatenate(
            [y_tb[c, pl.ds(b, tc, stride=BATCH), :] for b in range(BATCH)], axis=0)
        y_bt[:, lanes] = jax.nn.gelu(y_c + d_ref[:, lanes] * ub[:, lanes].astype(F32))

    b_proj(0)
    gate = _silu(us_ref[:, t0:t0 + tc, SSM_WIDTH:].reshape(rows, SSM_WIDTH).astype(F32))
    for c in range(n_chunk):
        if c + 1 < n_chunk:
            b_proj(c + 1)
        recurrence(c)
        if c > 0:
            skip_gelu(c - 1)
        c_proj(c)
    skip_gelu(n_chunk - 1)
    y = y_bt[...]
    z = jnp.dot(y.astype(BF16), gw_ref[...], preferred_element_type=F32) + gb_ref[...]
    out = y * _sigmoid(z) * gate
    o_ref[:, t0:t0 + tc, :] = out.reshape(BATCH, tc, SSM_WIDTH).astype(o_ref.dtype)


def _ssm(us3, params, i):
    tc = SSM_TC
    rows = SSM_SUB * BATCH
    dst_row = np.arange(rows)
    perm = jnp.asarray((dst_row[:, None] % BATCH) * SSM_SUB + dst_row[:, None] // BATCH
                       == dst_row[None, :], dtype=BF16)
    sub_scratch = [pltpu.VMEM((rows, N_STATE), F32), pltpu.VMEM((rows, N_STATE), F32),
                   pltpu.VMEM((SSM_WIDTH // LANES, rows, LANES), F32),
                   pltpu.VMEM((rows, SSM_WIDTH), F32)]
    return pl.pallas_call(
        _ssm_kernel,
        grid=(SEQ // tc,),
        in_specs=[pl.BlockSpec((BATCH, tc, 2 * SSM_WIDTH), lambda t: (0, t, 0)),
                  _const_spec((rows, rows))] + [_layer_spec(p, i) for p in params],
        out_specs=pl.BlockSpec((BATCH, tc, SSM_WIDTH), lambda t: (0, t, 0)),
        out_shape=jax.ShapeDtypeStruct((BATCH, SEQ, SSM_WIDTH), BF16),
        scratch_shapes=[pltpu.VMEM((BATCH, N_STATE), F32), pltpu.VMEM((BATCH, N_STATE), F32)]
        + sub_scratch * (SSM_TC // SSM_SUB),
        compiler_params=pltpu.CompilerParams(
            dimension_semantics=("arbitrary",), vmem_limit_bytes=VMEM_LIMIT),
        name="s5_branch",
    )(us3, perm, *params)


def _tail_kernel(layer_idx, x_ref, att_ref, g_ref, ssm_ref, w_eo_hbm, post_e_ref,
                 pre_ref, w_in_hbm, pw_hbm, ps_ref, w_out_hbm, post_ref, o_ref,
                 ext, tmp0, tmp1,
                 w_eo_ref, stage_eo, sem_eo, w_in_ref, stage_in, sem_in,
                 pw_ref, stage_pw, sem_pw, w_out_ref, stage_out, sem_out):
    tm = ODD_TILE
    hist = POOL_HIST
    total = hist + tm
    j = pl.program_id(1)

    @pl.when(_first_step())
    def _():
        _load_weights_bf16(layer_idx, [(w_eo_hbm, w_eo_ref, stage_eo, sem_eo),
                                       (w_in_hbm, w_in_ref, stage_in, sem_in),
                                       (pw_hbm, pw_ref, stage_pw, sem_pw),
                                       (w_out_hbm, w_out_ref, stage_out, sem_out)])

    @pl.when(j == 0)
    def _():
        ext[0:hist, :] = jnp.zeros((hist, POOL_WIDTH), F32)

    tmps = (tmp0, tmp1)
    sub = ODD_SUB
    for s in range(tm // sub):
        lo = hist + s * sub
        hi = lo + sub
        rows = slice(s * sub, (s + 1) * sub)
        half = sub // 2
        y_even = []
        for h in range(2):
            r = slice(s * sub + h * half, s * sub + (h + 1) * half)
            gated = (att_ref[r, :].astype(F32) * _silu(g_ref[r, :].astype(F32))).astype(BF16)
            y_even.append(
                jnp.dot(gated, w_eo_ref[:ATT_WIDTH, :], preferred_element_type=F32)
                + jnp.dot(ssm_ref[r, :], w_eo_ref[ATT_WIDTH:, :], preferred_element_type=F32))
        x_parts, hb_parts = [], []
        for h in range(2):
            r = slice(s * sub + h * half, s * sub + (h + 1) * half)
            x_h = x_ref[r, :] + _rms(y_even[h], post_e_ref[...])
            hb_h = _rms(x_h, pre_ref[...]).astype(BF16)
            ext[lo + h * half:lo + (h + 1) * half, :] = jnp.dot(
                hb_h, w_in_ref[:, :POOL_WIDTH], preferred_element_type=F32)
            x_parts.append(x_h)
            hb_parts.append(hb_h)
        x = jnp.concatenate(x_parts, axis=0)
        hb = jnp.concatenate(hb_parts, axis=0)
        t_idx = j * tm + s * sub + lax.broadcasted_iota(jnp.int32, (sub, 1), 0)
        inv_count = 1.0 / (t_idx + 1).astype(F32)
        y = None
        for g, w in enumerate(POOL_WINDOWS):
            cols = slice(g * POOL_GROUP, (g + 1) * POOL_GROUP)
            levels = int(math.log2(w))
            src_ref, src_cols = ext, cols
            for lev in range(levels):
                shift = 2 ** lev
                start = lo - 8 * (levels - 1 - lev)
                val = (src_ref[start:hi, src_cols] + src_ref[start - shift:hi - shift, src_cols])
                if lev < levels - 1:
                    dst = tmps[lev % 2]
                    dst[start:hi, :] = val
                    src_ref, src_cols = dst, slice(None)
            u_g = ext[lo:hi, cols]
            mixed = val * jnp.maximum(inv_count, 1.0 / w) - u_g
            yg = jnp.dot(mixed.astype(BF16), pw_ref[cols, :], preferred_element_type=F32)
            gate = jnp.dot(hb, w_in_ref[:, POOL_WIDTH + g * POOL_GROUP:
                                        POOL_WIDTH + (g + 1) * POOL_GROUP],
                           preferred_element_type=F32)
            yg = yg * ps_ref[:, cols] * _silu(gate)
            part = jnp.dot(yg.astype(BF16), w_out_ref[cols, :], preferred_element_type=F32)
            y = part if y is None else y + part
        o_ref[rows, :] = x + _rms(y, post_ref[...])
    ext[0:hist, :] = ext[tm:total, :]


def _even_tail_odd(x2, att2, g2, ssm2, w_even_out, pre, post, w_in, pool_w, pool_scale, w_out, i,
                   layer):
    tm = ODD_TILE
    nj = SEQ // tm
    row_spec = pl.BlockSpec((tm, D_MODEL), lambda b, j: (b * nj + j, 0))
    att_spec = pl.BlockSpec((tm, ATT_WIDTH), lambda b, j: (b * nj + j, 0))
    hbm = pl.BlockSpec(memory_space=pl.ANY)
    return pl.pallas_call(
        functools.partial(_tail_kernel, i),
        grid=(BATCH, nj),
        in_specs=[row_spec, att_spec, att_spec,
                  pl.BlockSpec((tm, SSM_WIDTH), lambda b, j: (b * nj + j, 0)),
                  hbm, _layer_spec(post, layer),
                  _layer_spec(pre, layer + 1), hbm,
                  hbm, _layer_spec(pool_scale, i),
                  hbm, _layer_spec(post, layer + 1)],
        out_specs=row_spec,
        out_shape=jax.ShapeDtypeStruct((BATCH * SEQ, D_MODEL), F32),
        scratch_shapes=[pltpu.VMEM((POOL_HIST + tm, POOL_WIDTH), F32),
                        pltpu.VMEM((POOL_HIST + tm, POOL_GROUP), F32),
                        pltpu.VMEM((POOL_HIST + tm, POOL_GROUP), F32)]
        + _weight_scratch(*w_even_out.shape[1:]) + _weight_scratch(*w_in.shape[1:])
        + _weight_scratch(*pool_w.shape[1:]) + _weight_scratch(*w_out.shape[1:]),
        compiler_params=pltpu.CompilerParams(
            dimension_semantics=("arbitrary", "arbitrary"), vmem_limit_bytes=VMEM_LIMIT),
        name="even_tail_odd_layer",
    )(x2, att2, g2, ssm2, w_even_out, post, pre, w_in, pool_w, pool_scale, w_out, post)


def _rope_tables():
    half = ROT_DIM // 2
    inv_freq = (np.float32(ROPE_THETA)
                ** (-np.arange(0, ROT_DIM, 2, dtype=np.float32) / np.float32(ROT_DIM)))
    ang = np.arange(SEQ, dtype=np.float32)[:, None] * inv_freq[None, :].astype(np.float32)
    cos, sin = np.cos(ang).astype(np.float32), np.sin(ang).astype(np.float32)
    zeros = np.zeros((SEQ, HEAD_DIM - ROT_DIM), np.float32)
    zeros_h = np.zeros((SEQ, half), np.float32)
    cos_h = np.concatenate([cos, cos, np.ones_like(zeros)], axis=1)
    sa_h = np.concatenate([-sin, zeros_h, zeros], axis=1)
    sb_h = np.concatenate([zeros_h, sin, zeros], axis=1)
    rep = LANES // HEAD_DIM
    pos = np.arange(SEQ).reshape(SEQ // ROW_TILE, ROW_TILE // N_CLS, N_CLS)
    order = pos.transpose(0, 2, 1).reshape(SEQ)
    return tuple(jnp.asarray(np.tile(t, (1, rep))[order]) for t in (cos_h, sa_h, sb_h))


def _ssm_params(a_re, a_im, log_dt, b_re, b_im, c_re, c_im):
    n_layer = a_re.shape[0]
    lam = lax.complex(a_re.astype(F32), a_im.astype(F32))
    dt = jnp.exp(log_dt.astype(F32))[..., None]
    lam_bar = jnp.exp(lam * dt)
    b_bar = ((lam_bar - 1.0) / lam)[..., None] * lax.complex(b_re.astype(F32), b_im.astype(F32))
    lam_re = jnp.broadcast_to(jnp.real(lam_bar).reshape(n_layer, 1, N_STATE),
                              (n_layer, BATCH, N_STATE))
    lam_im = jnp.broadcast_to(jnp.imag(lam_bar).reshape(n_layer, 1, N_STATE),
                              (n_layer, BATCH, N_STATE))
    gpc = LANES // SSM_GROUP
    n_chunk = SSM_GROUPS // gpc
    eye = jnp.eye(gpc, dtype=F32)

    def block_diag_in(m):
        m = m.reshape(n_layer, n_chunk, gpc, SSM_STATE, SSM_GROUP)
        return jnp.einsum('lcgph,gk->lcghkp', m, eye).reshape(
            n_layer, n_chunk, gpc * SSM_GROUP, gpc * SSM_STATE)

    def block_diag_out(m):
        m = m.reshape(n_layer, n_chunk, gpc, SSM_GROUP, SSM_STATE)
        return jnp.einsum('lcghp,gk->lcgpkh', m, eye).reshape(
            n_layer, n_chunk, gpc * SSM_STATE, gpc * SSM_GROUP)

    wb = jnp.concatenate([block_diag_in(jnp.real(b_bar)), block_diag_in(jnp.imag(b_bar))],
                         axis=3).astype(BF16)
    wc_re = block_diag_out(c_re.astype(F32)).astype(BF16)
    wc_im = block_diag_out(-c_im.astype(F32)).astype(BF16)
    return lam_re, lam_im, wb, wc_re, wc_im


def kernel(x, pre_norm, post_norm, even_w_in, even_w_out, ssm_a_re, ssm_a_im, ssm_log_dt,
           ssm_b_re, ssm_b_im, ssm_c_re, ssm_c_im, ssm_d, ssm_glu_w, ssm_glu_b,
           odd_w_in, pool_w, pool_scale, odd_w_out):
    cos_t, sa_t, sb_t = _rope_tables()
    pre = pre_norm.astype(F32).reshape(DEPTH, 1, D_MODEL)
    post = post_norm.astype(F32).reshape(DEPTH, 1, D_MODEL)
    even_w_in_f = even_w_in.astype(F32)
    even_w_out_f = even_w_out.astype(F32)
    odd_w_in_f = odd_w_in.astype(F32)
    odd_w_out_f = odd_w_out.astype(F32)
    pool_w_f = pool_w.astype(F32).reshape(-1, POOL_WIDTH, POOL_GROUP)
    pool_scale3 = pool_scale.astype(F32).reshape(-1, 1, POOL_WIDTH)
    ssm_params = _ssm_params(ssm_a_re, ssm_a_im, ssm_log_dt, ssm_b_re, ssm_b_im,
                             ssm_c_re, ssm_c_im) + (
        ssm_d.astype(F32).reshape(-1, 1, SSM_WIDTH), ssm_glu_w.astype(BF16),
        ssm_glu_b.astype(F32).reshape(-1, 1, SSM_WIDTH))
    x2 = x.reshape(BATCH * SEQ, D_MODEL)
    assert DEPTH % 2 == 0
    for layer in range(0, DEPTH, 2):
        i = layer // 2
        q16, k16, v16, g, us = _even_in(x2, pre, layer, even_w_in_f, i, cos_t, sa_t, sb_t)
        att = _attention(q16, k16, v16)
        ssm = _ssm(us.reshape(BATCH, SEQ, 2 * SSM_WIDTH), ssm_params, i)
        x2 = _even_tail_odd(x2, att.reshape(BATCH * SEQ, ATT_WIDTH), g,
                            ssm.reshape(BATCH * SEQ, SSM_WIDTH), even_w_out_f, pre, post,
                            odd_w_in_f, pool_w_f, pool_scale3, odd_w_out_f, i, layer)
    return x2.reshape(BATCH, SEQ, D_MODEL)
```

```python
import functools
import math

import jax
import jax.numpy as jnp
import numpy as np
from jax import lax
from jax.experimental import pallas as pl
from jax.experimental.pallas import tpu as pltpu

F32 = jnp.float32
BF16 = jnp.bfloat16

D_MODEL = 1024
BATCH = 8
SEQ = 2048
DEPTH = 4
HEAD_DIM = 64
ATT_WIDTH = 1024
ROT_DIM = 16
ROPE_THETA = 500000.0
SSM_WIDTH = 512
SSM_GROUP = 16
SSM_GROUPS = 32
SSM_STATE = 64
N_STATE = SSM_GROUPS * SSM_STATE
POOL_WIDTH = 2048
POOL_WINDOWS = (2, 4, 8, 16)
POOL_GROUP = 512
EVEN_IN = 4 * ATT_WIDTH + 2 * SSM_WIDTH
RMS_EPS = 1e-6
LOG2_E = math.log2(math.e)

LANES = 128
N_CLS = 16
CLS_LEN = SEQ // N_CLS
ATT_CHUNK = 128
ATT_GROUP = 4
ATT_PAIRS = 2
NEG_BIG = -1e30
VMEM_LIMIT = 56 * 1024 * 1024
WEIGHT_CHUNK_BYTES = 3 << 19
BF16_SUBLANES = 16

ROW_TILE = 512
SSM_TC = 256
SSM_SUB = 64
ODD_TILE = 512
ODD_SUB = 512
POOL_HIST = 32


def _sigmoid(x):
    return 0.5 + 0.5 * jnp.tanh(0.5 * x)


def _silu(x):
    h = 0.5 * x
    return h + h * jnp.tanh(h)


def _rms(x, gain):
    return x * lax.rsqrt(jnp.mean(x * x, axis=-1, keepdims=True) + RMS_EPS) * gain


def _const_spec(shape):
    nd = len(shape)
    return pl.BlockSpec(shape, lambda *_: (0,) * nd, pipeline_mode=pl.Buffered(1))


def _load_weights_bf16(layer, weights):
    def copy(w_hbm, stage, sem, k):
        slots, chunk = stage.shape[0], stage.shape[1]
        return pltpu.make_async_copy(
            w_hbm.at[layer, pl.ds(k * chunk, chunk), :], stage.at[k % slots], sem.at[k % slots])

    n_chunks = [dst.shape[0] // stage.shape[1] for _, dst, stage, _ in weights]
    for (w_hbm, dst, stage, sem), n in zip(weights, n_chunks):
        for k in range(min(stage.shape[0], n)):
            copy(w_hbm, stage, sem, k).start()
    for k in range(max(n_chunks)):
        for (w_hbm, dst, stage, sem), n in zip(weights, n_chunks):
            if k >= n:
                continue
            slots, chunk = stage.shape[0], stage.shape[1]
            copy(w_hbm, stage, sem, k).wait()
            dst[k * chunk:(k + 1) * chunk, :] = stage[k % slots].astype(BF16)
            if k + slots < n:
                copy(w_hbm, stage, sem, k + slots).start()


def _weight_scratch(rows, cols, slots=2):
    chunk = rows
    while chunk * cols * 4 > WEIGHT_CHUNK_BYTES:
        chunk //= 2
    assert rows % chunk == 0 and chunk % BF16_SUBLANES == 0
    return [pltpu.VMEM((rows, cols), BF16), pltpu.VMEM((slots, chunk, cols), F32),
            pltpu.SemaphoreType.DMA((slots,))]


def _first_step():
    return jnp.logical_and(pl.program_id(0) == 0, pl.program_id(1) == 0)


def _layer_spec(stacked, layer):
    shape = stacked.shape[1:]
    nd = len(shape)
    return pl.BlockSpec((None,) + tuple(shape), lambda *_: (layer,) + (0,) * nd,
                        pipeline_mode=pl.Buffered(1))


def _even_in_kernel(layer_idx, x_ref, gain_ref, w_hbm, cos_ref, sa_ref, sb_ref,
                    q16_ref, k16_ref, v16_ref, g_ref, us_ref, h_nat, h_cls, w_ref, stage, sem):
    tm = ROW_TILE
    per_cls = tm // N_CLS

    @pl.when(_first_step())
    def _():
        _load_weights_bf16(layer_idx, [(w_hbm, w_ref, stage, sem)])

    def proj(lhs, lo, width):
        return jnp.dot(lhs, w_ref[:, lo:lo + width], preferred_element_type=F32)

    half = tm // 2
    for hh in range(2):
        r = slice(hh * half, (hh + 1) * half)
        h = _rms(x_ref[r, :], gain_ref[...])
        for c in range(D_MODEL // LANES):
            h_nat[c, r, :] = h[:, c * LANES:(c + 1) * LANES]
        hb = h.astype(BF16)
        g_ref[r, :] = proj(hb, 3 * ATT_WIDTH, ATT_WIDTH).astype(g_ref.dtype)
        us_ref[r, :] = proj(hb, 4 * ATT_WIDTH, 2 * SSM_WIDTH).astype(us_ref.dtype)
    for c in range(D_MODEL // LANES):
        for cls in range(N_CLS):
            h_cls[cls * per_cls:(cls + 1) * per_cls, c * LANES:(c + 1) * LANES] = (
                h_nat[c, pl.ds(cls, per_cls, stride=N_CLS), :].astype(BF16))
    hb_cls = h_cls[...]
    cos = cos_ref[...]
    sa = sa_ref[...]
    sb = sb_ref[...]

    def rope_store(dst, y, scale):
        for c in range(ATT_WIDTH // LANES):
            yc = y[:, c * LANES:(c + 1) * LANES]
            r = (yc * cos + pltpu.roll(yc, LANES - ROT_DIM // 2, 1) * sa
                 + pltpu.roll(yc, ROT_DIM // 2, 1) * sb)
            dst[0, :, :, c * LANES:(c + 1) * LANES] = (
                (r * scale).astype(dst.dtype).reshape(N_CLS, per_cls, LANES))

    rope_store(q16_ref, proj(hb_cls, 0, ATT_WIDTH), HEAD_DIM ** -0.5 * LOG2_E)
    rope_store(k16_ref, proj(hb_cls, ATT_WIDTH, ATT_WIDTH), 1.0)
    v16_ref[0] = proj(hb_cls, 2 * ATT_WIDTH, ATT_WIDTH).astype(v16_ref.dtype).reshape(
        N_CLS, per_cls, ATT_WIDTH)


def _even_in(x2, gains, layer, w_in, i, cos_t, sa_t, sb_t):
    tm = ROW_TILE
    nj = SEQ // tm
    row_spec = pl.BlockSpec((tm, D_MODEL), lambda b, j: (b * nj + j, 0))
    tab_spec = pl.BlockSpec((tm, LANES), lambda b, j: (j, 0))
    att_spec = pl.BlockSpec((tm, ATT_WIDTH), lambda b, j: (b * nj + j, 0))
    cls_spec = pl.BlockSpec((1, N_CLS, tm // N_CLS, ATT_WIDTH), lambda b, j: (b, 0, j, 0))
    att_shape = jax.ShapeDtypeStruct((BATCH * SEQ, ATT_WIDTH), BF16)
    cls_shape = jax.ShapeDtypeStruct((BATCH, N_CLS, CLS_LEN, ATT_WIDTH), F32)
    return pl.pallas_call(
        functools.partial(_even_in_kernel, i),
        grid=(BATCH, nj),
        in_specs=[row_spec, _layer_spec(gains, layer), pl.BlockSpec(memory_space=pl.ANY),
                  tab_spec, tab_spec, tab_spec],
        out_specs=[cls_spec, cls_spec, cls_spec, att_spec, att_spec],
        out_shape=[cls_shape, jax.ShapeDtypeStruct(cls_shape.shape, BF16),
                   jax.ShapeDtypeStruct(cls_shape.shape, BF16), att_shape, att_shape],
        scratch_shapes=[pltpu.VMEM((D_MODEL // LANES, tm, LANES), F32),
                        pltpu.VMEM((tm, D_MODEL), BF16)]
        + _weight_scratch(D_MODEL, EVEN_IN, slots=4),
        compiler_params=pltpu.CompilerParams(
            dimension_semantics=("arbitrary", "arbitrary"), vmem_limit_bytes=VMEM_LIMIT),
        name="even_in_proj",
    )(x2, gains, w_in, cos_t, sa_t, sb_t)


def _attn_masks():
    ch = ATT_CHUNK
    per = ch // N_CLS
    quarter = ch // 4
    row = np.arange(ch)
    col1 = np.arange(ch)
    col2 = np.arange(2 * ch)
    tau = N_CLS * (row % per) + row // per
    kk = N_CLS * (col2 % (2 * per)) + col2 // (2 * per)
    p1_full = np.where(kk[None, :] < ch, kk[None, :] >= tau[:, None],
                       (kk[None, :] - ch) <= tau[:, None])
    p1_first = (N_CLS * (col1 % per) + col1 // per)[None, :] <= tau[:, None]
    off = 4 * (row % quarter) + row // quarter
    kk2 = 4 * (col2 % (2 * quarter)) + col2 // (2 * quarter)
    p2_full = (kk2[None, :] >= off[:, None]) & (kk2[None, :] <= off[:, None] + ch)
    p2_first = (4 * (col1 % quarter) + col1 // quarter)[None, :] <= off[:, None]
    p3 = col1[None, :] <= row[:, None]
    valid = np.concatenate([p1_full, p1_first, p2_full, p2_first, p3], axis=1)
    return jnp.asarray(np.where(valid, 0.0, NEG_BIG), dtype=F32)


def _attn_kernel(q16_ref, k16_ref, v16_ref, mask_ref, o_ref, *scratch):
    per_pair = len(scratch) // ATT_PAIRS
    for pair in range(ATT_PAIRS):
        _attn_head_pair(q16_ref, k16_ref, v16_ref, mask_ref, o_ref,
                        slice(pair * LANES, (pair + 1) * LANES),
                        *scratch[pair * per_pair:(pair + 1) * per_pair])


def _attn_head_pair(q16_ref, k16_ref, v16_ref, mask_ref, o_ref, lanes,
                    m_a, m_b, acc_a, acc_b, fin):
    ch = ATT_CHUNK
    per = ch // N_CLS
    lane = lax.broadcasted_iota(jnp.int32, (1, LANES), 1)
    is_a = lane < HEAD_DIM
    is_b = jnp.logical_not(is_a)
    state = ((m_a, acc_a), (m_b, acc_b))
    ones_row = jnp.ones((1, LANES), BF16)
    zeros_row = jnp.zeros((1, LANES), BF16)
    row = lax.broadcasted_iota(jnp.int32, (ch, ch), 0)
    col = lax.broadcasted_iota(jnp.int32, (ch, ch), 1)
    shuffle = jnp.where(col == per * (row % N_CLS) + row // N_CLS, 1.0, 0.0).astype(BF16)
    mask_p1_full = slice(0, 2 * ch)
    mask_p1_first = slice(2 * ch, 3 * ch)
    mask_p2_full = slice(3 * ch, 5 * ch)
    mask_p2_first = slice(5 * ch, 6 * ch)
    mask_p3 = slice(6 * ch, 7 * ch)

    def gather(ref, slabs):
        parts = [ref[s:s + n, :] for s, n in slabs]
        return parts[0] if len(parts) == 1 else jnp.concatenate(parts, axis=0)

    def gather_cls(ref, slabs):
        parts = [ref[0, s // CLS_LEN, s % CLS_LEN:s % CLS_LEN + n, lanes] for s, n in slabs]
        return parts[0] if len(parts) == 1 else jnp.concatenate(parts, axis=0)

    def scatter(ref, slabs, val):
        off = 0
        for s, n in slabs:
            ref[s:s + n, :] = val[off:off + n, :]
            off += n

    def phase_scores(items):
        scores = []
        for load_q, load_k, load_v, mask, slabs, has_state, finalize in items:
            qc = load_q()
            kc = load_k()
            for is_h in (is_a, is_b):
                qh = jnp.where(is_h, qc, zeros_row)
                s = lax.dot_general(qh, kc, (((1,), (1,)), ((), ())),
                                    preferred_element_type=F32) + mask_ref[:, mask]
                scores.append((s, jnp.max(s, axis=1, keepdims=True)))
        return scores

    def phase_update(items, scores):
        partial = []
        n = 0
        for load_q, load_k, load_v, mask, slabs, has_state, finalize in items:
            vc = load_v()
            for h, is_h in enumerate((is_a, is_b)):
                s, m_c = scores[n]
                n += 1
                if has_state:
                    m_prev = gather(state[h][0], slabs)
                    m_new = jnp.maximum(m_prev, m_c)
                else:
                    m_prev = None
                    m_new = jnp.broadcast_to(m_c, (ch, LANES))
                m_k = jnp.concatenate([m_new] * (s.shape[1] // LANES), axis=1)
                p = jnp.exp2(s - m_k)
                vh = jnp.where(is_h, vc, ones_row)
                pv = jnp.dot(p.astype(BF16), vh, preferred_element_type=F32)
                partial.append((m_prev, m_new, pv))
        n = 0
        for load_q, load_k, load_v, mask, slabs, has_state, finalize in items:
            res = []
            for h in range(2):
                m_prev, m_new, pv = partial[n]
                n += 1
                if has_state:
                    a_new = jnp.exp2(m_prev - m_new) * gather(state[h][1], slabs) + pv
                else:
                    a_new = pv
                res.append(a_new)
                if not finalize:
                    scatter(state[h][0], slabs, m_new)
                    scatter(state[h][1], slabs, a_new)
            if finalize:
                num = jnp.where(is_a, res[0], res[1])
                den = pltpu.roll(jnp.where(is_a, res[1], res[0]), HEAD_DIM, 1)
                scatter(fin, slabs, num / den)

    def block(slabs, kslabs, mask, has_state, finalize):
        return (lambda: gather_cls(q16_ref, slabs).astype(BF16),
                lambda: gather_cls(k16_ref, kslabs).astype(BF16),
                lambda: gather_cls(v16_ref, kslabs).astype(BF16), mask, slabs, has_state, finalize)

    order = []
    for i in range(SEQ // ch):
        slabs = [(r * CLS_LEN + per * i, per) for r in range(N_CLS)]
        if i > 0:
            kslabs = [(r * CLS_LEN + per * (i - 1), 2 * per) for r in range(N_CLS)]
            order.append(block(slabs, kslabs, mask_p1_full, False, False))
        else:
            order.append(block(slabs, slabs, mask_p1_first, False, False))

    quarter = ch // 4
    dil4 = []
    for r4 in range(4):
        for ic in range(SEQ // (4 * ch)):
            cls = [(4 * j + r4) * CLS_LEN for j in range(4)]
            slabs = [(c + quarter * ic, quarter) for c in cls]
            if ic > 0:
                kslabs = [(c + quarter * (ic - 1), 2 * quarter) for c in cls]
                dil4.append(block(slabs, kslabs, mask_p2_full, True, False))
            else:
                dil4.append(block(slabs, slabs, mask_p2_first, True, False))

    order += dil4

    for r in range(N_CLS):
        slabs = [(r * CLS_LEN, CLS_LEN)]
        order.append(block(slabs, slabs, mask_p3, True, True))

    assert (SEQ // ch) % ATT_GROUP == 0
    pending = None
    for i in range(0, len(order), ATT_GROUP):
        items = order[i:i + ATT_GROUP]
        scores = phase_scores(items)
        if pending is not None:
            phase_update(*pending)
        pending = (items, scores)
    phase_update(*pending)

    for i in range(SEQ // ch):
        slab = gather(fin, [(r * CLS_LEN + per * i, per) for r in range(N_CLS)])
        o_ref[0, i * ch:(i + 1) * ch, lanes] = jnp.dot(
            shuffle, slab.astype(BF16), preferred_element_type=F32).astype(o_ref.dtype)


def _attention(q16, k16, v16):
    assert CLS_LEN == ATT_CHUNK and SEQ % (4 * ATT_CHUNK) == 0
    width = ATT_PAIRS * LANES
    blk = pl.BlockSpec((1, SEQ, width), lambda b, h: (b, 0, h))
    cls_blk = pl.BlockSpec((1, N_CLS, CLS_LEN, width), lambda b, h: (b, 0, 0, h))
    f32_scr = pltpu.VMEM((SEQ, LANES), F32)
    masks = _attn_masks()
    return pl.pallas_call(
        _attn_kernel,
        grid=(BATCH, ATT_WIDTH // width),
        in_specs=[cls_blk, cls_blk, cls_blk, _const_spec(masks.shape)],
        out_specs=blk,
        out_shape=jax.ShapeDtypeStruct((BATCH, SEQ, ATT_WIDTH), BF16),
        scratch_shapes=[f32_scr] * (5 * ATT_PAIRS),
        compiler_params=pltpu.CompilerParams(
            dimension_semantics=("arbitrary", "arbitrary"), vmem_limit_bytes=VMEM_LIMIT),
        name="dilated_attention",
    )(q16, k16, v16, masks)


def _ssm_kernel(us_ref, perm_ref, lam_re_ref, lam_im_ref, wb_ref, wc_re_ref, wc_im_ref,
                d_ref, gw_ref, gb_ref, o_ref, st_re, st_im, *scratch):
    @pl.when(pl.program_id(0) == 0)
    def _():
        st_re[...] = jnp.zeros_like(st_re)
        st_im[...] = jnp.zeros_like(st_im)

    n_sub = SSM_TC // SSM_SUB
    per_sub = len(scratch) // n_sub
    for sub in range(n_sub):
        _ssm_sub_chunk(sub * SSM_SUB, us_ref, perm_ref, lam_re_ref, lam_im_ref, wb_ref, wc_re_ref,
                       wc_im_ref, d_ref, gw_ref, gb_ref, o_ref, st_re, st_im,
                       *scratch[sub * per_sub:(sub + 1) * per_sub])


def _ssm_sub_chunk(t0, us_ref, perm_ref, lam_re_ref, lam_im_ref, wb_ref, wc_re_ref, wc_im_ref,
                   d_ref, gw_ref, gb_ref, o_ref, st_re, st_im, xr, xi, y_tb, y_bt):
    tc = SSM_SUB
    rows = tc * BATCH
    n_chunk = SSM_WIDTH // LANES
    cpc = N_STATE // n_chunk

    ub = us_ref[:, t0:t0 + tc, :SSM_WIDTH].reshape(rows, SSM_WIDTH)
    u_tb = jnp.dot(perm_ref[...], ub, preferred_element_type=F32).astype(BF16)
    def b_proj(c):
        cols = slice(c * cpc, (c + 1) * cpc)
        bu = jnp.dot(u_tb[:, c * LANES:(c + 1) * LANES], wb_ref[c], preferred_element_type=F32)
        xr[:, cols] = bu[:, :cpc]
        xi[:, cols] = bu[:, cpc:]

    def recurrence(c):
        cols = slice(c * cpc, (c + 1) * cpc)
        lr = lam_re_ref[:, cols]
        li = lam_im_ref[:, cols]
        sr = st_re[:, cols]
        si = st_im[:, cols]
        for t in range(tc):
            trow = slice(t * BATCH, (t + 1) * BATCH)
            sr, si = (lr * sr - li * si + xr[trow, cols], lr * si + li * sr + xi[trow, cols])
            xr[trow, cols] = sr
            xi[trow, cols] = si
        st_re[:, cols] = sr
        st_im[:, cols] = si

    def c_proj(c):
        cols = slice(c * cpc, (c + 1) * cpc)
        y_tb[c] = (
            jnp.dot(xr[:, cols].astype(BF16), wc_re_ref[c], preferred_element_type=F32)
            + jnp.dot(xi[:, cols].astype(BF16), wc_im_ref[c], preferred_element_type=F32))

    def skip_gelu(c):
        lanes = slice(c * LANES, (c + 1) * LANES)
        y_c = jnp.concatenate(
            [y_tb[c, pl.ds(b, tc, stride=BATCH), :] for b in range(BATCH)], axis=0)
        y_bt[:, lanes] = jax.nn.gelu(y_c + d_ref[:, lanes] * ub[:, lanes].astype(F32))

    b_proj(0)
    gate = _silu(us_ref[:, t0:t0 + tc, SSM_WIDTH:].reshape(rows, SSM_WIDTH).astype(F32))
    for c in range(n_chunk):
        if c + 1 < n_chunk:
            b_proj(c + 1)
        recurrence(c)
        if c > 0:
            skip_gelu(c - 1)
        c_proj(c)
    skip_gelu(n_chunk - 1)
    y = y_bt[...]
    z = jnp.dot(y.astype(BF16), gw_ref[...], preferred_element_type=F32) + gb_ref[...]
    out = y * _sigmoid(z) * gate
    o_ref[:, t0:t0 + tc, :] = out.reshape(BATCH, tc, SSM_WIDTH).astype(o_ref.dtype)


def _ssm(us3, params, i):
    tc = SSM_TC
    rows = SSM_SUB * BATCH
    dst_row = np.arange(rows)
    perm = jnp.asarray((dst_row[:, None] % BATCH) * SSM_SUB + dst_row[:, None] // BATCH
                       == dst_row[None, :], dtype=BF16)
    sub_scratch = [pltpu.VMEM((rows, N_STATE), F32), pltpu.VMEM((rows, N_STATE), F32),
                   pltpu.VMEM((SSM_WIDTH // LANES, rows, LANES), F32),
                   pltpu.VMEM((rows, SSM_WIDTH), F32)]
    return pl.pallas_call(
        _ssm_kernel,
        grid=(SEQ // tc,),
        in_specs=[pl.BlockSpec((BATCH, tc, 2 * SSM_WIDTH), lambda t: (0, t, 0)),
                  _const_spec((rows, rows))] + [_layer_spec(p, i) for p in params],
        out_specs=pl.BlockSpec((BATCH, tc, SSM_WIDTH), lambda t: (0, t, 0)),
        out_shape=jax.ShapeDtypeStruct((BATCH, SEQ, SSM_WIDTH), BF16),
        scratch_shapes=[pltpu.VMEM((BATCH, N_STATE), F32), pltpu.VMEM((BATCH, N_STATE), F32)]
        + sub_scratch * (SSM_TC // SSM_SUB),
        compiler_params=pltpu.CompilerParams(
            dimension_semantics=("arbitrary",), vmem_limit_bytes=VMEM_LIMIT),
        name="s5_branch",
    )(us3, perm, *params)


def _tail_kernel(layer_idx, x_ref, att_ref, g_ref, ssm_ref, w_eo_hbm, post_e_ref,
                 pre_ref, w_in_hbm, pw_hbm, ps_ref, w_out_hbm, post_ref, o_ref,
                 ext, tmp0, tmp1,
                 w_eo_ref, stage_eo, sem_eo, w_in_ref, stage_in, sem_in,
                 pw_ref, stage_pw, sem_pw, w_out_ref, stage_out, sem_out):
    tm = ODD_TILE
    hist = POOL_HIST
    total = hist + tm
    j = pl.program_id(1)

    @pl.when(_first_step())
    def _():
        _load_weights_bf16(layer_idx, [(w_eo_hbm, w_eo_ref, stage_eo, sem_eo),
                                       (w_in_hbm, w_in_ref, stage_in, sem_in),
                                       (pw_hbm, pw_ref, stage_pw, sem_pw),
                                       (w_out_hbm, w_out_ref, stage_out, sem_out)])

    @pl.when(j == 0)
    def _():
        ext[0:hist, :] = jnp.zeros((hist, POOL_WIDTH), F32)

    tmps = (tmp0, tmp1)
    sub = ODD_SUB
    for s in range(tm // sub):
        lo = hist + s * sub
        hi = lo + sub
        rows = slice(s * sub, (s + 1) * sub)
        half = sub // 2
        y_even = []
        for h in range(2):
            r = slice(s * sub + h * half, s * sub + (h + 1) * half)
            gated = (att_ref[r, :].astype(F32) * _silu(g_ref[r, :].astype(F32))).astype(BF16)
            y_even.append(
                jnp.dot(gated, w_eo_ref[:ATT_WIDTH, :], preferred_element_type=F32)
                + jnp.dot(ssm_ref[r, :], w_eo_ref[ATT_WIDTH:, :], preferred_element_type=F32))
        x_parts, hb_parts = [], []
        for h in range(2):
            r = slice(s * sub + h * half, s * sub + (h + 1) * half)
            x_h = x_ref[r, :] + _rms(y_even[h], post_e_ref[...])
            hb_h = _rms(x_h, pre_ref[...]).astype(BF16)
            ext[lo + h * half:lo + (h + 1) * half, :] = jnp.dot(
                hb_h, w_in_ref[:, :POOL_WIDTH], preferred_element_type=F32)
            x_parts.append(x_h)
            hb_parts.append(hb_h)
        x = jnp.concatenate(x_parts, axis=0)
        hb = jnp.concatenate(hb_parts, axis=0)
        t_idx = j * tm + s * sub + lax.broadcasted_iota(jnp.int32, (sub, 1), 0)
        inv_count = 1.0 / (t_idx + 1).astype(F32)
        y = None
        for g, w in enumerate(POOL_WINDOWS):
            cols = slice(g * POOL_GROUP, (g + 1) * POOL_GROUP)
            levels = int(math.log2(w))
            src_ref, src_cols = ext, cols
            for lev in range(levels):
                shift = 2 ** lev
                start = lo - 8 * (levels - 1 - lev)
                val = (src_ref[start:hi, src_cols] + src_ref[start - shift:hi - shift, src_cols])
                if lev < levels - 1:
                    dst = tmps[lev % 2]
                    dst[start:hi, :] = val
                    src_ref, src_cols = dst, slice(None)
            u_g = ext[lo:hi, cols]
            mixed = val * jnp.maximum(inv_count, 1.0 / w) - u_g
            yg = jnp.dot(mixed.astype(BF16), pw_ref[cols, :], preferred_element_type=F32)
            gate = jnp.dot(hb, w_in_ref[:, POOL_WIDTH + g * POOL_GROUP:
                                        POOL_WIDTH + (g + 1) * POOL_GROUP],
                           preferred_element_type=F32)
            yg = yg * ps_ref[:, cols] * _silu(gate)
            part = jnp.dot(yg.astype(BF16), w_out_ref[cols, :], preferred_element_type=F32)
            y = part if y is None else y + part
        o_ref[rows, :] = x + _rms(y, post_ref[...])
    ext[0:hist, :] = ext[tm:total, :]


def _even_tail_odd(x2, att2, g2, ssm2, w_even_out, pre, post, w_in, pool_w, pool_scale, w_out, i,
                   layer):
    tm = ODD_TILE
    nj = SEQ // tm
    row_spec = pl.BlockSpec((tm, D_MODEL), lambda b, j: (b * nj + j, 0))
    att_spec = pl.BlockSpec((tm, ATT_WIDTH), lambda b, j: (b * nj + j, 0))
    hbm = pl.BlockSpec(memory_space=pl.ANY)
    return pl.pallas_call(
        functools.partial(_tail_kernel, i),
        grid=(BATCH, nj),
        in_specs=[row_spec, att_spec, att_spec,
                  pl.BlockSpec((tm, SSM_WIDTH), lambda b, j: (b * nj + j, 0)),
                  hbm, _layer_spec(post, layer),
                  _layer_spec(pre, layer + 1), hbm,
                  hbm, _layer_spec(pool_scale, i),
                  hbm, _layer_spec(post, layer + 1)],
        out_specs=row_spec,
        out_shape=jax.ShapeDtypeStruct((BATCH * SEQ, D_MODEL), F32),
        scratch_shapes=[pltpu.VMEM((POOL_HIST + tm, POOL_WIDTH), F32),
                        pltpu.VMEM((POOL_HIST + tm, POOL_GROUP), F32),
                        pltpu.VMEM((POOL_HIST + tm, POOL_GROUP), F32)]
        + _weight_scratch(*w_even_out.shape[1:]) + _weight_scratch(*w_in.shape[1:])
        + _weight_scratch(*pool_w.shape[1:]) + _weight_scratch(*w_out.shape[1:]),
        compiler_params=pltpu.CompilerParams(
            dimension_semantics=("arbitrary", "arbitrary"), vmem_limit_bytes=VMEM_LIMIT),
        name="even_tail_odd_layer",
    )(x2, att2, g2, ssm2, w_even_out, post, pre, w_in, pool_w, pool_scale, w_out, post)


def _rope_tables():
    half = ROT_DIM // 2
    inv_freq = (np.float32(ROPE_THETA)
                ** (-np.arange(0, ROT_DIM, 2, dtype=np.float32) / np.float32(ROT_DIM)))
    ang = np.arange(SEQ, dtype=np.float32)[:, None] * inv_freq[None, :].astype(np.float32)
    cos, sin = np.cos(ang).astype(np.float32), np.sin(ang).astype(np.float32)
    zeros = np.zeros((SEQ, HEAD_DIM - ROT_DIM), np.float32)
    zeros_h = np.zeros((SEQ, half), np.float32)
    cos_h = np.concatenate([cos, cos, np.ones_like(zeros)], axis=1)
    sa_h = np.concatenate([-sin, zeros_h, zeros], axis=1)
    sb_h = np.concatenate([zeros_h, sin, zeros], axis=1)
    rep = LANES // HEAD_DIM
    pos = np.arange(SEQ).reshape(SEQ // ROW_TILE, ROW_TILE // N_CLS, N_CLS)
    order = pos.transpose(0, 2, 1).reshape(SEQ)
    return tuple(jnp.asarray(np.tile(t, (1, rep))[order]) for t in (cos_h, sa_h, sb_h))


def _ssm_params(a_re, a_im, log_dt, b_re, b_im, c_re, c_im):
    n_layer = a_re.shape[0]
    lam = lax.complex(a_re.astype(F32), a_im.astype(F32))
    dt = jnp.exp(log_dt.astype(F32))[..., None]
    lam_bar = jnp.exp(lam * dt)
    b_bar = ((lam_bar - 1.0) / lam)[..., None] * lax.complex(b_re.astype(F32), b_im.astype(F32))
    lam_re = jnp.broadcast_to(jnp.real(lam_bar).reshape(n_layer, 1, N_STATE),
                              (n_layer, BATCH, N_STATE))
    lam_im = jnp.broadcast_to(jnp.imag(lam_bar).reshape(n_layer, 1, N_STATE),
                              (n_layer, BATCH, N_STATE))
    gpc = LANES // SSM_GROUP
    n_chunk = SSM_GROUPS // gpc
    eye = jnp.eye(gpc, dtype=F32)

    def block_diag_in(m):
        m = m.reshape(n_layer, n_chunk, gpc, SSM_STATE, SSM_GROUP)
        return jnp.einsum('lcgph,gk->lcghkp', m, eye).reshape(
            n_layer, n_chunk, gpc * SSM_GROUP, gpc * SSM_STATE)

    def block_diag_out(m):
        m = m.reshape(n_layer, n_chunk, gpc, SSM_GROUP, SSM_STATE)
        return jnp.einsum('lcghp,gk->lcgpkh', m, eye).reshape(
            n_layer, n_chunk, gpc * SSM_STATE, gpc * SSM_GROUP)

    wb = jnp.concatenate([block_diag_in(jnp.real(b_bar)), block_diag_in(jnp.imag(b_bar))],
                         axis=3).astype(BF16)
    wc_re = block_diag_out(c_re.astype(F32)).astype(BF16)
    wc_im = block_diag_out(-c_im.astype(F32)).astype(BF16)
    return lam_re, lam_im, wb, wc_re, wc_im


def kernel(x, pre_norm, post_norm, even_w_in, even_w_out, ssm_a_re, ssm_a_im, ssm_log_dt,
           ssm_b_re, ssm_b_im, ssm_c_re, ssm_c_im, ssm_d, ssm_glu_w, ssm_glu_b,
           odd_w_in, pool_w, pool_scale, odd_w_out):
    cos_t, sa_t, sb_t = _rope_tables()
    pre = pre_norm.astype(F32).reshape(DEPTH, 1, D_MODEL)
    post = post_norm.astype(F32).reshape(DEPTH, 1, D_MODEL)
    even_w_in_f = even_w_in.astype(F32)
    even_w_out_f = even_w_out.astype(F32)
    odd_w_in_f = odd_w_in.astype(F32)
    odd_w_out_f = odd_w_out.astype(F32)
    pool_w_f = pool_w.astype(F32).reshape(-1, POOL_WIDTH, POOL_GROUP)
    pool_scale3 = pool_scale.astype(F32).reshape(-1, 1, POOL_WIDTH)
    ssm_params = _ssm_params(ssm_a_re, ssm_a_im, ssm_log_dt, ssm_b_re, ssm_b_im,
                             ssm_c_re, ssm_c_im) + (
        ssm_d.astype(F32).reshape(-1, 1, SSM_WIDTH), ssm_glu_w.astype(BF16),
        ssm_glu_b.astype(F32).reshape(-1, 1, SSM_WIDTH))
    x2 = x.reshape(BATCH * SEQ, D_MODEL)
    assert DEPTH % 2 == 0
    for layer in range(0, DEPTH, 2):
        i = layer // 2
        q16, k16, v16, g, us = _even_in(x2, pre, layer, even_w_in_f, i, cos_t, sa_t, sb_t)
        att = _attention(q16, k16, v16)
        ssm = _ssm(us.reshape(BATCH, SEQ, 2 * SSM_WIDTH), ssm_params, i)
        x2 = _even_tail_odd(x2, att.reshape(BATCH * SEQ, ATT_WIDTH), g,
                            ssm.reshape(BATCH * SEQ, SSM_WIDTH), even_w_out_f, pre, post,
                            odd_w_in_f, pool_w_f, pool_scale3, odd_w_out_f, i, layer)
    return x2.reshape(BATCH, SEQ, D_MODEL)
```

```python
import functools
import math

import jax
import jax.numpy as jnp
import numpy as np
from jax import lax
from jax.experimental import pallas as pl
from jax.experimental.pallas import tpu as pltpu

F32 = jnp.float32
BF16 = jnp.bfloat16

D_MODEL = 1024
BATCH = 8
SEQ = 2048
DEPTH = 4
HEAD_DIM = 64
ATT_WIDTH = 1024
ROT_DIM = 16
ROPE_THETA = 500000.0
SSM_WIDTH = 512
SSM_GROUP = 16
SSM_GROUPS = 32
SSM_STATE = 64
N_STATE = SSM_GROUPS * SSM_STATE
POOL_WIDTH = 2048
POOL_WINDOWS = (2, 4, 8, 16)
POOL_GROUP = 512
EVEN_IN = 4 * ATT_WIDTH + 2 * SSM_WIDTH
RMS_EPS = 1e-6
LOG2_E = math.log2(math.e)

LANES = 128
N_CLS = 16
CLS_LEN = SEQ // N_CLS
ATT_CHUNK = 128
ATT_GROUP = 4
ATT_HEAD_GROUP = 8
ATT_PAIRS = 2
NEG_BIG = -1e30
VMEM_LIMIT = 56 * 1024 * 1024
WEIGHT_CHUNK_BYTES = 3 << 19
BF16_SUBLANES = 16

ROW_TILE = 512
SSM_TC = 128
SSM_SUB = 64
ODD_TILE = 512
ODD_SUB = 512
POOL_HIST = 32


def _sigmoid(x):
    return 0.5 + 0.5 * jnp.tanh(0.5 * x)


def _silu(x):
    h = 0.5 * x
    return h + h * jnp.tanh(h)


def _rms(x, gain):
    return x * lax.rsqrt(jnp.mean(x * x, axis=-1, keepdims=True) + RMS_EPS) * gain


def _const_spec(shape):
    nd = len(shape)
    return pl.BlockSpec(shape, lambda *_: (0,) * nd, pipeline_mode=pl.Buffered(1))


def _load_weights_bf16(layer, weights):
    def copy(w_hbm, stage, sem, k):
        slots, chunk = stage.shape[0], stage.shape[1]
        return pltpu.make_async_copy(
            w_hbm.at[layer, pl.ds(k * chunk, chunk), :], stage.at[k % slots], sem.at[k % slots])

    n_chunks = [dst.shape[0] // stage.shape[1] for _, dst, stage, _ in weights]
    for (w_hbm, dst, stage, sem), n in zip(weights, n_chunks):
        for k in range(min(stage.shape[0], n)):
            copy(w_hbm, stage, sem, k).start()
    for k in range(max(n_chunks)):
        for (w_hbm, dst, stage, sem), n in zip(weights, n_chunks):
            if k >= n:
                continue
            slots, chunk = stage.shape[0], stage.shape[1]
            copy(w_hbm, stage, sem, k).wait()
            dst[k * chunk:(k + 1) * chunk, :] = stage[k % slots].astype(BF16)
            if k + slots < n:
                copy(w_hbm, stage, sem, k + slots).start()


def _weight_scratch(rows, cols, slots=2):
    chunk = rows
    while chunk * cols * 4 > WEIGHT_CHUNK_BYTES:
        chunk //= 2
    assert rows % chunk == 0 and chunk % BF16_SUBLANES == 0
    return [pltpu.VMEM((rows, cols), BF16), pltpu.VMEM((slots, chunk, cols), F32),
            pltpu.SemaphoreType.DMA((slots,))]


def _first_step():
    return jnp.logical_and(pl.program_id(0) == 0, pl.program_id(1) == 0)


def _layer_spec(stacked, layer):
    shape = stacked.shape[1:]
    nd = len(shape)
    return pl.BlockSpec((None,) + tuple(shape), lambda *_: (layer,) + (0,) * nd,
                        pipeline_mode=pl.Buffered(1))


def _even_in_kernel(layer_idx, x_ref, gain_ref, w_hbm, cos_ref, sa_ref, sb_ref,
                    q16_ref, k16_ref, v16_ref, g_ref, us_ref, h_nat, h_cls, w_ref, stage, sem):
    tm = ROW_TILE
    per_cls = tm // N_CLS

    @pl.when(_first_step())
    def _():
        _load_weights_bf16(layer_idx, [(w_hbm, w_ref, stage, sem)])

    def proj(lhs, lo, width):
        return jnp.dot(lhs, w_ref[:, lo:lo + width], preferred_element_type=F32)

    half = tm // 2
    for hh in range(2):
        r = slice(hh * half, (hh + 1) * half)
        h = _rms(x_ref[r, :], gain_ref[...])
        for c in range(D_MODEL // LANES):
            h_nat[c, r, :] = h[:, c * LANES:(c + 1) * LANES]
        hb = h.astype(BF16)
        g_ref[r, :] = proj(hb, 3 * ATT_WIDTH, ATT_WIDTH).astype(g_ref.dtype)
        us_ref[r, :] = proj(hb, 4 * ATT_WIDTH, 2 * SSM_WIDTH).astype(us_ref.dtype)
    for c in range(D_MODEL // LANES):
        for cls in range(N_CLS):
            h_cls[cls * per_cls:(cls + 1) * per_cls, c * LANES:(c + 1) * LANES] = (
                h_nat[c, pl.ds(cls, per_cls, stride=N_CLS), :].astype(BF16))
    hb_cls = h_cls[...]
    cos = cos_ref[...]
    sa = sa_ref[...]
    sb = sb_ref[...]

    def rope_store(dst, y, scale):
        for c in range(ATT_WIDTH // LANES):
            yc = y[:, c * LANES:(c + 1) * LANES]
            r = (yc * cos + pltpu.roll(yc, LANES - ROT_DIM // 2, 1) * sa
                 + pltpu.roll(yc, ROT_DIM // 2, 1) * sb)
            dst[0, :, :, c * LANES:(c + 1) * LANES] = (
                (r * scale).astype(dst.dtype).reshape(N_CLS, per_cls, LANES))

    rope_store(q16_ref, proj(hb_cls, 0, ATT_WIDTH), HEAD_DIM ** -0.5 * LOG2_E)
    rope_store(k16_ref, proj(hb_cls, ATT_WIDTH, ATT_WIDTH), 1.0)
    v16_ref[0] = proj(hb_cls, 2 * ATT_WIDTH, ATT_WIDTH).astype(v16_ref.dtype).reshape(
        N_CLS, per_cls, ATT_WIDTH)


def _even_in(x2, gains, layer, w_in, i, cos_t, sa_t, sb_t):
    tm = ROW_TILE
    nj = SEQ // tm
    row_spec = pl.BlockSpec((tm, D_MODEL), lambda b, j: (b * nj + j, 0))
    tab_spec = pl.BlockSpec((tm, LANES), lambda b, j: (j, 0))
    att_spec = pl.BlockSpec((tm, ATT_WIDTH), lambda b, j: (b * nj + j, 0))
    cls_spec = pl.BlockSpec((1, N_CLS, tm // N_CLS, ATT_WIDTH), lambda b, j: (b, 0, j, 0))
    att_shape = jax.ShapeDtypeStruct((BATCH * SEQ, ATT_WIDTH), BF16)
    cls_shape = jax.ShapeDtypeStruct((BATCH, N_CLS, CLS_LEN, ATT_WIDTH), F32)
    return pl.pallas_call(
        functools.partial(_even_in_kernel, i),
        grid=(BATCH, nj),
        in_specs=[row_spec, _layer_spec(gains, layer), pl.BlockSpec(memory_space=pl.ANY),
                  tab_spec, tab_spec, tab_spec],
        out_specs=[cls_spec, cls_spec, cls_spec, att_spec, att_spec],
        out_shape=[cls_shape, jax.ShapeDtypeStruct(cls_shape.shape, BF16),
                   jax.ShapeDtypeStruct(cls_shape.shape, BF16), att_shape, att_shape],
        scratch_shapes=[pltpu.VMEM((D_MODEL // LANES, tm, LANES), F32),
                        pltpu.VMEM((tm, D_MODEL), BF16)]
        + _weight_scratch(D_MODEL, EVEN_IN, slots=4),
        compiler_params=pltpu.CompilerParams(
            dimension_semantics=("arbitrary", "arbitrary"), vmem_limit_bytes=VMEM_LIMIT),
        name="even_in_proj",
    )(x2, gains, w_in, cos_t, sa_t, sb_t)


def _attn_masks():
    ch = ATT_CHUNK
    per = ch // N_CLS
    quarter = ch // 4
    row = np.arange(ch)
    col1 = np.arange(ch)
    col2 = np.arange(2 * ch)
    tau = N_CLS * (row % per) + row // per
    kk = N_CLS * (col2 % (2 * per)) + col2 // (2 * per)
    p1_full = np.where(kk[None, :] < ch, kk[None, :] >= tau[:, None],
                       (kk[None, :] - ch) <= tau[:, None])
    p1_first = (N_CLS * (col1 % per) + col1 // per)[None, :] <= tau[:, None]
    off = 4 * (row % quarter) + row // quarter
    kk2 = 4 * (col2 % (2 * quarter)) + col2 // (2 * quarter)
    p2_full = (kk2[None, :] >= off[:, None]) & (kk2[None, :] <= off[:, None] + ch)
    p2_first = (4 * (col1 % quarter) + col1 // quarter)[None, :] <= off[:, None]
    p3 = col1[None, :] <= row[:, None]
    valid = np.concatenate([p1_full, p1_first, p2_full, p2_first, p3], axis=1)
    return jnp.asarray(np.where(valid, 0.0, NEG_BIG), dtype=F32)


def _attn_kernel(q16_ref, k16_ref, v16_ref, mask_ref, o_ref, *scratch):
    per_pair = len(scratch) // ATT_PAIRS
    for pair in range(ATT_PAIRS):
        _attn_head_pair(q16_ref, k16_ref, v16_ref, mask_ref, o_ref,
                        slice(pair * LANES, (pair + 1) * LANES),
                        *scratch[pair * per_pair:(pair + 1) * per_pair])


def _attn_head_pair(q16_ref, k16_ref, v16_ref, mask_ref, o_ref, lanes,
                    m_a, m_b, acc_a, acc_b, fin):
    ch = ATT_CHUNK
    per = ch // N_CLS
    lane = lax.broadcasted_iota(jnp.int32, (1, LANES), 1)
    is_a = lane < HEAD_DIM
    is_b = jnp.logical_not(is_a)
    state = ((m_a, acc_a), (m_b, acc_b))
    ones_row = jnp.ones((1, LANES), BF16)
    zeros_row = jnp.zeros((1, LANES), BF16)
    row = lax.broadcasted_iota(jnp.int32, (ch, ch), 0)
    col = lax.broadcasted_iota(jnp.int32, (ch, ch), 1)
    shuffle = jnp.where(col == per * (row % N_CLS) + row // N_CLS, 1.0, 0.0).astype(BF16)
    mask_p1_full = slice(0, 2 * ch)
    mask_p1_first = slice(2 * ch, 3 * ch)
    mask_p2_full = slice(3 * ch, 5 * ch)
    mask_p2_first = slice(5 * ch, 6 * ch)
    mask_p3 = slice(6 * ch, 7 * ch)

    def gather(ref, slabs):
        parts = [ref[s:s + n, :] for s, n in slabs]
        return parts[0] if len(parts) == 1 else jnp.concatenate(parts, axis=0)

    def gather_cls(ref, slabs):
        parts = [ref[0, s // CLS_LEN, s % CLS_LEN:s % CLS_LEN + n, lanes] for s, n in slabs]
        return parts[0] if len(parts) == 1 else jnp.concatenate(parts, axis=0)

    def scatter(ref, slabs, val):
        off = 0
        for s, n in slabs:
            ref[s:s + n, :] = val[off:off + n, :]
            off += n

    def phase_scores(items, h):
        is_h = (is_a, is_b)[h]
        scores = []
        for load_q, load_k, load_v, mask, slabs, has_state, finalize in items:
            qh = jnp.where(is_h, load_q(), zeros_row)
            s = lax.dot_general(qh, load_k(), (((1,), (1,)), ((), ())),
                                preferred_element_type=F32) + mask_ref[:, mask]
            scores.append((s, jnp.max(s, axis=1, keepdims=True)))
        return scores

    def phase_update(items, h, scores):
        is_h = (is_a, is_b)[h]
        partial = []
        for (load_q, load_k, load_v, mask, slabs, has_state, finalize), (s, m_c) in zip(
                items, scores):
            if has_state:
                m_prev = gather(state[h][0], slabs)
                m_new = jnp.maximum(m_prev, m_c)
            else:
                m_prev = None
                m_new = jnp.broadcast_to(m_c, (ch, LANES))
            m_k = jnp.concatenate([m_new] * (s.shape[1] // LANES), axis=1)
            p = jnp.exp2(s - m_k)
            vh = jnp.where(is_h, load_v(), ones_row)
            pv = jnp.dot(p.astype(BF16), vh, preferred_element_type=F32)
            partial.append((m_prev, m_new, pv))
        for (load_q, load_k, load_v, mask, slabs, has_state, finalize), (m_prev, m_new, pv) in zip(
                items, partial):
            if has_state:
                a_new = jnp.exp2(m_prev - m_new) * gather(state[h][1], slabs) + pv
            else:
                a_new = pv
            if not finalize:
                scatter(state[h][0], slabs, m_new)
                scatter(state[h][1], slabs, a_new)
            else:
                val = a_new / pltpu.roll(a_new, HEAD_DIM, 1)
                if h == 1:
                    val = jnp.where(is_a, gather(fin, slabs), val)
                scatter(fin, slabs, val)

    def block(slabs, kslabs, mask, has_state, finalize):
        return (lambda: gather_cls(q16_ref, slabs).astype(BF16),
                lambda: gather_cls(k16_ref, kslabs).astype(BF16),
                lambda: gather_cls(v16_ref, kslabs).astype(BF16), mask, slabs, has_state, finalize)

    order = []
    for i in range(SEQ // ch):
        slabs = [(r * CLS_LEN + per * i, per) for r in range(N_CLS)]
        if i > 0:
            kslabs = [(r * CLS_LEN + per * (i - 1), 2 * per) for r in range(N_CLS)]
            order.append(block(slabs, kslabs, mask_p1_full, False, False))
        else:
            order.append(block(slabs, slabs, mask_p1_first, False, False))

    quarter = ch // 4
    dil4 = []
    for r4 in range(4):
        for ic in range(SEQ // (4 * ch)):
            cls = [(4 * j + r4) * CLS_LEN for j in range(4)]
            slabs = [(c + quarter * ic, quarter) for c in cls]
            if ic > 0:
                kslabs = [(c + quarter * (ic - 1), 2 * quarter) for c in cls]
                dil4.append(block(slabs, kslabs, mask_p2_full, True, False))
            else:
                dil4.append(block(slabs, slabs, mask_p2_first, True, False))

    order += dil4

    for r in range(N_CLS):
        slabs = [(r * CLS_LEN, CLS_LEN)]
        order.append(block(slabs, slabs, mask_p3, True, True))

    assert (SEQ // ch) % ATT_GROUP == 0
    pending = None
    for i in range(0, len(order), ATT_HEAD_GROUP):
        items = order[i:i + ATT_HEAD_GROUP]
        for h in range(2):
            scores = phase_scores(items, h)
            if pending is not None:
                phase_update(*pending)
            pending = (items, h, scores)
    phase_update(*pending)

    for i in range(SEQ // ch):
        slab = gather(fin, [(r * CLS_LEN + per * i, per) for r in range(N_CLS)])
        o_ref[0, i * ch:(i + 1) * ch, lanes] = jnp.dot(
            shuffle, slab.astype(BF16), preferred_element_type=F32).astype(o_ref.dtype)


def _attention(q16, k16, v16):
    assert CLS_LEN == ATT_CHUNK and SEQ % (4 * ATT_CHUNK) == 0
    width = ATT_PAIRS * LANES
    blk = pl.BlockSpec((1, SEQ, width), lambda b, h: (b, 0, h))
    cls_blk = pl.BlockSpec((1, N_CLS, CLS_LEN, width), lambda b, h: (b, 0, 0, h))
    f32_scr = pltpu.VMEM((SEQ, LANES), F32)
    masks = _attn_masks()
    return pl.pallas_call(
        _attn_kernel,
        grid=(BATCH, ATT_WIDTH // width),
        in_specs=[cls_blk, cls_blk, cls_blk, _const_spec(masks.shape)],
        out_specs=blk,
        out_shape=jax.ShapeDtypeStruct((BATCH, SEQ, ATT_WIDTH), BF16),
        scratch_shapes=[f32_scr] * (5 * ATT_PAIRS),
        compiler_params=pltpu.CompilerParams(
            dimension_semantics=("arbitrary", "arbitrary"), vmem_limit_bytes=VMEM_LIMIT),
        name="dilated_attention",
    )(q16, k16, v16, masks)


def _ssm_kernel(us_ref, perm_ref, lam_re_ref, lam_im_ref, wb_ref, wc_re_ref, wc_im_ref,
                d_ref, gw_ref, gb_ref, o_ref, st_re, st_im, *scratch):
    @pl.when(pl.program_id(0) == 0)
    def _():
        st_re[...] = jnp.zeros_like(st_re)
        st_im[...] = jnp.zeros_like(st_im)

    n_sub = SSM_TC // SSM_SUB
    per_sub = len(scratch) // n_sub
    for sub in range(n_sub):
        _ssm_sub_chunk(sub * SSM_SUB, us_ref, perm_ref, lam_re_ref, lam_im_ref, wb_ref, wc_re_ref,
                       wc_im_ref, d_ref, gw_ref, gb_ref, o_ref, st_re, st_im,
                       *scratch[sub * per_sub:(sub + 1) * per_sub])


def _ssm_sub_chunk(t0, us_ref, perm_ref, lam_re_ref, lam_im_ref, wb_ref, wc_re_ref, wc_im_ref,
                   d_ref, gw_ref, gb_ref, o_ref, st_re, st_im, xr, xi, y_tb, y_bt):
    tc = SSM_SUB
    rows = tc * BATCH
    n_chunk = SSM_WIDTH // LANES
    cpc = N_STATE // n_chunk

    ub = us_ref[:, t0:t0 + tc, :SSM_WIDTH].reshape(rows, SSM_WIDTH)
    u_tb = jnp.dot(perm_ref[...], ub, preferred_element_type=F32).astype(BF16)
    def b_proj(c):
        cols = slice(c * cpc, (c + 1) * cpc)
        bu = jnp.dot(u_tb[:, c * LANES:(c + 1) * LANES], wb_ref[c], preferred_element_type=F32)
        xr[:, cols] = bu[:, :cpc]
        xi[:, cols] = bu[:, cpc:]

    def recurrence(c):
        cols = slice(c * cpc, (c + 1) * cpc)
        lr = lam_re_ref[:, cols]
        li = lam_im_ref[:, cols]
        sr = st_re[:, cols]
        si = st_im[:, cols]
        for t in range(tc):
            trow = slice(t * BATCH, (t + 1) * BATCH)
            sr, si = (lr * sr - li * si + xr[trow, cols], lr * si + li * sr + xi[trow, cols])
            xr[trow, cols] = sr
            xi[trow, cols] = si
        st_re[:, cols] = sr
        st_im[:, cols] = si

    def c_proj(c):
        cols = slice(c * cpc, (c + 1) * cpc)
        y_tb[c] = (
            jnp.dot(xr[:, cols].astype(BF16), wc_re_ref[c], preferred_element_type=F32)
            + jnp.dot(xi[:, cols].astype(BF16), wc_im_ref[c], preferred_element_type=F32))

    def skip_gelu(c):
        lanes = slice(c * LANES, (c + 1) * LANES)
        y_c = jnp.concatenate(
            [y_tb[c, pl.ds(b, tc, stride=BATCH), :] for b in range(BATCH)], axis=0)
        y_bt[:, lanes] = jax.nn.gelu(y_c + d_ref[:, lanes] * ub[:, lanes].astype(F32))

    b_proj(0)
    gate = _silu(us_ref[:, t0:t0 + tc, SSM_WIDTH:].reshape(rows, SSM_WIDTH).astype(F32))
    for c in range(n_chunk):
        if c + 1 < n_chunk:
            b_proj(c + 1)
        recurrence(c)
        if c > 0:
            skip_gelu(c - 1)
        c_proj(c)
    skip_gelu(n_chunk - 1)
    y = y_bt[...]
    z = jnp.dot(y.astype(BF16), gw_ref[...], preferred_element_type=F32) + gb_ref[...]
    out = y * _sigmoid(z) * gate
    o_ref[:, t0:t0 + tc, :] = out.reshape(BATCH, tc, SSM_WIDTH).astype(o_ref.dtype)


def _ssm(us3, params, i):
    tc = SSM_TC
    rows = SSM_SUB * BATCH
    dst_row = np.arange(rows)
    perm = jnp.asarray((dst_row[:, None] % BATCH) * SSM_SUB + dst_row[:, None] // BATCH
                       == dst_row[None, :], dtype=BF16)
    sub_scratch = [pltpu.VMEM((rows, N_STATE), F32), pltpu.VMEM((rows, N_STATE), F32),
                   pltpu.VMEM((SSM_WIDTH // LANES, rows, LANES), F32),
                   pltpu.VMEM((rows, SSM_WIDTH), F32)]
    return pl.pallas_call(
        _ssm_kernel,
        grid=(SEQ // tc,),
        in_specs=[pl.BlockSpec((BATCH, tc, 2 * SSM_WIDTH), lambda t: (0, t, 0)),
                  _const_spec((rows, rows))] + [_layer_spec(p, i) for p in params],
        out_specs=pl.BlockSpec((BATCH, tc, SSM_WIDTH), lambda t: (0, t, 0)),
        out_shape=jax.ShapeDtypeStruct((BATCH, SEQ, SSM_WIDTH), BF16),
        scratch_shapes=[pltpu.VMEM((BATCH, N_STATE), F32), pltpu.VMEM((BATCH, N_STATE), F32)]
        + sub_scratch * (SSM_TC // SSM_SUB),
        compiler_params=pltpu.CompilerParams(
            dimension_semantics=("arbitrary",), vmem_limit_bytes=VMEM_LIMIT),
        name="s5_branch",
    )(us3, perm, *params)


def _tail_kernel(layer_idx, x_ref, att_ref, g_ref, ssm_ref, w_eo_hbm, post_e_ref,
                 pre_ref, w_in_hbm, pw_hbm, ps_ref, w_out_hbm, post_ref, o_ref,
                 ext, tmp0, tmp1,
                 w_eo_ref, stage_eo, sem_eo, w_in_ref, stage_in, sem_in,
                 pw_ref, stage_pw, sem_pw, w_out_ref, stage_out, sem_out):
    tm = ODD_TILE
    hist = POOL_HIST
    total = hist + tm
    j = pl.program_id(1)

    @pl.when(_first_step())
    def _():
        _load_weights_bf16(layer_idx, [(w_eo_hbm, w_eo_ref, stage_eo, sem_eo),
                                       (w_in_hbm, w_in_ref, stage_in, sem_in),
                                       (pw_hbm, pw_ref, stage_pw, sem_pw),
                                       (w_out_hbm, w_out_ref, stage_out, sem_out)])

    @pl.when(j == 0)
    def _():
        ext[0:hist, :] = jnp.zeros((hist, POOL_WIDTH), F32)

    tmps = (tmp0, tmp1)
    sub = ODD_SUB
    for s in range(tm // sub):
        lo = hist + s * sub
        hi = lo + sub
        rows = slice(s * sub, (s + 1) * sub)
        half = sub // 2
        y_even = []
        for h in range(2):
            r = slice(s * sub + h * half, s * sub + (h + 1) * half)
            gated = (att_ref[r, :].astype(F32) * _silu(g_ref[r, :].astype(F32))).astype(BF16)
            y_even.append(
                jnp.dot(gated, w_eo_ref[:ATT_WIDTH, :], preferred_element_type=F32)
                + jnp.dot(ssm_ref[r, :], w_eo_ref[ATT_WIDTH:, :], preferred_element_type=F32))
        x_parts, hb_parts = [], []
        for h in range(2):
            r = slice(s * sub + h * half, s * sub + (h + 1) * half)
            x_h = x_ref[r, :] + _rms(y_even[h], post_e_ref[...])
            hb_h = _rms(x_h, pre_ref[...]).astype(BF16)
            ext[lo + h * half:lo + (h + 1) * half, :] = jnp.dot(
                hb_h, w_in_ref[:, :POOL_WIDTH], preferred_element_type=F32)
            x_parts.append(x_h)
            hb_parts.append(hb_h)
        x = jnp.concatenate(x_parts, axis=0)
        hb = jnp.concatenate(hb_parts, axis=0)
        t_idx = j * tm + s * sub + lax.broadcasted_iota(jnp.int32, (sub, 1), 0)
        inv_count = 1.0 / (t_idx + 1).astype(F32)
        y = None
        for g, w in enumerate(POOL_WINDOWS):
            cols = slice(g * POOL_GROUP, (g + 1) * POOL_GROUP)
            levels = int(math.log2(w))
            src_ref, src_cols = ext, cols
            for lev in range(levels):
                shift = 2 ** lev
                start = lo - 8 * (levels - 1 - lev)
                val = (src_ref[start:hi, src_cols] + src_ref[start - shift:hi - shift, src_cols])
                if lev < levels - 1:
                    dst = tmps[lev % 2]
                    dst[start:hi, :] = val
                    src_ref, src_cols = dst, slice(None)
            u_g = ext[lo:hi, cols]
            mixed = val * jnp.maximum(inv_count, 1.0 / w) - u_g
            yg = jnp.dot(mixed.astype(BF16), pw_ref[cols, :], preferred_element_type=F32)
            gate = jnp.dot(hb, w_in_ref[:, POOL_WIDTH + g * POOL_GROUP:
                                        POOL_WIDTH + (g + 1) * POOL_GROUP],
                           preferred_element_type=F32)
            yg = yg * ps_ref[:, cols] * _silu(gate)
            part = jnp.dot(yg.astype(BF16), w_out_ref[cols, :], preferred_element_type=F32)
            y = part if y is None else y + part
        o_ref[rows, :] = x + _rms(y, post_ref[...])
    ext[0:hist, :] = ext[tm:total, :]


def _even_tail_odd(x2, att2, g2, ssm2, w_even_out, pre, post, w_in, pool_w, pool_scale, w_out, i,
                   layer):
    tm = ODD_TILE
    nj = SEQ // tm
    row_spec = pl.BlockSpec((tm, D_MODEL), lambda b, j: (b * nj + j, 0))
    att_spec = pl.BlockSpec((tm, ATT_WIDTH), lambda b, j: (b * nj + j, 0))
    hbm = pl.BlockSpec(memory_space=pl.ANY)
    return pl.pallas_call(
        functools.partial(_tail_kernel, i),
        grid=(BATCH, nj),
        in_specs=[row_spec, att_spec, att_spec,
                  pl.BlockSpec((tm, SSM_WIDTH), lambda b, j: (b * nj + j, 0)),
                  hbm, _layer_spec(post, layer),
                  _layer_spec(pre, layer + 1), hbm,
                  hbm, _layer_spec(pool_scale, i),
                  hbm, _layer_spec(post, layer + 1)],
        out_specs=row_spec,
        out_shape=jax.ShapeDtypeStruct((BATCH * SEQ, D_MODEL), F32),
        scratch_shapes=[pltpu.VMEM((POOL_HIST + tm, POOL_WIDTH), F32),
                        pltpu.VMEM((POOL_HIST + tm, POOL_GROUP), F32),
                        pltpu.VMEM((POOL_HIST + tm, POOL_GROUP), F32)]
        + _weight_scratch(*w_even_out.shape[1:]) + _weight_scratch(*w_in.shape[1:])
        + _weight_scratch(*pool_w.shape[1:]) + _weight_scratch(*w_out.shape[1:]),
        compiler_params=pltpu.CompilerParams(
            dimension_semantics=("arbitrary", "arbitrary"), vmem_limit_bytes=VMEM_LIMIT),
        name="even_tail_odd_layer",
    )(x2, att2, g2, ssm2, w_even_out, post, pre, w_in, pool_w, pool_scale, w_out, post)


def _rope_tables():
    half = ROT_DIM // 2
    inv_freq = (np.float32(ROPE_THETA)
                ** (-np.arange(0, ROT_DIM, 2, dtype=np.float32) / np.float32(ROT_DIM)))
    ang = np.arange(SEQ, dtype=np.float32)[:, None] * inv_freq[None, :].astype(np.float32)
    cos, sin = np.cos(ang).astype(np.float32), np.sin(ang).astype(np.float32)
    zeros = np.zeros((SEQ, HEAD_DIM - ROT_DIM), np.float32)
    zeros_h = np.zeros((SEQ, half), np.float32)
    cos_h = np.concatenate([cos, cos, np.ones_like(zeros)], axis=1)
    sa_h = np.concatenate([-sin, zeros_h, zeros], axis=1)
    sb_h = np.concatenate([zeros_h, sin, zeros], axis=1)
    rep = LANES // HEAD_DIM
    pos = np.arange(SEQ).reshape(SEQ // ROW_TILE, ROW_TILE // N_CLS, N_CLS)
    order = pos.transpose(0, 2, 1).reshape(SEQ)
    return tuple(jnp.asarray(np.tile(t, (1, rep))[order]) for t in (cos_h, sa_h, sb_h))


def _ssm_params(a_re, a_im, log_dt, b_re, b_im, c_re, c_im):
    n_layer = a_re.shape[0]
    lam = lax.complex(a_re.astype(F32), a_im.astype(F32))
    dt = jnp.exp(log_dt.astype(F32))[..., None]
    lam_bar = jnp.exp(lam * dt)
    b_bar = ((lam_bar - 1.0) / lam)[..., None] * lax.complex(b_re.astype(F32), b_im.astype(F32))
    lam_re = jnp.broadcast_to(jnp.real(lam_bar).reshape(n_layer, 1, N_STATE),
                              (n_layer, BATCH, N_STATE))
    lam_im = jnp.broadcast_to(jnp.imag(lam_bar).reshape(n_layer, 1, N_STATE),
                              (n_layer, BATCH, N_STATE))
    gpc = LANES // SSM_GROUP
    n_chunk = SSM_GROUPS // gpc
    eye = jnp.eye(gpc, dtype=F32)

    def block_diag_in(m):
        m = m.reshape(n_layer, n_chunk, gpc, SSM_STATE, SSM_GROUP)
        return jnp.einsum('lcgph,gk->lcghkp', m, eye).reshape(
            n_layer, n_chunk, gpc * SSM_GROUP, gpc * SSM_STATE)

    def block_diag_out(m):
        m = m.reshape(n_layer, n_chunk, gpc, SSM_GROUP, SSM_STATE)
        return jnp.einsum('lcghp,gk->lcgpkh', m, eye).reshape(
            n_layer, n_chunk, gpc * SSM_STATE, gpc * SSM_GROUP)

    wb = jnp.concatenate([block_diag_in(jnp.real(b_bar)), block_diag_in(jnp.imag(b_bar))],
                         axis=3).astype(BF16)
    wc_re = block_diag_out(c_re.astype(F32)).astype(BF16)
    wc_im = block_diag_out(-c_im.astype(F32)).astype(BF16)
    return lam_re, lam_im, wb, wc_re, wc_im


def kernel(x, pre_norm, post_norm, even_w_in, even_w_out, ssm_a_re, ssm_a_im, ssm_log_dt,
           ssm_b_re, ssm_b_im, ssm_c_re, ssm_c_im, ssm_d, ssm_glu_w, ssm_glu_b,
           odd_w_in, pool_w, pool_scale, odd_w_out):
    cos_t, sa_t, sb_t = _rope_tables()
    pre = pre_norm.astype(F32).reshape(DEPTH, 1, D_MODEL)
    post = post_norm.astype(F32).reshape(DEPTH, 1, D_MODEL)
    even_w_in_f = even_w_in.astype(F32)
    even_w_out_f = even_w_out.astype(F32)
    odd_w_in_f = odd_w_in.astype(F32)
    odd_w_out_f = odd_w_out.astype(F32)
    pool_w_f = pool_w.astype(F32).reshape(-1, POOL_WIDTH, POOL_GROUP)
    pool_scale3 = pool_scale.astype(F32).reshape(-1, 1, POOL_WIDTH)
    ssm_params = _ssm_params(ssm_a_re, ssm_a_im, ssm_log_dt, ssm_b_re, ssm_b_im,
                             ssm_c_re, ssm_c_im) + (
        ssm_d.astype(F32).reshape(-1, 1, SSM_WIDTH), ssm_glu_w.astype(BF16),
        ssm_glu_b.astype(F32).reshape(-1, 1, SSM_WIDTH))
    x2 = x.reshape(BATCH * SEQ, D_MODEL)
    assert DEPTH % 2 == 0
    for layer in range(0, DEPTH, 2):
        i = layer // 2
        q16, k16, v16, g, us = _even_in(x2, pre, layer, even_w_in_f, i, cos_t, sa_t, sb_t)
        att = _attention(q16, k16, v16)
        ssm = _ssm(us.reshape(BATCH, SEQ, 2 * SSM_WIDTH), ssm_params, i)
        x2 = _even_tail_odd(x2, att.reshape(BATCH * SEQ, ATT_WIDTH), g,
                            ssm.reshape(BATCH * SEQ, SSM_WIDTH), even_w_out_f, pre, post,
                            odd_w_in_f, pool_w_f, pool_scale3, odd_w_out_f, i, layer)
    return x2.reshape(BATCH, SEQ, D_MODEL)
```
